```python
import jax, jax.numpy as jnp
from jax import lax
import numpy as np

D_MODEL = 2048
BATCH = 2
SEQ = 8192
DEPTH = 2

PLE_DIM = 256
N_BRANCH = 3
BRANCH_WIDTH = 1024
SWA_Q_HEADS = 16
SWA_KV_HEADS = 4
SWA_HEAD_DIM = 64
SWA_WINDOW = 128
SWA_BLOCK = 128
DN_HEADS = 8
DN_HEAD_DIM = 128
DN_WIDTH = DN_HEADS * DN_HEAD_DIM
DN_CONV = 4
DN_CHUNK = 64
MLA_HEADS = 8
MLA_Q_LORA = 512
MLA_KV_LORA = 512
MLA_NOPE = 128
MLA_ROPE = 64
MLA_V = 128
MLA_QBLOCK = 128
ROPE_THETA = 10000.0
D_FF = 7168
N_EXPERTS = 8
TOP_K = 2
D_FF_EXPERT = 7168
MOE_BLOCK = 512
N_DENSE = (DEPTH + 1) // 2
N_MOE = DEPTH // 2
NORM_EPS = 1e-6
IN_SPLITS = (
    SWA_Q_HEADS * SWA_HEAD_DIM,
    SWA_KV_HEADS * SWA_HEAD_DIM,
    SWA_KV_HEADS * SWA_HEAD_DIM,
    3 * DN_WIDTH,
    DN_WIDTH,
    DN_HEADS,
    DN_HEADS,
    MLA_Q_LORA,
    MLA_KV_LORA,
    MLA_ROPE,
    N_BRANCH * D_MODEL,
)
IN_COLS = sum(IN_SPLITS)

kernel_name = "hybrid_swa_deltanet_mla_moe_block"


def rms_norm(x, gain):
    xf = x.astype(jnp.float32)
    inv = lax.rsqrt(jnp.mean(xf * xf, axis=-1, keepdims=True) + NORM_EPS)
    return (xf * inv * gain.astype(jnp.float32)).astype(x.dtype)


def l2_normalize(x):
    return x * lax.rsqrt(jnp.sum(x * x, axis=-1, keepdims=True) + 1e-6)


def rotary(x, positions):
    half = x.shape[-1] // 2
    inv_freq = ROPE_THETA ** (-jnp.arange(half, dtype=jnp.float32) / half)
    ang = positions.astype(jnp.float32)[:, :, None] * inv_freq
    cos = jnp.cos(ang)[:, :, None, :]
    sin = jnp.sin(ang)[:, :, None, :]
    xf = x.astype(jnp.float32)
    x1, x2 = xf[..., :half], xf[..., half:]
    return jnp.concatenate([x1 * cos - x2 * sin, x2 * cos + x1 * sin], axis=-1).astype(x.dtype)


def causal_depthwise_conv(x, w):
    width, channels = w.shape
    return lax.conv_general_dilated(
        x, w[:, None, :].astype(x.dtype), window_strides=(1,), padding=((width - 1, 0),),
        dimension_numbers=('NWC', 'WIO', 'NWC'), feature_group_count=channels)


def sliding_window_attention(q, k, v, sinks):
    B, S, Hq, dh = q.shape
    Hkv = k.shape[2]
    G = Hq // Hkv
    L = SWA_BLOCK
    NB = S // L
    qb = q.reshape(B, NB, L, Hkv, G, dh)

    def band(t):
        tp = jnp.pad(t, ((0, 0), (L, 0), (0, 0), (0, 0))).reshape(B, NB + 1, L, Hkv, dh)
        return jnp.concatenate([tp[:, :-1], tp[:, 1:]], axis=2)

    kb, vb = band(k), band(v)
    s = jnp.einsum('bnqhgd,bnkhd->bnhgqk', qb, kb,
                   preferred_element_type=jnp.float32) * (dh ** -0.5)
    qi = jnp.arange(L)[:, None]
    kj = jnp.arange(2 * L)[None, :]
    rel = qi + L - kj
    key_pos = jnp.arange(NB)[:, None, None] * L + kj - L
    valid = (rel >= 0) & (rel < SWA_WINDOW) & (key_pos >= 0)
    s = jnp.where(valid[None, :, None, None], s, -jnp.inf)
    sink = sinks.astype(jnp.float32).reshape(1, 1, Hkv, G, 1, 1)
    m = jnp.maximum(jnp.max(s, axis=-1, keepdims=True), sink)
    e = jnp.exp(s - m)
    probs = e / (jnp.sum(e, axis=-1, keepdims=True) + jnp.exp(sink - m))
    o = jnp.einsum('bnhgqk,bnkhd->bnqhgd', probs.astype(v.dtype), vb)
    return o.reshape(B, S, Hq * dh)


def gated_delta_rule(q, k, v, g, beta):
    B, S, H, dk = q.shape
    dv = v.shape[-1]
    C = DN_CHUNK
    N = S // C

    def chunk(t):
        t = t.reshape((B, N, C, H) + t.shape[3:])
        return jnp.moveaxis(t, (1, 3), (0, 2))

    qc, kc, vc = chunk(q), chunk(k), chunk(v)
    gc = jnp.cumsum(chunk(g), axis=-1)
    bc = chunk(beta)
    idx = jnp.arange(C)
    incl = idx[:, None] >= idx[None, :]
    strict = idx[:, None] > idx[None, :]
    decay = jnp.exp(jnp.where(incl, gc[..., :, None] - gc[..., None, :], -jnp.inf))
    k_beta = kc * bc[..., None]
    kk = jnp.einsum('nbhid,nbhjd->nbhij', k_beta, kc) * decay
    eye = jnp.eye(C, dtype=jnp.float32)
    a_mat = eye + jnp.where(strict, kk, 0.0)
    t_mat = lax.linalg.triangular_solve(a_mat, jnp.broadcast_to(eye, a_mat.shape),
                                        left_side=True, lower=True)
    u = t_mat @ (vc * bc[..., None])
    w = t_mat @ (k_beta * jnp.exp(gc)[..., None])
    qk = jnp.where(incl, jnp.einsum('nbhid,nbhjd->nbhij', qc, kc) * decay, 0.0)

    def step(state, inp):
        q_i, k_i, u_i, w_i, g_i, qk_i = inp
        v_new = u_i - w_i @ state
        o = (q_i * jnp.exp(g_i)[..., None]) @ state + qk_i @ v_new
        g_last = g_i[..., -1:]
        state = state * jnp.exp(g_last)[..., None] + jnp.einsum(
            'bhcd,bhce->bhde', k_i * jnp.exp(g_last - g_i)[..., None], v_new)
        return state, o

    state0 = jnp.zeros((B, H, dk, dv), jnp.float32)
    _, o = lax.scan(step, state0, (qc, kc, u, w, gc, qk))
    return jnp.moveaxis(o, (0, 2), (1, 3)).reshape(B, S, H, dv)


def causal_block_attention(q, k, v):
    B, S, H, dq = q.shape
    Lq = MLA_QBLOCK
    NB = S // Lq
    scale = dq ** -0.5
    qb = jnp.moveaxis(q.reshape(B, NB, Lq, H, dq), 1, 0)
    key_pos = jnp.arange(S)

    def one_block(args):
        q_blk, n = args
        s = jnp.einsum('bqhd,bkhd->bhqk', q_blk, k, preferred_element_type=jnp.float32) * scale
        q_pos = n * Lq + jnp.arange(Lq)
        s = jnp.where(key_pos[None, :] <= q_pos[:, None], s, -jnp.inf)
        probs = jax.nn.softmax(s, axis=-1)
        return jnp.einsum('bhqk,bkhd->bqhd', probs.astype(v.dtype), v)

    o = lax.map(one_block, (qb, jnp.arange(NB)))
    return jnp.moveaxis(o, 0, 1).reshape(B, S, H * v.shape[-1])


def hybrid_mixer(h, positions, w_in, conv_w, dn_a_log, dn_dt_bias, dn_norm, swa_sinks,
                 mla_q_norm, w_uq, mla_kv_norm, w_ukv, w_branch, w_out):
    B, S, _ = h.shape
    proj = h @ w_in
    cuts = np.cumsum(IN_SPLITS)[:-1].tolist()
    (a_q, a_k, a_v, b_qkv, b_z, b_beta, b_decay,
     c_q, c_kv, c_kr, gate_logits) = jnp.split(proj, cuts, axis=-1)

    y_a = sliding_window_attention(
        a_q.reshape(B, S, SWA_Q_HEADS, SWA_HEAD_DIM),
        a_k.reshape(B, S, SWA_KV_HEADS, SWA_HEAD_DIM),
        a_v.reshape(B, S, SWA_KV_HEADS, SWA_HEAD_DIM), swa_sinks)

    qkv = jax.nn.silu(causal_depthwise_conv(b_qkv, conv_w))
    dq, dk_, dv_ = jnp.split(qkv, 3, axis=-1)
    dq = l2_normalize(dq.reshape(B, S, DN_HEADS, DN_HEAD_DIM).astype(jnp.float32)) * (DN_HEAD_DIM ** -0.5)
    dk_ = l2_normalize(dk_.reshape(B, S, DN_HEADS, DN_HEAD_DIM).astype(jnp.float32))
    dv_ = dv_.reshape(B, S, DN_HEADS, DN_HEAD_DIM).astype(jnp.float32)
    beta = jax.nn.sigmoid(b_beta.astype(jnp.float32))
    g = -jnp.exp(dn_a_log.astype(jnp.float32)) * jax.nn.softplus(
        b_decay.astype(jnp.float32) + dn_dt_bias.astype(jnp.float32))
    o_b = gated_delta_rule(dq, dk_, dv_, g, beta)
    z = b_z.reshape(B, S, DN_HEADS, DN_HEAD_DIM)
    y_b = (rms_norm(o_b, dn_norm).astype(h.dtype) * jax.nn.silu(z)).reshape(B, S, DN_WIDTH)

    q_c = (rms_norm(c_q, mla_q_norm) @ w_uq).reshape(B, S, MLA_HEADS, MLA_NOPE + MLA_ROPE)
    q_c = jnp.concatenate([q_c[..., :MLA_NOPE], rotary(q_c[..., MLA_NOPE:], positions)], axis=-1)
    kv = (rms_norm(c_kv, mla_kv_norm) @ w_ukv).reshape(B, S, MLA_HEADS, MLA_NOPE + MLA_V)
    k_nope, v_c = kv[..., :MLA_NOPE], kv[..., MLA_NOPE:]
    k_rope = rotary(c_kr[:, :, None, :], positions)
    k_c = jnp.concatenate([k_nope, jnp.broadcast_to(k_rope, (B, S, MLA_HEADS, MLA_ROPE))], axis=-1)
    y_c = causal_block_attention(q_c, k_c, v_c)

    branches = jnp.stack([y_a, y_b, y_c], axis=2)
    branch_d = jnp.einsum('bsnw,nwd->bsnd', branches, w_branch)
    gates = jax.nn.sigmoid(gate_logits.reshape(B, S, N_BRANCH, D_MODEL))
    merged = jnp.sum(gates * branch_d, axis=2)
    return merged @ w_out


def swiglu(h, w_gate, w_up, w_down):
    return (jax.nn.silu(h @ w_gate) * (h @ w_up)) @ w_down


def moe_swiglu(h, w_router, w_gate, w_up, w_down):
    B, S, D = h.shape
    T = B * S
    hf = h.reshape(T, D)
    logits = jnp.dot(hf, w_router, preferred_element_type=jnp.float32)
    top_logit, top_idx = lax.top_k(logits, TOP_K)
    top_w = jax.nn.softmax(top_logit, axis=-1)
    flat_e = top_idx.reshape(-1)
    flat_tok = jnp.repeat(jnp.arange(T, dtype=jnp.int32), TOP_K)
    flat_w = top_w.reshape(-1)
    order = jnp.argsort(flat_e)
    e_sorted, tok_sorted, w_sorted = flat_e[order], flat_tok[order], flat_w[order]
    counts = jnp.bincount(flat_e, length=N_EXPERTS)
    padded = (counts + MOE_BLOCK - 1) // MOE_BLOCK * MOE_BLOCK
    start = jnp.cumsum(counts) - counts
    pad_end = jnp.cumsum(padded)
    pad_start = pad_end - padded
    dest = pad_start[e_sorted] + jnp.arange(T * TOP_K) - start[e_sorted]
    n_rows = -(-(T * TOP_K + N_EXPERTS * (MOE_BLOCK - 1)) // MOE_BLOCK) * MOE_BLOCK
    n_blocks = n_rows // MOE_BLOCK
    row_tok = jnp.zeros((n_rows,), jnp.int32).at[dest].set(tok_sorted)
    row_w = jnp.zeros((n_rows,), jnp.float32).at[dest].set(w_sorted)
    block_e = jnp.minimum(
        jnp.sum(jnp.arange(n_blocks)[:, None] * MOE_BLOCK >= pad_end[None, :], axis=1),
        N_EXPERTS - 1)
    xb = hf[row_tok].reshape(n_blocks, MOE_BLOCK, D)

    def expert_block(args):
        x_blk, e = args
        return (jax.nn.silu(x_blk @ w_gate[e]) * (x_blk @ w_up[e])) @ w_down[e]

    yb = lax.map(expert_block, (xb, block_e)).reshape(n_rows, D)
    y = jnp.zeros((T, D), h.dtype).at[row_tok].add(yb * row_w[:, None].astype(h.dtype))
    return y.reshape(B, S, D)


def setup_inputs(seed: int = 0) -> dict:
    key = jax.random.key(seed)
    ks = iter(jax.random.split(key, 40))
    f32 = jnp.float32

    def nrm(shape, scale):
        return jax.random.normal(next(ks), shape, f32) * scale

    def gain(shape):
        return 1.0 + 0.02 * jax.random.normal(next(ks), shape, f32)

    x = nrm((BATCH, SEQ, D_MODEL), 1.0)
    p = nrm((DEPTH, BATCH, SEQ, PLE_DIM), 1.0)
    offset = jax.random.randint(next(ks), (BATCH, 1), 0, 1024, dtype=jnp.int32)
    positions = offset + jnp.arange(SEQ, dtype=jnp.int32)[None, :]
    norm_mix = gain((DEPTH, D_MODEL))
    w_in = nrm((DEPTH, D_MODEL, IN_COLS), D_MODEL ** -0.5)
    conv_w = nrm((DEPTH, DN_CONV, 3 * DN_WIDTH), DN_CONV ** -0.5)
    dn_a_log = jnp.log(jax.random.uniform(next(ks), (DEPTH, DN_HEADS), f32, 1.0, 16.0))
    dt = jnp.exp(jax.random.uniform(next(ks), (DEPTH, DN_HEADS), f32,
                                    float(np.log(1e-3)), float(np.log(1e-1))))
    dn_dt_bias = dt + jnp.log(-jnp.expm1(-dt))
    dn_norm = gain((DEPTH, DN_HEAD_DIM))
    swa_sinks = nrm((DEPTH, SWA_Q_HEADS), 1.0)
    mla_q_norm = gain((DEPTH, MLA_Q_LORA))
    w_uq = nrm((DEPTH, MLA_Q_LORA, MLA_HEADS * (MLA_NOPE + MLA_ROPE)), MLA_Q_LORA ** -0.5)
    mla_kv_norm = gain((DEPTH, MLA_KV_LORA))
    w_ukv = nrm((DEPTH, MLA_KV_LORA, MLA_HEADS * (MLA_NOPE + MLA_V)), MLA_KV_LORA ** -0.5)
    w_branch = nrm((DEPTH, N_BRANCH, BRANCH_WIDTH, D_MODEL), BRANCH_WIDTH ** -0.5)
    w_out = nrm((DEPTH, D_MODEL, D_MODEL), D_MODEL ** -0.5)
    norm_ffn = gain((DEPTH, D_MODEL))
    w_ffn_gate = nrm((N_DENSE, D_MODEL, D_FF), D_MODEL ** -0.5)
    w_ffn_up = nrm((N_DENSE, D_MODEL, D_FF), D_MODEL ** -0.5)
    w_ffn_down = nrm((N_DENSE, D_FF, D_MODEL), D_FF ** -0.5)
    w_router = nrm((N_MOE, D_MODEL, N_EXPERTS), D_MODEL ** -0.5)
    w_exp_gate = nrm((N_MOE, N_EXPERTS, D_MODEL, D_FF_EXPERT), D_MODEL ** -0.5)
    w_exp_up = nrm((N_MOE, N_EXPERTS, D_MODEL, D_FF_EXPERT), D_MODEL ** -0.5)
    w_exp_down = nrm((N_MOE, N_EXPERTS, D_FF_EXPERT, D_MODEL), D_FF_EXPERT ** -0.5)
    norm_ple = gain((DEPTH, D_MODEL))
    w_ple_gate = nrm((DEPTH, D_MODEL, D_MODEL), D_MODEL ** -0.5)
    w_ple_proj = nrm((DEPTH, PLE_DIM, D_MODEL), PLE_DIM ** -0.5)
    final_norm = gain((D_MODEL,))
    return {
        "x": x, "p": p, "positions": positions,
        "norm_mix": norm_mix, "w_in": w_in, "conv_w": conv_w,
        "dn_a_log": dn_a_log, "dn_dt_bias": dn_dt_bias, "dn_norm": dn_norm,
        "swa_sinks": swa_sinks, "mla_q_norm": mla_q_norm, "w_uq": w_uq,
        "mla_kv_norm": mla_kv_norm, "w_ukv": w_ukv, "w_branch": w_branch, "w_out": w_out,
        "norm_ffn": norm_ffn, "w_ffn_gate": w_ffn_gate, "w_ffn_up": w_ffn_up,
        "w_ffn_down": w_ffn_down, "w_router": w_router, "w_exp_gate": w_exp_gate,
        "w_exp_up": w_exp_up, "w_exp_down": w_exp_down, "norm_ple": norm_ple,
        "w_ple_gate": w_ple_gate, "w_ple_proj": w_ple_proj, "final_norm": final_norm,
    }


def reference(x, p, positions, norm_mix, w_in, conv_w, dn_a_log, dn_dt_bias, dn_norm,
              swa_sinks, mla_q_norm, w_uq, mla_kv_norm, w_ukv, w_branch, w_out,
              norm_ffn, w_ffn_gate, w_ffn_up, w_ffn_down, w_router, w_exp_gate,
              w_exp_up, w_exp_down, norm_ple, w_ple_gate, w_ple_proj, final_norm):
    for i in range(DEPTH):
        h = rms_norm(x, norm_mix[i])
        x = x + hybrid_mixer(h, positions, w_in[i], conv_w[i], dn_a_log[i], dn_dt_bias[i],
                             dn_norm[i], swa_sinks[i], mla_q_norm[i], w_uq[i],
                             mla_kv_norm[i], w_ukv[i], w_branch[i], w_out[i])
        h = rms_norm(x, norm_ffn[i])
        j = i // 2
        if i % 2 == 0:
            x = x + swiglu(h, w_ffn_gate[j], w_ffn_up[j], w_ffn_down[j])
        else:
            x = x + moe_swiglu(h, w_router[j], w_exp_gate[j], w_exp_up[j], w_exp_down[j])
        hp = rms_norm(x, norm_ple[i])
        x = x + (p[i] @ w_ple_proj[i]) * jax.nn.sigmoid(hp @ w_ple_gate[i])
    return rms_norm(x, final_norm)
```

```python
import functools

import jax
import jax.numpy as jnp
import numpy as np
from jax import lax
from jax.experimental import pallas as pl
from jax.experimental.pallas import tpu as pltpu

BF16 = jnp.bfloat16
F32 = jnp.float32
NEG_INF = float("-inf")

NORM_EPS = 1e-6
SWA_Q_HEADS, SWA_KV_HEADS, SWA_HEAD_DIM, SWA_WINDOW = 16, 4, 64, 128
DN_HEADS, DN_HEAD_DIM, DN_CONV, DN_CHUNK = 8, 128, 4, 64
MLA_HEADS, MLA_NOPE, MLA_ROPE, MLA_V = 8, 128, 64, 128
ROPE_THETA = 10000.0
N_EXPERTS, TOP_K = 8, 2

LANES = 128
VMEM_LIMIT_BYTES = 56 * 1024 * 1024


def _cparams(*semantics):
    return pltpu.CompilerParams(dimension_semantics=semantics, vmem_limit_bytes=VMEM_LIMIT_BYTES)


def _tile(n, pref):
    t = min(n, pref)
    while n % t:
        t //= 2
    return t


def _dot(a, b):
    return jnp.dot(a, b, preferred_element_type=F32)


def _dot_nt(a, b):
    return lax.dot_general(a, b, (((1,), (1,)), ((), ())), preferred_element_type=F32)


def _dot_tn(a, b):
    return lax.dot_general(a, b, (((0,), (0,)), ((), ())), preferred_element_type=F32)


def _rms(x, gain):
    inv = lax.rsqrt(jnp.mean(x * x, axis=-1, keepdims=True) + NORM_EPS)
    return x * inv * gain


def _sigmoid(x):
    return 1.0 / (1.0 + jnp.exp(-x))


def _rmsnorm_kernel(x_ref, g_ref, o_ref):
    o_ref[...] = _rms(x_ref[...], g_ref[...]).astype(o_ref.dtype)


def rmsnorm_rows(x, gains, layer, out_dtype):
    T, D = x.shape
    tm = _tile(T, 512)
    return pl.pallas_call(
        _rmsnorm_kernel,
        grid=(T // tm,),
        in_specs=[pl.BlockSpec((tm, D), lambda i: (i, 0)),
                  pl.BlockSpec((None, 1, D), lambda i: (layer, 0, 0))],
        out_specs=pl.BlockSpec((tm, D), lambda i: (i, 0)),
        out_shape=jax.ShapeDtypeStruct((T, D), out_dtype),
        compiler_params=_cparams("parallel"),
        name="rmsnorm_rows",
    )(x, gains)


def _matmul_ws_kernel(x_ref, w_ref, o_ref, *scratch, cast, act):
    if cast:
        (wb_ref,) = scratch

        @pl.when(pl.program_id(1) == 0)
        def _():
            wb_ref[...] = w_ref[...].astype(BF16)

        w = wb_ref[...]
    else:
        w = w_ref[...]
    acc = _dot(x_ref[...], w)
    if act == "sigmoid":
        acc = _sigmoid(acc)
    o_ref[...] = acc.astype(o_ref.dtype)


def matmul_ws(x, w, layer, *, n_cols, col_block_offset, tn, out_dtype, act=None, tm_pref=1024):
    M, K = x.shape
    tm = _tile(M, tm_pref)
    assert n_cols % tn == 0
    cast = w.dtype != BF16
    scratch = [pltpu.VMEM((K, tn), BF16)] if cast else []
    return pl.pallas_call(
        functools.partial(_matmul_ws_kernel, cast=cast, act=act),
        grid=(n_cols // tn, M // tm),
        in_specs=[pl.BlockSpec((tm, K), lambda j, i: (i, 0)),
                  pl.BlockSpec((None, K, tn), lambda j, i: (layer, 0, j + col_block_offset))],
        out_specs=pl.BlockSpec((tm, tn), lambda j, i: (i, j)),
        out_shape=jax.ShapeDtypeStruct((M, n_cols), out_dtype),
        scratch_shapes=scratch,
        compiler_params=_cparams("arbitrary", "arbitrary"),
        name="matmul_ws",
    )(x, w)


def _swa_kernel(sinks_ref, q_ref, kc_ref, vc_ref, kp_ref, vp_ref, o_ref, *, layer, tiles_per_seq, rows):
    L = SWA_WINDOW
    dh = SWA_HEAD_DIM
    G = SWA_Q_HEADS // SWA_KV_HEADS
    first = (pl.program_id(0) % tiles_per_seq) == 0
    kall = jnp.concatenate([kp_ref[...], kc_ref[...]], axis=0)
    vall = jnp.concatenate([vp_ref[...], vc_ref[...]], axis=0)
    qi = lax.broadcasted_iota(jnp.int32, (L, 2 * L), 0)
    kj = lax.broadcasted_iota(jnp.int32, (L, 2 * L), 1)
    rel = qi + L - kj
    band = jnp.logical_and(rel >= 0, rel < SWA_WINDOW)
    kj_min = jnp.where(first, L, 0)
    scale = dh ** -0.5
    for b in range(rows // L):
        valid = jnp.logical_and(band, kj >= kj_min) if b == 0 else band
        kw = kall[b * L:(b + 2) * L]
        vw = vall[b * L:(b + 2) * L]
        for h in range(SWA_KV_HEADS):
            kh = kw[:, h * dh:(h + 1) * dh]
            vh = vw[:, h * dh:(h + 1) * dh]
            for g in range(G):
                hq = h * G + g
                qg = q_ref[b * L:(b + 1) * L, hq * dh:(hq + 1) * dh]
                s = _dot_nt(qg, kh) * scale
                s = jnp.where(valid, s, NEG_INF)
                sink = sinks_ref[layer, hq]
                m = jnp.maximum(jnp.max(s, axis=-1, keepdims=True), sink)
                e = jnp.exp(s - m)
                denom = jnp.sum(e, axis=-1, keepdims=True) + jnp.exp(sink - m)
                p = (e / denom).astype(BF16)
                o_ref[b * L:(b + 1) * L, hq * dh:(hq + 1) * dh] = _dot(p, vh).astype(o_ref.dtype)


def swa_attention(qkv, sinks, layer, seq_len):
    T = qkv.shape[0]
    L = SWA_WINDOW
    rows = _tile(seq_len, 512)
    wq = SWA_Q_HEADS * SWA_HEAD_DIM
    wkv = SWA_KV_HEADS * SWA_HEAD_DIM
    kcol, vcol = wq // wkv, wq // wkv + 1
    rpl = rows // L

    def prev_map(col):
        return lambda i: (jnp.maximum(i * rpl - 1, 0), col)

    return pl.pallas_call(
        functools.partial(_swa_kernel, layer=layer, tiles_per_seq=seq_len // rows, rows=rows),
        grid=(T // rows,),
        in_specs=[pl.BlockSpec(memory_space=pltpu.SMEM),
                  pl.BlockSpec((rows, wq), lambda i: (i, 0)),
                  pl.BlockSpec((rows, wkv), lambda i: (i, kcol)),
                  pl.BlockSpec((rows, wkv), lambda i: (i, vcol)),
                  pl.BlockSpec((L, wkv), prev_map(kcol)),
                  pl.BlockSpec((L, wkv), prev_map(vcol))],
        out_specs=pl.BlockSpec((rows, wq), lambda i: (i, 0)),
        out_shape=jax.ShapeDtypeStruct((T, wq), BF16),
        compiler_params=_cparams("parallel"),
        name="swa_attention",
    )(sinks, qkv, qkv, qkv, qkv, qkv)


def _dn_prep_kernel(cur_ref, prev_ref, cw_ref, t2_ref, alog_ref, dtb_ref, qkv_ref, gb_ref, *, tiles_per_seq, tm):
    H, dh, W = DN_HEADS, DN_HEAD_DIM, DN_CONV
    first = (pl.program_id(0) % tiles_per_seq) == 0
    prev = prev_ref[...].astype(F32)
    prev = jnp.where(first, 0.0, prev)
    xcat = jnp.concatenate([prev, cur_ref[...].astype(F32)], axis=0)
    P = prev.shape[0]
    cw = cw_ref[...]
    y = None
    for j in range(W):
        term = xcat[P - (W - 1) + j:P - (W - 1) + j + tm] * cw[j:j + 1]
        y = term if y is None else y + term
    y = y * _sigmoid(y)
    width = H * dh
    for h in range(H):
        qh = y[:, h * dh:(h + 1) * dh]
        kh = y[:, width + h * dh:width + (h + 1) * dh]
        qn = qh * lax.rsqrt(jnp.sum(qh * qh, axis=-1, keepdims=True) + 1e-6) * (dh ** -0.5)
        kn = kh * lax.rsqrt(jnp.sum(kh * kh, axis=-1, keepdims=True) + 1e-6)
        qkv_ref[:, h * dh:(h + 1) * dh] = qn.astype(qkv_ref.dtype)
        qkv_ref[:, width + h * dh:width + (h + 1) * dh] = kn.astype(qkv_ref.dtype)
    qkv_ref[:, 2 * width:] = y[:, 2 * width:].astype(qkv_ref.dtype)

    t2 = t2_ref[...]
    xs = t2 + dtb_ref[...]
    softplus = jnp.maximum(xs, 0.0) + jnp.log(1.0 + jnp.exp(-jnp.abs(xs)))
    g = -jnp.exp(alog_ref[...]) * softplus
    ri = lax.broadcasted_iota(jnp.int32, (tm, tm), 0)
    ci = lax.broadcasted_iota(jnp.int32, (tm, tm), 1)
    same_chunk = (ri // DN_CHUNK) == (ci // DN_CHUNK)
    tri = jnp.where(jnp.logical_and(same_chunk, ri >= ci), 1.0, 0.0)
    gc = jnp.dot(tri, g, preferred_element_type=F32, precision=lax.Precision.HIGHEST)
    lane = lax.broadcasted_iota(jnp.int32, t2.shape, 1)
    gb_ref[...] = jnp.where(lane < H, _sigmoid(t2), gc)


def dn_prep(proj_b, proj_small, conv_w, alog_vec, dtb_vec, layer, seq_len):
    T = proj_b.shape[0]
    width3 = 3 * DN_HEADS * DN_HEAD_DIM
    tm = _tile(seq_len, 256)
    P = 16
    tail2_col = proj_small.shape[1] // LANES - 1
    return pl.pallas_call(
        functools.partial(_dn_prep_kernel, tiles_per_seq=seq_len // tm, tm=tm),
        grid=(T // tm,),
        in_specs=[pl.BlockSpec((tm, width3), lambda i: (i, 0)),
                  pl.BlockSpec((P, width3), lambda i: (jnp.maximum(i * (tm // P) - 1, 0), 0)),
                  pl.BlockSpec((None, DN_CONV, width3), lambda i: (layer, 0, 0)),
                  pl.BlockSpec((tm, LANES), lambda i: (i, tail2_col)),
                  pl.BlockSpec((None, 1, LANES), lambda i: (layer, 0, 0)),
                  pl.BlockSpec((None, 1, LANES), lambda i: (layer, 0, 0))],
        out_specs=[pl.BlockSpec((tm, width3), lambda i: (i, 0)),
                   pl.BlockSpec((tm, LANES), lambda i: (i, 0))],
        out_shape=[jax.ShapeDtypeStruct((T, width3), BF16),
                   jax.ShapeDtypeStruct((T, LANES), F32)],
        compiler_params=_cparams("parallel"),
        name="dn_prep",
    )(proj_b, proj_b, conv_w, proj_small, alog_vec, dtb_vec)


def _unit_lower_inverse(lm):
    C = lm.shape[0]
    ri = lax.broadcasted_iota(jnp.int32, (C, C), 0)
    ci = lax.broadcasted_iota(jnp.int32, (C, C), 1)
    t = jnp.where(ri == ci, 1.0, 0.0) - lm
    p = lm
    span = 2
    while span < C:
        p = _dot(p.astype(BF16), p.astype(BF16))
        t = t + _dot(t.astype(BF16), p.astype(BF16))
        span *= 2
    return t


def _dn_chunk_kernel(qkv_ref, z_ref, gb_ref, grow_ref, norm_ref, o_ref, state_ref, *, rows):
    H, dh, C = DN_HEADS, DN_HEAD_DIM, DN_CHUNK
    width = H * dh

    @pl.when(pl.program_id(1) == 0)
    def _():
        state_ref[...] = jnp.zeros_like(state_ref)

    ri = lax.broadcasted_iota(jnp.int32, (C, C), 0)
    ci = lax.broadcasted_iota(jnp.int32, (C, C), 1)
    incl = ri >= ci
    strict = ri > ci
    gain = norm_ref[...]

    def chunk(c, carry):
        r0 = pl.multiple_of(c * C, C)
        gb = gb_ref[pl.ds(r0, C), :]
        grow_all = grow_ref[c]
        for h in range(H):
            q = qkv_ref[pl.ds(r0, C), h * dh:(h + 1) * dh]
            k = qkv_ref[pl.ds(r0, C), width + h * dh:width + (h + 1) * dh]
            v = qkv_ref[pl.ds(r0, C), 2 * width + h * dh:2 * width + (h + 1) * dh]
            beta = gb[:, h:h + 1]
            gcol = gb[:, H + h:H + h + 1]
            grow = grow_all[h:h + 1, :]
            decay = jnp.exp(jnp.where(incl, gcol - grow, NEG_INF))
            kf = k.astype(F32)
            kbeta = kf * beta
            kk = _dot_nt(kbeta.astype(BF16), k) * decay
            tmat = _unit_lower_inverse(jnp.where(strict, kk, 0.0)).astype(BF16)
            eg = jnp.exp(gcol)
            u = _dot(tmat, (v.astype(F32) * beta).astype(BF16))
            w = _dot(tmat, (kbeta * eg).astype(BF16))
            qk = jnp.where(incl, _dot_nt(q, k) * decay, 0.0)
            state = state_ref[h]
            sb = state.astype(BF16)
            v_new = u - _dot(w.astype(BF16), sb)
            vnb = v_new.astype(BF16)
            o = _dot((q.astype(F32) * eg).astype(BF16), sb) + _dot(qk.astype(BF16), vnb)
            g_last = gcol[C - 1:C, :]
            kdec = (kf * jnp.exp(g_last - gcol)).astype(BF16)
            state_ref[h] = state * jnp.exp(g_last) + _dot_tn(kdec, vnb)
            z = z_ref[pl.ds(r0, C), h * dh:(h + 1) * dh].astype(F32)
            y = _rms(o, gain) * (z * _sigmoid(z))
            o_ref[pl.ds(r0, C), h * dh:(h + 1) * dh] = y.astype(o_ref.dtype)
        return carry

    lax.fori_loop(0, rows // C, chunk, 0)


def dn_chunked(qkvn, proj_b, gb, grow, dn_norm, layer, batch, seq_len):
    T = qkvn.shape[0]
    H, dh, C = DN_HEADS, DN_HEAD_DIM, DN_CHUNK
    width = H * dh
    rows = _tile(seq_len, 512)
    tps = seq_len // rows
    zcol = 3
    return pl.pallas_call(
        functools.partial(_dn_chunk_kernel, rows=rows),
        grid=(batch, tps),
        in_specs=[pl.BlockSpec((rows, 3 * width), lambda b, n: (b * tps + n, 0)),
                  pl.BlockSpec((rows, width), lambda b, n: (b * tps + n, zcol)),
                  pl.BlockSpec((rows, LANES), lambda b, n: (b * tps + n, 0)),
                  pl.BlockSpec((rows // C, H, C), lambda b, n: (b * tps + n, 0, 0)),
                  pl.BlockSpec((None, 1, dh), lambda b, n: (layer, 0, 0))],
        out_specs=pl.BlockSpec((rows, width), lambda b, n: (b * tps + n, 0)),
        out_shape=jax.ShapeDtypeStruct((T, width), BF16),
        scratch_shapes=[pltpu.VMEM((H, dh, dh), F32)],
        compiler_params=_cparams("parallel", "arbitrary"),
        name="dn_chunked",
    )(qkvn, proj_b, gb, grow, dn_norm)


def _rotate_pairs(x, cos_tab, sin_tab):
    return x * cos_tab + pltpu.roll(x, LANES // 2, axis=1) * sin_tab


def _mla_q_kernel(c_ref, g_ref, w_ref, cos_ref, sin_ref, o_ref):
    H = MLA_HEADS
    scale = (MLA_NOPE + MLA_ROPE) ** -0.5
    cn = _rms(c_ref[...], g_ref[...]).astype(BF16)
    acc = _dot(cn, w_ref[...]) * scale
    cos_tab, sin_tab = cos_ref[...], sin_ref[...]
    for h in range(H):
        o_ref[:, 2 * h * LANES:(2 * h + 1) * LANES] = acc[:, h * LANES:(h + 1) * LANES].astype(o_ref.dtype)
        xr = acc[:, (H + h) * LANES:(H + h + 1) * LANES]
        o_ref[:, (2 * h + 1) * LANES:(2 * h + 2) * LANES] = _rotate_pairs(xr, cos_tab, sin_tab).astype(o_ref.dtype)


def mla_q(proj_small, gains, wq, cos_tab, sin_tab, layer):
    T = proj_small.shape[0]
    R = wq.shape[1]
    tm = _tile(T, 512)
    N = wq.shape[2]
    return pl.pallas_call(
        _mla_q_kernel,
        grid=(T // tm,),
        in_specs=[pl.BlockSpec((tm, R), lambda i: (i, 0)),
                  pl.BlockSpec((None, 1, R), lambda i: (layer, 0, 0)),
                  pl.BlockSpec((None, R, N), lambda i: (layer, 0, 0)),
                  pl.BlockSpec((tm, LANES), lambda i: (i, 0)),
                  pl.BlockSpec((tm, LANES), lambda i: (i, 0))],
        out_specs=pl.BlockSpec((tm, N), lambda i: (i, 0)),
        out_shape=jax.ShapeDtypeStruct((T, N), BF16),
        compiler_params=_cparams("parallel"),
        name="mla_q",
    )(proj_small, gains, wq, cos_tab, sin_tab)


def _mla_kv_kernel(c_ref, kr_ref, g_ref, w_ref, cos_ref, sin_ref, k_ref, v_ref):
    H = MLA_HEADS
    cn = _rms(c_ref[...], g_ref[...]).astype(BF16)
    acc = _dot(cn, w_ref[...])
    kr = _rotate_pairs(kr_ref[...], cos_ref[...], sin_ref[...]).astype(k_ref.dtype)
    for h in range(H):
        k_ref[:, 2 * h * LANES:(2 * h + 1) * LANES] = acc[:, h * LANES:(h + 1) * LANES].astype(k_ref.dtype)
        k_ref[:, (2 * h + 1) * LANES:(2 * h + 2) * LANES] = kr
    v_ref[...] = acc[:, H * LANES:].astype(v_ref.dtype)


def mla_kv(proj_small, gains, wkv, cos_tab, sin_tab, layer):
    T = proj_small.shape[0]
    R = wkv.shape[1]
    tm = _tile(T, 512)
    N = wkv.shape[2]
    H = MLA_HEADS
    kr_col = (2 * R) // LANES
    return pl.pallas_call(
        _mla_kv_kernel,
        grid=(T // tm,),
        in_specs=[pl.BlockSpec((tm, R), lambda i: (i, 1)),
                  pl.BlockSpec((tm, LANES), lambda i: (i, kr_col)),
                  pl.BlockSpec((None, 1, R), lambda i: (layer, 0, 0)),
                  pl.BlockSpec((None, R, N), lambda i: (layer, 0, 0)),
                  pl.BlockSpec((tm, LANES), lambda i: (i, 0)),
                  pl.BlockSpec((tm, LANES), lambda i: (i, 0))],
        out_specs=[pl.BlockSpec((tm, 2 * H * LANES), lambda i: (i, 0)),
                   pl.BlockSpec((tm, H * MLA_V), lambda i: (i, 0))],
        out_shape=[jax.ShapeDtypeStruct((T, 2 * H * LANES), BF16),
                   jax.ShapeDtypeStruct((T, H * MLA_V), BF16)],
        compiler_params=_cparams("parallel"),
        name="mla_kv",
    )(proj_small, proj_small, gains, wkv, cos_tab, sin_tab)


def _flash_kernel(q_ref, k_ref, v_ref, o_ref, m_ref, l_ref, acc_ref, *, tq, tk):
    H, dv = MLA_HEADS, MLA_V
    dqk = 2 * LANES
    qi = pl.program_id(1)
    kj = pl.program_id(2)

    @pl.when(kj == 0)
    def _():
        m_ref[...] = jnp.full_like(m_ref, NEG_INF)
        l_ref[...] = jnp.zeros_like(l_ref)
        acc_ref[...] = jnp.zeros_like(acc_ref)

    def step(masked):
        if masked:
            row = lax.broadcasted_iota(jnp.int32, (tq, tk), 0)
            col = lax.broadcasted_iota(jnp.int32, (tq, tk), 1)
            keep = col <= row
        for h in range(H):
            s = _dot_nt(q_ref[:, h * dqk:(h + 1) * dqk], k_ref[:, h * dqk:(h + 1) * dqk])
            if masked:
                s = jnp.where(keep, s, NEG_INF)
            m_prev = m_ref[h][:, :1]
            m_new = jnp.maximum(m_prev, jnp.max(s, axis=-1, keepdims=True))
            alpha = jnp.exp(m_prev - m_new)
            p = jnp.exp(s - m_new)
            l_new = alpha * l_ref[h][:, :1] + jnp.sum(p, axis=-1, keepdims=True)
            acc_ref[h] = alpha * acc_ref[h] + _dot(p.astype(BF16), v_ref[:, h * dv:(h + 1) * dv])
            m_ref[h] = jnp.broadcast_to(m_new, (tq, LANES))
            l_ref[h] = jnp.broadcast_to(l_new, (tq, LANES))

    @pl.when(kj < qi)
    def _():
        step(False)

    @pl.when(kj == qi)
    def _():
        step(True)
        for h in range(H):
            o_ref[:, h * dv:(h + 1) * dv] = (acc_ref[h] / l_ref[h][:, :1]).astype(o_ref.dtype)


def mla_flash(qf, kf, v, batch, seq_len):
    T = qf.shape[0]
    H, dv = MLA_HEADS, MLA_V
    t = _tile(seq_len, 512)
    nq = seq_len // t
    return pl.pallas_call(
        functools.partial(_flash_kernel, tq=t, tk=t),
        grid=(batch, nq, nq),
        in_specs=[pl.BlockSpec((t, qf.shape[1]), lambda b, i, j: (b * nq + i, 0)),
                  pl.BlockSpec((t, kf.shape[1]), lambda b, i, j: (b * nq + jnp.minimum(j, i), 0)),
                  pl.BlockSpec((t, v.shape[1]), lambda b, i, j: (b * nq + jnp.minimum(j, i), 0))],
        out_specs=pl.BlockSpec((t, H * dv), lambda b, i, j: (b * nq + i, 0)),
        out_shape=jax.ShapeDtypeStruct((T, H * dv), BF16),
        scratch_shapes=[pltpu.VMEM((H, t, LANES), F32),
                        pltpu.VMEM((H, t, LANES), F32),
                        pltpu.VMEM((H, t, dv), F32)],
        compiler_params=_cparams("parallel", "parallel", "arbitrary"),
        name="mla_flash",
    )(qf, kf, v)


def _merge_kernel(ya_ref, yb_ref, yc_ref, ga_ref, gb_ref, gc_ref, wa_ref, wb_ref, wc_ref, o_ref, sa_ref, sb_ref, sc_ref):
    @pl.when(pl.program_id(1) == 0)
    def _():
        sa_ref[...] = wa_ref[...].astype(BF16)
        sb_ref[...] = wb_ref[...].astype(BF16)
        sc_ref[...] = wc_ref[...].astype(BF16)

    acc = ga_ref[...].astype(F32) * _dot(ya_ref[...], sa_ref[...])
    acc = acc + gb_ref[...].astype(F32) * _dot(yb_ref[...], sb_ref[...])
    acc = acc + gc_ref[...].astype(F32) * _dot(yc_ref[...], sc_ref[...])
    o_ref[...] = acc.astype(o_ref.dtype)


def merge_branches(ya, yb, yc, gates, w_branch, layer):
    T, Wb = ya.shape
    D = w_branch.shape[-1]
    tm = _tile(T, 1024)
    tn = _tile(D, 512)
    nj = D // tn
    y_spec = pl.BlockSpec((tm, Wb), lambda j, i: (i, 0))

    def g_spec(n):
        return pl.BlockSpec((tm, tn), lambda j, i: (i, n * nj + j))

    def w_spec(n):
        return pl.BlockSpec((None, None, Wb, tn), lambda j, i: (layer, n, 0, j))

    return pl.pallas_call(
        _merge_kernel,
        grid=(nj, T // tm),
        in_specs=[y_spec, y_spec, y_spec, g_spec(0), g_spec(1), g_spec(2), w_spec(0), w_spec(1), w_spec(2)],
        out_specs=pl.BlockSpec((tm, tn), lambda j, i: (i, j)),
        out_shape=jax.ShapeDtypeStruct((T, D), BF16),
        scratch_shapes=[pltpu.VMEM((Wb, tn), BF16)] * 3,
        compiler_params=_cparams("arbitrary", "arbitrary"),
        name="merge_branches",
    )(ya, yb, yc, gates, gates, gates, w_branch, w_branch, w_branch)


def _out_proj_kernel(m_ref, w_ref, x_ref, g_ref, xo_ref, ho_ref):
    xn = x_ref[...] + _dot(m_ref[...], w_ref[...])
    xo_ref[...] = xn
    ho_ref[...] = _rms(xn, g_ref[...]).astype(ho_ref.dtype)


def out_proj_residual(merged, w_out_bf, x, norm_gain, layer):
    T, D = x.shape
    tm = _tile(T, 512)
    return pl.pallas_call(
        _out_proj_kernel,
        grid=(T // tm,),
        in_specs=[pl.BlockSpec((tm, D), lambda i: (i, 0)),
                  pl.BlockSpec((None, D, D), lambda i: (layer, 0, 0)),
                  pl.BlockSpec((tm, D), lambda i: (i, 0)),
                  pl.BlockSpec((None, 1, D), lambda i: (layer, 0, 0))],
        out_specs=[pl.BlockSpec((tm, D), lambda i: (i, 0)),
                   pl.BlockSpec((tm, D), lambda i: (i, 0))],
        out_shape=[jax.ShapeDtypeStruct((T, D), F32), jax.ShapeDtypeStruct((T, D), BF16)],
        compiler_params=_cparams("parallel"),
        name="out_proj_residual",
    )(merged, w_out_bf, x, norm_gain)


def _ffn_kernel(h_ref, wg_ref, wu_ref, wd_ref, x_ref, g_ref, xo_ref, ho_ref, acc_ref):
    f = pl.program_id(1)

    @pl.when(f == 0)
    def _():
        acc_ref[...] = x_ref[...]

    h = h_ref[...]
    gate = _dot(h, wg_ref[...])
    up = _dot(h, wu_ref[...])
    act = (gate * _sigmoid(gate) * up).astype(BF16)
    acc_ref[...] += _dot(act, wd_ref[...])

    @pl.when(f == pl.num_programs(1) - 1)
    def _():
        xn = acc_ref[...]
        xo_ref[...] = xn
        ho_ref[...] = _rms(xn, g_ref[...]).astype(ho_ref.dtype)


def ffn_swiglu(h, wg, wu, wd, x, norm_gain, widx, layer):
    T, D = x.shape
    F = wg.shape[-1]
    tm = _tile(T, 512)
    tf = _tile(F, 512)
    return pl.pallas_call(
        _ffn_kernel,
        grid=(T // tm, F // tf),
        in_specs=[pl.BlockSpec((tm, D), lambda i, f: (i, 0)),
                  pl.BlockSpec((None, D, tf), lambda i, f: (widx, 0, f)),
                  pl.BlockSpec((None, D, tf), lambda i, f: (widx, 0, f)),
                  pl.BlockSpec((None, tf, D), lambda i, f: (widx, f, 0)),
                  pl.BlockSpec((tm, D), lambda i, f: (i, 0)),
                  pl.BlockSpec((None, 1, D), lambda i, f: (layer, 0, 0))],
        out_specs=[pl.BlockSpec((tm, D), lambda i, f: (i, 0)),
                   pl.BlockSpec((tm, D), lambda i, f: (i, 0))],
        out_shape=[jax.ShapeDtypeStruct((T, D), F32), jax.ShapeDtypeStruct((T, D), BF16)],
        scratch_shapes=[pltpu.VMEM((tm, D), F32)],
        compiler_params=_cparams("parallel", "arbitrary"),
        name="ffn_swiglu",
    )(h, wg, wu, wd, x, norm_gain)


def _router_kernel(x_ref, g_ref, w_ref, o_ref):
    h = _rms(x_ref[...], g_ref[...])
    logits = jnp.dot(h, w_ref[...], preferred_element_type=F32, precision=lax.Precision.HIGHEST)
    lane = lax.broadcasted_iota(jnp.int32, logits.shape, 1)
    lanef = lane.astype(F32)
    big = float(LANES)
    logits = jnp.where(lane < N_EXPERTS, logits, NEG_INF)
    l1 = jnp.max(logits, axis=-1, keepdims=True)
    i1 = jnp.min(jnp.where(logits == l1, lanef, big), axis=-1, keepdims=True)
    rest = jnp.where(lanef == i1, NEG_INF, logits)
    l2 = jnp.max(rest, axis=-1, keepdims=True)
    i2 = jnp.min(jnp.where(rest == l2, lanef, big), axis=-1, keepdims=True)
    e2 = jnp.exp(l2 - l1)
    w1 = 1.0 / (1.0 + e2)
    w2 = e2 / (1.0 + e2)
    out = jnp.where(lane == 0, i1, jnp.where(lane == 1, i2, jnp.where(lane == 2, w1, jnp.where(lane == 3, w2, 0.0))))
    o_ref[...] = out


def moe_router(x, norm_gain, w_router_pad, layer, widx):
    T, D = x.shape
    tm = _tile(T, 512)
    return pl.pallas_call(
        _router_kernel,
        grid=(T // tm,),
        in_specs=[pl.BlockSpec((tm, D), lambda i: (i, 0)),
                  pl.BlockSpec((None, 1, D), lambda i: (layer, 0, 0)),
                  pl.BlockSpec((None, D, LANES), lambda i: (widx, 0, 0))],
        out_specs=pl.BlockSpec((tm, LANES), lambda i: (i, 0)),
        out_shape=jax.ShapeDtypeStruct((T, LANES), F32),
        compiler_params=_cparams("parallel"),
        name="moe_router",
    )(x, norm_gain, w_router_pad)


def _gather_rows_kernel(idx_hbm, x_hbm, g_ref, o_ref, idx_smem, rows_vmem, idx_sem, row_sem, *, bm):
    blk = pl.program_id(0)
    idx_copy = pltpu.make_async_copy(idx_hbm.at[blk], idx_smem, idx_sem)
    idx_copy.start()
    idx_copy.wait()

    def row_copy(r):
        return pltpu.make_async_copy(x_hbm.at[pl.ds(idx_smem[r], 1)], rows_vmem.at[pl.ds(r, 1)], row_sem)

    def issue(r, c):
        row_copy(r).start()
        return c

    lax.fori_loop(0, bm, issue, 0)

    def drain(r, c):
        row_copy(r).wait()
        return c

    lax.fori_loop(0, bm, drain, 0)
    o_ref[...] = _rms(rows_vmem[...], g_ref[...]).astype(o_ref.dtype)


def moe_gather_norm(x, row_tok, norm_gain, layer, bm):
    T, D = x.shape
    n_blocks = row_tok.shape[0]
    return pl.pallas_call(
        functools.partial(_gather_rows_kernel, bm=bm),
        grid=(n_blocks,),
        in_specs=[pl.BlockSpec(memory_space=pl.ANY),
                  pl.BlockSpec(memory_space=pl.ANY),
                  pl.BlockSpec((None, 1, D), lambda i: (layer, 0, 0))],
        out_specs=pl.BlockSpec((bm, D), lambda i: (i, 0)),
        out_shape=jax.ShapeDtypeStruct((n_blocks * bm, D), BF16),
        scratch_shapes=[pltpu.SMEM((bm,), jnp.int32),
                        pltpu.VMEM((bm, D), F32),
                        pltpu.SemaphoreType.DMA(()),
                        pltpu.SemaphoreType.DMA(())],
        compiler_params=_cparams("arbitrary"),
        name="moe_gather_norm",
    )(row_tok, x, norm_gain)


def _expert_up_kernel(be_ref, x_ref, wg_ref, wu_ref, o_ref):
    x = x_ref[...]
    gate = _dot(x, wg_ref[...])
    up = _dot(x, wu_ref[...])
    o_ref[...] = (gate * _sigmoid(gate) * up).astype(o_ref.dtype)


def moe_expert_up(xs, block_e, wg, wu, widx, bm):
    R, D = xs.shape
    F = wg.shape[-1]
    tf = _tile(F, 512)
    nb = R // bm
    return pl.pallas_call(
        _expert_up_kernel,
        grid_spec=pltpu.PrefetchScalarGridSpec(
            num_scalar_prefetch=1,
            grid=(F // tf, nb),
            in_specs=[pl.BlockSpec((bm, D), lambda f, m, be: (m, 0)),
                      pl.BlockSpec((None, None, D, tf), lambda f, m, be: (widx, be[m], 0, f)),
                      pl.BlockSpec((None, None, D, tf), lambda f, m, be: (widx, be[m], 0, f))],
            out_specs=pl.BlockSpec((bm, tf), lambda f, m, be: (m, f))),
        out_shape=jax.ShapeDtypeStruct((R, F), BF16),
        compiler_params=_cparams("arbitrary", "arbitrary"),
        name="moe_expert_up",
    )(block_e, xs, wg, wu)


def _expert_down_kernel(be_ref, a_ref, wd_ref, o_ref):
    o_ref[...] = _dot(a_ref[...], wd_ref[...]).astype(o_ref.dtype)


def moe_expert_down(act, block_e, wd, widx, bm):
    R, F = act.shape
    D = wd.shape[-1]
    tn = _tile(D, 1024)
    nb = R // bm
    return pl.pallas_call(
        _expert_down_kernel,
        grid_spec=pltpu.PrefetchScalarGridSpec(
            num_scalar_prefetch=1,
            grid=(D // tn, nb),
            in_specs=[pl.BlockSpec((bm, F), lambda j, m, be: (m, 0)),
                      pl.BlockSpec((None, None, F, tn), lambda j, m, be: (widx, be[m], 0, j))],
            out_specs=pl.BlockSpec((bm, tn), lambda j, m, be: (m, j))),
        out_shape=jax.ShapeDtypeStruct((R, D), F32),
        compiler_params=_cparams("arbitrary", "arbitrary"),
        name="moe_expert_down",
    )(block_e, act, wd)


def _combine_kernel(dest_hbm, yb_hbm, x_ref, rt_ref, g_ref, xo_ref, ho_ref, idx_smem, rows_vmem, idx_sem, row_sem, *, tm):
    blk = pl.program_id(0)
    idx_copy = pltpu.make_async_copy(dest_hbm.at[blk], idx_smem, idx_sem)
    idx_copy.start()
    idx_copy.wait()

    def row_copy(r):
        return pltpu.make_async_copy(yb_hbm.at[pl.ds(idx_smem[r], 1)], rows_vmem.at[pl.ds(r, 1)], row_sem)

    def issue(r, c):
        row_copy(r).start()
        return c

    lax.fori_loop(0, TOP_K * tm, issue, 0)

    def drain(r, c):
        row_copy(r).wait()
        return c

    lax.fori_loop(0, TOP_K * tm, drain, 0)
    rt = rt_ref[...]
    xn = x_ref[...] + rt[:, 2:3] * rows_vmem[0:tm, :] + rt[:, 3:4] * rows_vmem[tm:2 * tm, :]
    xo_ref[...] = xn
    ho_ref[...] = _rms(xn, g_ref[...]).astype(ho_ref.dtype)


def moe_combine(x, yb, dest_blocks, route, norm_gain, layer, tm):
    T, D = x.shape
    return pl.pallas_call(
        functools.partial(_combine_kernel, tm=tm),
        grid=(T // tm,),
        in_specs=[pl.BlockSpec(memory_space=pl.ANY),
                  pl.BlockSpec(memory_space=pl.ANY),
                  pl.BlockSpec((tm, D), lambda i: (i, 0)),
                  pl.BlockSpec((tm, LANES), lambda i: (i, 0)),
                  pl.BlockSpec((None, 1, D), lambda i: (layer, 0, 0))],
        out_specs=[pl.BlockSpec((tm, D), lambda i: (i, 0)),
                   pl.BlockSpec((tm, D), lambda i: (i, 0))],
        out_shape=[jax.ShapeDtypeStruct((T, D), F32), jax.ShapeDtypeStruct((T, D), BF16)],
        scratch_shapes=[pltpu.SMEM((TOP_K * tm,), jnp.int32),
                        pltpu.VMEM((TOP_K * tm, D), F32),
                        pltpu.SemaphoreType.DMA(()),
                        pltpu.SemaphoreType.DMA(())],
        compiler_params=_cparams("arbitrary"),
        name="moe_combine",
    )(dest_blocks, yb, x, route, norm_gain)


def moe_layer(x, norm_ffn, norm_ple, w_router_pad, wg, wu, wd, layer, widx):
    T, D = x.shape
    E = N_EXPERTS
    bm = _tile(T, 512)
    route = moe_router(x, norm_ffn, w_router_pad, layer, widx)
    flat_e = route[:, :TOP_K].astype(jnp.int32).reshape(-1)
    onehot = (flat_e[:, None] == jnp.arange(E, dtype=jnp.int32)[None, :]).astype(jnp.int32)
    incl = jnp.cumsum(onehot, axis=0)
    counts = incl[-1]
    rank = jnp.sum((incl - onehot) * onehot, axis=1)
    padded = (counts + bm - 1) // bm * bm
    pad_end = jnp.cumsum(padded)
    pad_start = pad_end - padded
    dest = pad_start[flat_e] + rank
    n_rows = -(-(T * TOP_K + E * (bm - 1)) // bm) * bm
    n_blocks = n_rows // bm
    flat_tok = jnp.repeat(jnp.arange(T, dtype=jnp.int32), TOP_K)
    row_tok = jnp.zeros((n_rows,), jnp.int32).at[dest].set(flat_tok)
    block_e = jnp.minimum(
        jnp.sum(jnp.arange(n_blocks, dtype=jnp.int32)[:, None] * bm >= pad_end[None, :], axis=1), E - 1
    ).astype(jnp.int32)

    xs = moe_gather_norm(x, row_tok.reshape(n_blocks, bm), norm_ffn, layer, bm)
    act = moe_expert_up(xs, block_e, wg, wu, widx, bm)
    yb = moe_expert_down(act, block_e, wd, widx, bm)
    tmc = _tile(T, 256)
    dest_blocks = dest.reshape(T // tmc, tmc, TOP_K).transpose(0, 2, 1).reshape(T // tmc, TOP_K * tmc)
    return moe_combine(x, yb, dest_blocks, route, norm_ple, layer, tmc)


def _ple_kernel(x_ref, hp_ref, p_ref, wg_ref, wp_ref, g_ref, xo_ref, ho_ref):
    gate = _sigmoid(_dot(hp_ref[...], wg_ref[...]))
    emb = _dot(p_ref[...].astype(BF16), wp_ref[...])
    xn = x_ref[...] + emb * gate
    xo_ref[...] = xn
    ho_ref[...] = _rms(xn, g_ref[...]).astype(ho_ref.dtype)


def ple_layer(x, hp, p, w_gate_bf, w_proj_bf, next_gain, layer, gain_idx, out_dtype):
    T, D = x.shape
    Pd = p.shape[-1]
    tm = _tile(T, 512)
    return pl.pallas_call(
        _ple_kernel,
        grid=(T // tm,),
        in_specs=[pl.BlockSpec((tm, D), lambda i: (i, 0)),
                  pl.BlockSpec((tm, D), lambda i: (i, 0)),
                  pl.BlockSpec((None, tm, Pd), lambda i: (layer, i, 0)),
                  pl.BlockSpec((None, D, D), lambda i: (layer, 0, 0)),
                  pl.BlockSpec((None, Pd, D), lambda i: (layer, 0, 0)),
                  pl.BlockSpec((None, 1, D), lambda i: (gain_idx, 0, 0))],
        out_specs=[pl.BlockSpec((tm, D), lambda i: (i, 0)),
                   pl.BlockSpec((tm, D), lambda i: (i, 0))],
        out_shape=[jax.ShapeDtypeStruct((T, D), F32), jax.ShapeDtypeStruct((T, D), out_dtype)],
        compiler_params=_cparams("parallel"),
        name="ple_layer",
    )(x, hp, p, w_gate_bf, w_proj_bf, next_gain)


def _split_points(D):
    swa_q = SWA_Q_HEADS * SWA_HEAD_DIM
    swa_kv = SWA_KV_HEADS * SWA_HEAD_DIM
    dn_w = DN_HEADS * DN_HEAD_DIM
    return swa_q + 2 * swa_kv, swa_q + 2 * swa_kv + 4 * dn_w


def _pack_small_in_proj(w_in, q_lora, kv_lora):
    _, b_end = _split_points(w_in.shape[1])
    H = DN_HEADS
    o = b_end
    beta = w_in[:, :, o:o + H]
    decay = w_in[:, :, o + H:o + 2 * H]
    o += 2 * H
    cq = w_in[:, :, o:o + q_lora]
    o += q_lora
    ckv = w_in[:, :, o:o + kv_lora]
    o += kv_lora
    half = MLA_ROPE // 2
    kr1 = w_in[:, :, o:o + half]
    kr2 = w_in[:, :, o + half:o + 2 * half]
    o += MLA_ROPE
    pad = jnp.zeros(w_in.shape[:2] + (LANES - 2 * H,), w_in.dtype)
    packed = jnp.concatenate([cq, ckv, kr1, kr2, kr2, kr1, beta, decay, pad], axis=-1)
    return packed.astype(BF16), o


def _pack_w_uq(w_uq):
    Ld, R, _ = w_uq.shape
    H, half = MLA_HEADS, MLA_ROPE // 2
    w = w_uq.reshape(Ld, R, H, MLA_NOPE + MLA_ROPE)
    nope = w[..., :MLA_NOPE].reshape(Ld, R, H * MLA_NOPE)
    r1 = w[..., MLA_NOPE:MLA_NOPE + half]
    r2 = w[..., MLA_NOPE + half:]
    rope = jnp.concatenate([r1, r2, r2, r1], axis=-1).reshape(Ld, R, H * LANES)
    return jnp.concatenate([nope, rope], axis=-1).astype(BF16)


def _pack_w_ukv(w_ukv):
    Ld, R, _ = w_ukv.shape
    H = MLA_HEADS
    w = w_ukv.reshape(Ld, R, H, MLA_NOPE + MLA_V)
    kn = w[..., :MLA_NOPE].reshape(Ld, R, H * MLA_NOPE)
    vv = w[..., MLA_NOPE:].reshape(Ld, R, H * MLA_V)
    return jnp.concatenate([kn, vv], axis=-1).astype(BF16)


def _rope_tables(positions):
    half = MLA_ROPE // 2
    inv_freq = ROPE_THETA ** (-jnp.arange(half, dtype=F32) / half)
    ang = positions.astype(F32).reshape(-1)[:, None] * inv_freq
    cos, sin = jnp.cos(ang), jnp.sin(ang)
    zeros = jnp.zeros_like(cos)
    return (jnp.concatenate([cos, cos, zeros, zeros], axis=-1),
            jnp.concatenate([-sin, sin, zeros, zeros], axis=-1))


def _lane_vec(v, offset):
    Ld, n = v.shape
    out = jnp.zeros((Ld, 1, LANES), F32)
    return out.at[:, 0, offset:offset + n].set(v.astype(F32))


def kernel(x, p, positions, norm_mix, w_in, conv_w, dn_a_log, dn_dt_bias, dn_norm, swa_sinks, mla_q_norm, w_uq,
           mla_kv_norm, w_ukv, w_branch, w_out, norm_ffn, w_ffn_gate, w_ffn_up, w_ffn_down, w_router, w_exp_gate,
           w_exp_up, w_exp_down, norm_ple, w_ple_gate, w_ple_proj, final_norm):
    B, S, D = x.shape
    T = B * S
    depth = w_in.shape[0]
    q_lora, kv_lora = w_uq.shape[1], w_ukv.shape[1]
    a_end, b_end = _split_points(D)

    w_small, gate_col0 = _pack_small_in_proj(w_in, q_lora, kv_lora)
    w_gates = w_in[:, :, gate_col0:].astype(BF16)
    wq_packed = _pack_w_uq(w_uq)
    wkv_packed = _pack_w_ukv(w_ukv)
    w_out_bf = w_out.astype(BF16)
    w_ple_gate_bf = w_ple_gate.astype(BF16)
    w_ple_proj_bf = w_ple_proj.astype(BF16)
    wfg, wfu, wfd = w_ffn_gate.astype(BF16), w_ffn_up.astype(BF16), w_ffn_down.astype(BF16)
    weg, weu, wed = w_exp_gate.astype(BF16), w_exp_up.astype(BF16), w_exp_down.astype(BF16)
    w_router_pad = jnp.pad(w_router, ((0, 0), (0, 0), (0, LANES - w_router.shape[-1])))
    cos_tab, sin_tab = _rope_tables(positions)
    alog_vec = _lane_vec(dn_a_log, DN_HEADS)
    dtb_vec = _lane_vec(dn_dt_bias, DN_HEADS)
    row = lambda g: g.reshape(g.shape[0], 1, g.shape[-1])
    norm_mix3, norm_ffn3, norm_ple3 = row(norm_mix), row(norm_ffn), row(norm_ple)
    dn_norm3, mla_q_norm3, mla_kv_norm3 = row(dn_norm), row(mla_q_norm), row(mla_kv_norm)
    final3 = final_norm.reshape(1, 1, D)

    xf = x.reshape(T, D)
    pf = p.reshape(depth, T, p.shape[-1])
    h = rmsnorm_rows(xf, norm_mix3, 0, BF16)
    tn_in = 512
    for i in range(depth):
        proj_a = matmul_ws(h, w_in, i, n_cols=a_end, col_block_offset=0, tn=tn_in, out_dtype=BF16)
        proj_b = matmul_ws(h, w_in, i, n_cols=b_end - a_end, col_block_offset=a_end // tn_in, tn=tn_in,
                           out_dtype=BF16)
        proj_small = matmul_ws(h, w_small, i, n_cols=w_small.shape[-1], col_block_offset=0,
                               tn=w_small.shape[-1], out_dtype=F32)
        gates = matmul_ws(h, w_gates, i, n_cols=w_gates.shape[-1], col_block_offset=0, tn=1024,
                          out_dtype=BF16, act="sigmoid")

        y_a = swa_attention(proj_a, swa_sinks, i, S)

        qkvn, gb = dn_prep(proj_b, proj_small, conv_w, alog_vec, dtb_vec, i, S)
        nchunks = T // DN_CHUNK
        grow = gb[:, DN_HEADS:2 * DN_HEADS].reshape(nchunks, DN_CHUNK, DN_HEADS).transpose(0, 2, 1)
        y_b = dn_chunked(qkvn, proj_b, gb, grow, dn_norm3, i, B, S)

        qfull = mla_q(proj_small, mla_q_norm3, wq_packed, cos_tab, sin_tab, i)
        kfull, vfull = mla_kv(proj_small, mla_kv_norm3, wkv_packed, cos_tab, sin_tab, i)
        y_c = mla_flash(qfull, kfull, vfull, B, S)

        merged = merge_branches(y_a, y_b, y_c, gates, w_branch, i)
        xf, h2 = out_proj_residual(merged, w_out_bf, xf, norm_ffn3, i)

        j = i // 2
        if i % 2 == 0:
            xf, hp = ffn_swiglu(h2, wfg, wfu, wfd, xf, norm_ple3, j, i)
        else:
            xf, hp = moe_layer(xf, norm_ffn3, norm_ple3, w_router_pad, weg, weu, wed, i, j)

        if i + 1 < depth:
            xf, h = ple_layer(xf, hp, pf, w_ple_gate_bf, w_ple_proj_bf, norm_mix3, i, i + 1, BF16)
        else:
            xf, out = ple_layer(xf, hp, pf, w_ple_gate_bf, w_ple_proj_bf, final3, i, 0, F32)
    return out.reshape(B, S, D)
```

```python
import functools

import jax
import jax.numpy as jnp
import numpy as np
from jax import lax
from jax.experimental import pallas as pl
from jax.experimental.pallas import tpu as pltpu

BF16 = jnp.bfloat16
F32 = jnp.float32
NEG_INF = float("-inf")

NORM_EPS = 1e-6
SWA_Q_HEADS, SWA_KV_HEADS, SWA_HEAD_DIM, SWA_WINDOW = 16, 4, 64, 128
DN_HEADS, DN_HEAD_DIM, DN_CONV, DN_CHUNK = 8, 128, 4, 64
MLA_HEADS, MLA_NOPE, MLA_ROPE, MLA_V = 8, 128, 64, 128
ROPE_THETA = 10000.0
N_EXPERTS, TOP_K = 8, 2

LANES = 128
VMEM_LIMIT_BYTES = 56 * 1024 * 1024


def _cparams(*semantics):
    return pltpu.CompilerParams(dimension_semantics=semantics, vmem_limit_bytes=VMEM_LIMIT_BYTES)


def _tile(n, pref):
    t = min(n, pref)
    while n % t:
        t //= 2
    return t


def _dot(a, b):
    return jnp.dot(a, b, preferred_element_type=F32)


def _dot_nt(a, b):
    return lax.dot_general(a, b, (((1,), (1,)), ((), ())), preferred_element_type=F32)


def _dot_tn(a, b):
    return lax.dot_general(a, b, (((0,), (0,)), ((), ())), preferred_element_type=F32)


def _rms(x, gain):
    inv = lax.rsqrt(jnp.mean(x * x, axis=-1, keepdims=True) + NORM_EPS)
    return x * inv * gain


def _sigmoid(x):
    return 1.0 / (1.0 + jnp.exp(-x))


def _lane_repeat(x, n):
    return jnp.concatenate([x] * n, axis=1)


def _rmsnorm_kernel(x_ref, g_ref, o_ref):
    o_ref[...] = _rms(x_ref[...], g_ref[...]).astype(o_ref.dtype)


def rmsnorm_rows(x, gains, layer, out_dtype):
    T, D = x.shape
    tm = _tile(T, 512)
    return pl.pallas_call(
        _rmsnorm_kernel,
        grid=(T // tm,),
        in_specs=[pl.BlockSpec((tm, D), lambda i: (i, 0)),
                  pl.BlockSpec((None, 1, D), lambda i: (layer, 0, 0))],
        out_specs=pl.BlockSpec((tm, D), lambda i: (i, 0)),
        out_shape=jax.ShapeDtypeStruct((T, D), out_dtype),
        compiler_params=_cparams("parallel"),
        name="rmsnorm_rows",
    )(x, gains)


def _matmul_ws_kernel(x_ref, wt_ref, o_ref, wb_ref, *, act):
    @pl.when(pl.program_id(1) == 0)
    def _():
        wb_ref[...] = wt_ref[...].T.astype(BF16)

    acc = _dot(x_ref[...], wb_ref[...])
    if act == "sigmoid":
        acc = _sigmoid(acc)
    o_ref[...] = acc.astype(o_ref.dtype)


def matmul_ws(x, w_t, layer, *, n_cols, row_block_offset, tn, out_dtype, act=None, tm_pref=1024):
    M, K = x.shape
    tm = _tile(M, tm_pref)
    assert n_cols % tn == 0
    return pl.pallas_call(
        functools.partial(_matmul_ws_kernel, act=act),
        grid=(n_cols // tn, M // tm),
        in_specs=[pl.BlockSpec((tm, K), lambda j, i: (i, 0)),
                  pl.BlockSpec((None, tn, K), lambda j, i: (layer, j + row_block_offset, 0))],
        out_specs=pl.BlockSpec((tm, tn), lambda j, i: (i, j)),
        out_shape=jax.ShapeDtypeStruct((M, n_cols), out_dtype),
        scratch_shapes=[pltpu.VMEM((K, tn), BF16)],
        compiler_params=_cparams("arbitrary", "arbitrary"),
        name="matmul_ws",
    )(x, w_t)


def _swa_kernel(sinks_ref, q_ref, kc_ref, vc_ref, kp_ref, vp_ref, o_ref, *, layer, tiles_per_seq, rows):
    L = SWA_WINDOW
    dh = SWA_HEAD_DIM
    G = SWA_Q_HEADS // SWA_KV_HEADS
    first = (pl.program_id(0) % tiles_per_seq) == 0
    kall = jnp.concatenate([kp_ref[...], kc_ref[...]], axis=0)
    vall = jnp.concatenate([vp_ref[...], vc_ref[...]], axis=0)
    qi = lax.broadcasted_iota(jnp.int32, (L, 2 * L), 0)
    kj = lax.broadcasted_iota(jnp.int32, (L, 2 * L), 1)
    rel = qi + L - kj
    band = jnp.logical_and(rel >= 0, rel < SWA_WINDOW)
    kj_min = jnp.where(first, L, 0)
    scale = dh ** -0.5
    problems = [(b, h) for b in range(rows // L) for h in range(SWA_KV_HEADS)]
    scores = []
    for b, h in problems:
        qs = jnp.concatenate(
            [q_ref[b * L:(b + 1) * L, (h * G + g) * dh:(h * G + g + 1) * dh] for g in range(G)], axis=0)
        scores.append(_dot_nt(qs, kall[b * L:(b + 2) * L, h * dh:(h + 1) * dh]))
    probs, inv_denoms = [], []
    for (b, h), s in zip(problems, scores):
        valid = jnp.logical_and(band, kj >= kj_min) if b == 0 else band
        es, rs = [], []
        for g in range(G):
            sg = jnp.where(valid, s[g * L:(g + 1) * L] * scale, NEG_INF)
            sink = sinks_ref[layer, h * G + g]
            m = jnp.maximum(jnp.full((L, LANES), sink, F32), jnp.max(sg, axis=-1, keepdims=True))
            e = jnp.exp(sg - _lane_repeat(m, 2 * L // LANES))
            denom = jnp.sum(e, axis=-1, keepdims=True) + jnp.exp(sink - m[:, :1])
            es.append(e.astype(BF16))
            rs.append(1.0 / denom)
        probs.append(jnp.concatenate(es, axis=0))
        inv_denoms.append(rs)
    for (b, h), e, rs in zip(problems, probs, inv_denoms):
        o = _dot(e, vall[b * L:(b + 2) * L, h * dh:(h + 1) * dh])
        for g in range(G):
            hq = h * G + g
            o_ref[b * L:(b + 1) * L, hq * dh:(hq + 1) * dh] = (o[g * L:(g + 1) * L] * rs[g]).astype(o_ref.dtype)


def swa_attention(qkv, sinks, layer, seq_len):
    T = qkv.shape[0]
    L = SWA_WINDOW
    rows = _tile(seq_len, 512)
    wq = SWA_Q_HEADS * SWA_HEAD_DIM
    wkv = SWA_KV_HEADS * SWA_HEAD_DIM
    kcol, vcol = wq // wkv, wq // wkv + 1
    rpl = rows // L

    def prev_map(col):
        return lambda i: (jnp.maximum(i * rpl - 1, 0), col)

    return pl.pallas_call(
        functools.partial(_swa_kernel, layer=layer, tiles_per_seq=seq_len // rows, rows=rows),
        grid=(T // rows,),
        in_specs=[pl.BlockSpec(memory_space=pltpu.SMEM),
                  pl.BlockSpec((rows, wq), lambda i: (i, 0)),
                  pl.BlockSpec((rows, wkv), lambda i: (i, kcol)),
                  pl.BlockSpec((rows, wkv), lambda i: (i, vcol)),
                  pl.BlockSpec((L, wkv), prev_map(kcol)),
                  pl.BlockSpec((L, wkv), prev_map(vcol))],
        out_specs=pl.BlockSpec((rows, wq), lambda i: (i, 0)),
        out_shape=jax.ShapeDtypeStruct((T, wq), BF16),
        compiler_params=_cparams("parallel"),
        name="swa_attention",
    )(sinks, qkv, qkv, qkv, qkv, qkv)


def _dn_prep_kernel(cur_ref, prev_ref, cw_ref, t2_ref, alog_ref, dtb_ref, qkv_ref, gb_ref, *, tiles_per_seq, tm):
    H, dh, W = DN_HEADS, DN_HEAD_DIM, DN_CONV
    first = (pl.program_id(0) % tiles_per_seq) == 0
    prev = prev_ref[...].astype(F32)
    prev = jnp.where(first, 0.0, prev)
    xcat = jnp.concatenate([prev, cur_ref[...].astype(F32)], axis=0)
    P = prev.shape[0]
    cw = cw_ref[...]
    y = None
    for j in range(W):
        term = xcat[P - (W - 1) + j:P - (W - 1) + j + tm] * cw[j:j + 1]
        y = term if y is None else y + term
    y = y * _sigmoid(y)
    width = H * dh
    for h in range(H):
        qh = y[:, h * dh:(h + 1) * dh]
        kh = y[:, width + h * dh:width + (h + 1) * dh]
        qn = qh * lax.rsqrt(jnp.sum(qh * qh, axis=-1, keepdims=True) + 1e-6) * (dh ** -0.5)
        kn = kh * lax.rsqrt(jnp.sum(kh * kh, axis=-1, keepdims=True) + 1e-6)
        qkv_ref[:, h * dh:(h + 1) * dh] = qn.astype(qkv_ref.dtype)
        qkv_ref[:, width + h * dh:width + (h + 1) * dh] = kn.astype(qkv_ref.dtype)
    qkv_ref[:, 2 * width:] = y[:, 2 * width:].astype(qkv_ref.dtype)

    t2 = t2_ref[...]
    xs = t2 + dtb_ref[...]
    softplus = jnp.maximum(xs, 0.0) + jnp.log(1.0 + jnp.exp(-jnp.abs(xs)))
    g = -jnp.exp(alog_ref[...]) * softplus
    ri = lax.broadcasted_iota(jnp.int32, (tm, tm), 0)
    ci = lax.broadcasted_iota(jnp.int32, (tm, tm), 1)
    same_chunk = (ri // DN_CHUNK) == (ci // DN_CHUNK)
    tri = jnp.where(jnp.logical_and(same_chunk, ri >= ci), 1.0, 0.0)
    gc = jnp.dot(tri, g, preferred_element_type=F32, precision=lax.Precision.HIGHEST)
    lane = lax.broadcasted_iota(jnp.int32, t2.shape, 1)
    gb_ref[...] = jnp.where(lane < H, _sigmoid(t2), gc)


def dn_prep(proj_b, proj_small, conv_w, alog_vec, dtb_vec, layer, seq_len):
    T = proj_b.shape[0]
    width3 = 3 * DN_HEADS * DN_HEAD_DIM
    tm = _tile(seq_len, 256)
    P = 16
    tail2_col = proj_small.shape[1] // LANES - 1
    return pl.pallas_call(
        functools.partial(_dn_prep_kernel, tiles_per_seq=seq_len // tm, tm=tm),
        grid=(T // tm,),
        in_specs=[pl.BlockSpec((tm, width3), lambda i: (i, 0)),
                  pl.BlockSpec((P, width3), lambda i: (jnp.maximum(i * (tm // P) - 1, 0), 0)),
                  pl.BlockSpec((None, DN_CONV, width3), lambda i: (layer, 0, 0)),
                  pl.BlockSpec((tm, LANES), lambda i: (i, tail2_col)),
                  pl.BlockSpec((None, 1, LANES), lambda i: (layer, 0, 0)),
                  pl.BlockSpec((None, 1, LANES), lambda i: (layer, 0, 0))],
        out_specs=[pl.BlockSpec((tm, width3), lambda i: (i, 0)),
                   pl.BlockSpec((tm, LANES), lambda i: (i, 0))],
        out_shape=[jax.ShapeDtypeStruct((T, width3), BF16),
                   jax.ShapeDtypeStruct((T, LANES), F32)],
        compiler_params=_cparams("parallel"),
        name="dn_prep",
    )(proj_b, proj_b, conv_w, proj_small, alog_vec, dtb_vec)


def _unit_lower_inverse_many(ls):
    C = ls[0].shape[0]
    assert C == 64
    ri = lax.broadcasted_iota(jnp.int32, (C, C), 0)
    ci = lax.broadcasted_iota(jnp.int32, (C, C), 1)
    eye = jnp.where(ri == ci, 1.0, 0.0)

    def mm(a, b):
        return [_dot(x.astype(BF16), y.astype(BF16)) for x, y in zip(a, b)]

    l2 = mm(ls, ls)
    l4 = mm(l2, l2)
    l3 = mm(ls, l2)
    l8 = mm(l4, l4)
    x1 = [b - a - c for a, b, c in zip(ls, l2, l3)]
    l16 = mm(l8, l8)
    l12 = mm(l4, l8)
    x2 = [a + b + c for a, b, c in zip(l4, l8, l12)]
    l32 = mm(l16, l16)
    x12 = mm(x1, x2)
    y = [a + b + c for a, b, c in zip(x1, x2, x12)]
    l48 = mm(l16, l32)
    x3 = [a + b + c for a, b, c in zip(l16, l32, l48)]
    yx3 = mm(y, x3)
    return [eye + a + b + c for a, b, c in zip(y, x3, yx3)]


def _dn_chunk_kernel(qkv_ref, z_ref, gb_ref, grow_ref, norm_ref, o_ref,
                     state_ref, u_ref, w_ref, qe_ref, kd_ref, qk_ref, *, rows):
    H, dh, C = DN_HEADS, DN_HEAD_DIM, DN_CHUNK
    width = H * dh
    heads = range(H)

    @pl.when(pl.program_id(1) == 0)
    def _():
        state_ref[...] = jnp.zeros_like(state_ref)

    ri = lax.broadcasted_iota(jnp.int32, (C, C), 0)
    ci = lax.broadcasted_iota(jnp.int32, (C, C), 1)
    incl = ri >= ci
    strict = ri > ci
    gain = norm_ref[...]

    def prepare(c, carry):
        r0 = pl.multiple_of(c * C, C)
        gb = gb_ref[pl.ds(r0, C), :]
        grow_all = grow_ref[c]
        q = [qkv_ref[pl.ds(r0, C), h * dh:(h + 1) * dh] for h in heads]
        k = [qkv_ref[pl.ds(r0, C), width + h * dh:width + (h + 1) * dh] for h in heads]
        v = [qkv_ref[pl.ds(r0, C), 2 * width + h * dh:2 * width + (h + 1) * dh] for h in heads]
        beta = [gb[:, h:h + 1] for h in heads]
        gcol = [gb[:, H + h:H + h + 1] for h in heads]
        decay = [jnp.exp(jnp.where(incl, gcol[h] - grow_all[h:h + 1, :], NEG_INF)) for h in heads]
        kf = [k[h].astype(F32) for h in heads]
        kbeta = [kf[h] * beta[h] for h in heads]
        eg = [jnp.exp(gcol[h]) for h in heads]
        kk = [_dot_nt(kbeta[h].astype(BF16), k[h]) for h in heads]
        qk = [_dot_nt(q[h], k[h]) for h in heads]
        tmat = _unit_lower_inverse_many([jnp.where(strict, kk[h] * decay[h], 0.0) for h in heads])
        tb = [t.astype(BF16) for t in tmat]
        u = [_dot(tb[h], (v[h].astype(F32) * beta[h]).astype(BF16)) for h in heads]
        w = [_dot(tb[h], (kbeta[h] * eg[h]).astype(BF16)) for h in heads]
        for h in heads:
            g_last = gcol[h][C - 1:C, :]
            u_ref[h, pl.ds(r0, C), :] = u[h]
            w_ref[h, pl.ds(r0, C), :] = w[h].astype(BF16)
            qe_ref[h, pl.ds(r0, C), :] = (q[h].astype(F32) * eg[h]).astype(BF16)
            kd_ref[h, pl.ds(r0, C), :] = (kf[h] * jnp.exp(g_last - gcol[h])).astype(BF16)
            qk_ref[h, pl.ds(r0, C), :] = jnp.where(incl, qk[h] * decay[h], 0.0).astype(BF16)
        return carry

    lax.fori_loop(0, rows // C, prepare, 0)

    def scan(c, carry):
        r0 = pl.multiple_of(c * C, C)
        gb = gb_ref[pl.ds(r0, C), :]
        state = [state_ref[h] for h in heads]
        sb = [s.astype(BF16) for s in state]
        w_s = [_dot(w_ref[h, pl.ds(r0, C), :], sb[h]) for h in heads]
        q_s = [_dot(qe_ref[h, pl.ds(r0, C), :], sb[h]) for h in heads]
        vnb = [(u_ref[h, pl.ds(r0, C), :] - w_s[h]).astype(BF16) for h in heads]
        o_in = [_dot(qk_ref[h, pl.ds(r0, C), :], vnb[h]) for h in heads]
        kv = [_dot_tn(kd_ref[h, pl.ds(r0, C), :], vnb[h]) for h in heads]
        for h in heads:
            g_last = gb[C - 1:C, H + h:H + h + 1]
            state_ref[h] = state[h] * jnp.exp(g_last) + kv[h]
            z = z_ref[pl.ds(r0, C), h * dh:(h + 1) * dh].astype(F32)
            y = _rms(q_s[h] + o_in[h], gain) * (z * _sigmoid(z))
            o_ref[pl.ds(r0, C), h * dh:(h + 1) * dh] = y.astype(o_ref.dtype)
        return carry

    lax.fori_loop(0, rows // C, scan, 0)


def dn_chunked(qkvn, proj_b, gb, grow, dn_norm, layer, batch, seq_len):
    T = qkvn.shape[0]
    H, dh, C = DN_HEADS, DN_HEAD_DIM, DN_CHUNK
    width = H * dh
    rows = _tile(seq_len, 512)
    tps = seq_len // rows
    zcol = 3
    return pl.pallas_call(
        functools.partial(_dn_chunk_kernel, rows=rows),
        grid=(batch, tps),
        in_specs=[pl.BlockSpec((rows, 3 * width), lambda b, n: (b * tps + n, 0)),
                  pl.BlockSpec((rows, width), lambda b, n: (b * tps + n, zcol)),
                  pl.BlockSpec((rows, LANES), lambda b, n: (b * tps + n, 0)),
                  pl.BlockSpec((rows // C, H, C), lambda b, n: (b * tps + n, 0, 0)),
                  pl.BlockSpec((None, 1, dh), lambda b, n: (layer, 0, 0))],
        out_specs=pl.BlockSpec((rows, width), lambda b, n: (b * tps + n, 0)),
        out_shape=jax.ShapeDtypeStruct((T, width), BF16),
        scratch_shapes=[pltpu.VMEM((H, dh, dh), F32),
                        pltpu.VMEM((H, rows, dh), F32),
                        pltpu.VMEM((H, rows, dh), BF16),
                        pltpu.VMEM((H, rows, dh), BF16),
                        pltpu.VMEM((H, rows, dh), BF16),
                        pltpu.VMEM((H, rows, C), BF16)],
        compiler_params=_cparams("parallel", "arbitrary"),
        name="dn_chunked",
    )(qkvn, proj_b, gb, grow, dn_norm)


def _rotate_pairs(x, cos_tab, sin_tab):
    return x * cos_tab + pltpu.roll(x, LANES // 2, axis=1) * sin_tab


def _mla_q_kernel(c_ref, g_ref, w_ref, cos_ref, sin_ref, o_ref):
    H = MLA_HEADS
    scale = (MLA_NOPE + MLA_ROPE) ** -0.5 * float(np.log2(np.e))
    cn = _rms(c_ref[...], g_ref[...]).astype(BF16)
    acc = _dot(cn, w_ref[...]) * scale
    cos_tab, sin_tab = cos_ref[...], sin_ref[...]
    for h in range(H):
        o_ref[:, 2 * h * LANES:(2 * h + 1) * LANES] = acc[:, h * LANES:(h + 1) * LANES].astype(o_ref.dtype)
        xr = acc[:, (H + h) * LANES:(H + h + 1) * LANES]
        o_ref[:, (2 * h + 1) * LANES:(2 * h + 2) * LANES] = _rotate_pairs(xr, cos_tab, sin_tab).astype(o_ref.dtype)


def mla_q(proj_small, gains, wq, cos_tab, sin_tab, layer):
    T = proj_small.shape[0]
    R = wq.shape[1]
    tm = _tile(T, 512)
    N = wq.shape[2]
    return pl.pallas_call(
        _mla_q_kernel,
        grid=(T // tm,),
        in_specs=[pl.BlockSpec((tm, R), lambda i: (i, 0)),
                  pl.BlockSpec((None, 1, R), lambda i: (layer, 0, 0)),
                  pl.BlockSpec((None, R, N), lambda i: (layer, 0, 0)),
                  pl.BlockSpec((tm, LANES), lambda i: (i, 0)),
                  pl.BlockSpec((tm, LANES), lambda i: (i, 0))],
        out_specs=pl.BlockSpec((tm, N), lambda i: (i, 0)),
        out_shape=jax.ShapeDtypeStruct((T, N), BF16),
        compiler_params=_cparams("parallel"),
        name="mla_q",
    )(proj_small, gains, wq, cos_tab, sin_tab)


def _mla_kv_kernel(c_ref, kr_ref, g_ref, w_ref, cos_ref, sin_ref, k_ref, v_ref):
    H = MLA_HEADS
    cn = _rms(c_ref[...], g_ref[...]).astype(BF16)
    acc = _dot(cn, w_ref[...])
    kr = _rotate_pairs(kr_ref[...], cos_ref[...], sin_ref[...]).astype(k_ref.dtype)
    for h in range(H):
        k_ref[:, 2 * h * LANES:(2 * h + 1) * LANES] = acc[:, h * LANES:(h + 1) * LANES].astype(k_ref.dtype)
        k_ref[:, (2 * h + 1) * LANES:(2 * h + 2) * LANES] = kr
    v_ref[...] = acc[:, H * LANES:].astype(v_ref.dtype)


def mla_kv(proj_small, gains, wkv, cos_tab, sin_tab, layer):
    T = proj_small.shape[0]
    R = wkv.shape[1]
    tm = _tile(T, 512)
    N = wkv.shape[2]
    H = MLA_HEADS
    kr_col = (2 * R) // LANES
    return pl.pallas_call(
        _mla_kv_kernel,
        grid=(T // tm,),
        in_specs=[pl.BlockSpec((tm, R), lambda i: (i, 1)),
                  pl.BlockSpec((tm, LANES), lambda i: (i, kr_col)),
                  pl.BlockSpec((None, 1, R), lambda i: (layer, 0, 0)),
                  pl.BlockSpec((None, R, N), lambda i: (layer, 0, 0)),
                  pl.BlockSpec((tm, LANES), lambda i: (i, 0)),
                  pl.BlockSpec((tm, LANES), lambda i: (i, 0))],
        out_specs=[pl.BlockSpec((tm, 2 * H * LANES), lambda i: (i, 0)),
                   pl.BlockSpec((tm, H * MLA_V), lambda i: (i, 0))],
        out_shape=[jax.ShapeDtypeStruct((T, 2 * H * LANES), BF16),
                   jax.ShapeDtypeStruct((T, H * MLA_V), BF16)],
        compiler_params=_cparams("parallel"),
        name="mla_kv",
    )(proj_small, proj_small, gains, wkv, cos_tab, sin_tab)


def _flash_kernel(q_ref, k_ref, v_ref, o_ref, m_ref, l_ref, acc_ref, *, tq, tk):
    H, dv = MLA_HEADS, MLA_V
    dqk = 2 * LANES
    qi = pl.program_id(1)
    kj = pl.program_id(2)

    @pl.when(kj == 0)
    def _():
        m_ref[...] = jnp.full_like(m_ref, NEG_INF)
        l_ref[...] = jnp.zeros_like(l_ref)
        acc_ref[...] = jnp.zeros_like(acc_ref)

    def step(masked):
        if masked:
            row = lax.broadcasted_iota(jnp.int32, (tq, tk), 0)
            col = lax.broadcasted_iota(jnp.int32, (tq, tk), 1)
            keep = col <= row
        scores = [_dot_nt(q_ref[:, h * dqk:(h + 1) * dqk], k_ref[:, h * dqk:(h + 1) * dqk]) for h in range(H)]
        probs = []
        for h in range(H):
            s = scores[h]
            if masked:
                s = jnp.where(keep, s, NEG_INF)
            m_prev = m_ref[h]
            m_new = jnp.maximum(m_prev, jnp.max(s, axis=-1, keepdims=True))
            alpha = jnp.exp2(m_prev - m_new)
            p = jnp.exp2(s - _lane_repeat(m_new, tk // LANES))
            l_part = p[:, 0:LANES]
            for t in range(1, tk // LANES):
                l_part = l_part + p[:, t * LANES:(t + 1) * LANES]
            l_ref[h] = alpha * l_ref[h] + l_part
            m_ref[h] = m_new
            acc_ref[h] = alpha * acc_ref[h]
            probs.append(p.astype(BF16))
        for h in range(H):
            acc_ref[h] += _dot(probs[h], v_ref[:, h * dv:(h + 1) * dv])

    @pl.when(kj < qi)
    def _():
        step(False)

    @pl.when(kj == qi)
    def _():
        step(True)
        for h in range(H):
            l = jnp.sum(l_ref[h], axis=-1, keepdims=True)
            o_ref[:, h * dv:(h + 1) * dv] = (acc_ref[h] / l).astype(o_ref.dtype)


def mla_flash(qf, kf, v, batch, seq_len):
    T = qf.shape[0]
    H, dv = MLA_HEADS, MLA_V
    t = _tile(seq_len, 512)
    nq = seq_len // t
    return pl.pallas_call(
        functools.partial(_flash_kernel, tq=t, tk=t),
        grid=(batch, nq, nq),
        in_specs=[pl.BlockSpec((t, qf.shape[1]), lambda b, i, j: (b * nq + i, 0)),
                  pl.BlockSpec((t, kf.shape[1]), lambda b, i, j: (b * nq + jnp.minimum(j, i), 0)),
                  pl.BlockSpec((t, v.shape[1]), lambda b, i, j: (b * nq + jnp.minimum(j, i), 0))],
        out_specs=pl.BlockSpec((t, H * dv), lambda b, i, j: (b * nq + i, 0)),
        out_shape=jax.ShapeDtypeStruct((T, H * dv), BF16),
        scratch_shapes=[pltpu.VMEM((H, t, LANES), F32),
                        pltpu.VMEM((H, t, LANES), F32),
                        pltpu.VMEM((H, t, dv), F32)],
        compiler_params=_cparams("parallel", "parallel", "arbitrary"),
        name="mla_flash",
    )(qf, kf, v)


def _merge_kernel(ya_ref, yb_ref, yc_ref, ga_ref, gb_ref, gc_ref, wa_ref, wb_ref, wc_ref, o_ref, sa_ref, sb_ref, sc_ref):
    @pl.when(pl.program_id(1) == 0)
    def _():
        sa_ref[...] = wa_ref[...].astype(BF16)
        sb_ref[...] = wb_ref[...].astype(BF16)
        sc_ref[...] = wc_ref[...].astype(BF16)

    acc = ga_ref[...].astype(F32) * _dot(ya_ref[...], sa_ref[...])
    acc = acc + gb_ref[...].astype(F32) * _dot(yb_ref[...], sb_ref[...])
    acc = acc + gc_ref[...].astype(F32) * _dot(yc_ref[...], sc_ref[...])
    o_ref[...] = acc.astype(o_ref.dtype)


def merge_branches(ya, yb, yc, gates, w_branch, layer):
    T, Wb = ya.shape
    D = w_branch.shape[-1]
    tm = _tile(T, 1024)
    tn = _tile(D, 512)
    nj = D // tn
    y_spec = pl.BlockSpec((tm, Wb), lambda j, i: (i, 0))

    def g_spec(n):
        return pl.BlockSpec((tm, tn), lambda j, i: (i, n * nj + j))

    def w_spec(n):
        return pl.BlockSpec((None, None, Wb, tn), lambda j, i: (layer, n, 0, j))

    return pl.pallas_call(
        _merge_kernel,
        grid=(nj, T // tm),
        in_specs=[y_spec, y_spec, y_spec, g_spec(0), g_spec(1), g_spec(2), w_spec(0), w_spec(1), w_spec(2)],
        out_specs=pl.BlockSpec((tm, tn), lambda j, i: (i, j)),
        out_shape=jax.ShapeDtypeStruct((T, D), BF16),
        scratch_shapes=[pltpu.VMEM((Wb, tn), BF16)] * 3,
        compiler_params=_cparams("arbitrary", "arbitrary"),
        name="merge_branches",
    )(ya, yb, yc, gates, gates, gates, w_branch, w_branch, w_branch)


def _out_proj_kernel(m_ref, w_ref, x_ref, g_ref, xo_ref, ho_ref):
    xn = x_ref[...] + _dot(m_ref[...], w_ref[...])
    xo_ref[...] = xn
    ho_ref[...] = _rms(xn, g_ref[...]).astype(ho_ref.dtype)


def out_proj_residual(merged, w_out_bf, x, norm_gain, layer):
    T, D = x.shape
    tm = _tile(T, 512)
    return pl.pallas_call(
        _out_proj_kernel,
        grid=(T // tm,),
        in_specs=[pl.BlockSpec((tm, D), lambda i: (i, 0)),
                  pl.BlockSpec((None, D, D), lambda i: (layer, 0, 0)),
                  pl.BlockSpec((tm, D), lambda i: (i, 0)),
                  pl.BlockSpec((None, 1, D), lambda i: (layer, 0, 0))],
        out_specs=[pl.BlockSpec((tm, D), lambda i: (i, 0)),
                   pl.BlockSpec((tm, D), lambda i: (i, 0))],
        out_shape=[jax.ShapeDtypeStruct((T, D), F32), jax.ShapeDtypeStruct((T, D), BF16)],
        compiler_params=_cparams("parallel"),
        name="out_proj_residual",
    )(merged, w_out_bf, x, norm_gain)


def _ffn_kernel(h_ref, wg_ref, wu_ref, wd_ref, x_ref, g_ref, xo_ref, ho_ref, acc_ref):
    f = pl.program_id(1)

    @pl.when(f == 0)
    def _():
        acc_ref[...] = x_ref[...]

    h = h_ref[...]
    gate = _dot(h, wg_ref[...])
    up = _dot(h, wu_ref[...])
    act = (gate * _sigmoid(gate) * up).astype(BF16)
    acc_ref[...] += _dot(act, wd_ref[...])

    @pl.when(f == pl.num_programs(1) - 1)
    def _():
        xn = acc_ref[...]
        xo_ref[...] = xn
        ho_ref[...] = _rms(xn, g_ref[...]).astype(ho_ref.dtype)


def ffn_swiglu(h, wg, wu, wd, x, norm_gain, widx, layer):
    T, D = x.shape
    F = wg.shape[-1]
    tm = _tile(T, 512)
    tf = _tile(F, 512)
    return pl.pallas_call(
        _ffn_kernel,
        grid=(T // tm, F // tf),
        in_specs=[pl.BlockSpec((tm, D), lambda i, f: (i, 0)),
                  pl.BlockSpec((None, D, tf), lambda i, f: (widx, 0, f)),
                  pl.BlockSpec((None, D, tf), lambda i, f: (widx, 0, f)),
                  pl.BlockSpec((None, tf, D), lambda i, f: (widx, f, 0)),
                  pl.BlockSpec((tm, D), lambda i, f: (i, 0)),
                  pl.BlockSpec((None, 1, D), lambda i, f: (layer, 0, 0))],
        out_specs=[pl.BlockSpec((tm, D), lambda i, f: (i, 0)),
                   pl.BlockSpec((tm, D), lambda i, f: (i, 0))],
        out_shape=[jax.ShapeDtypeStruct((T, D), F32), jax.ShapeDtypeStruct((T, D), BF16)],
        scratch_shapes=[pltpu.VMEM((tm, D), F32)],
        compiler_params=_cparams("parallel", "arbitrary"),
        name="ffn_swiglu",
    )(h, wg, wu, wd, x, norm_gain)


def _router_kernel(x_ref, g_ref, w_ref, o_ref):
    h = _rms(x_ref[...], g_ref[...])
    logits = jnp.dot(h, w_ref[...], preferred_element_type=F32, precision=lax.Precision.HIGHEST)
    lane = lax.broadcasted_iota(jnp.int32, logits.shape, 1)
    lanef = lane.astype(F32)
    big = float(LANES)
    logits = jnp.where(lane < N_EXPERTS, logits, NEG_INF)
    l1 = jnp.max(logits, axis=-1, keepdims=True)
    i1 = jnp.min(jnp.where(logits == l1, lanef, big), axis=-1, keepdims=True)
    rest = jnp.where(lanef == i1, NEG_INF, logits)
    l2 = jnp.max(rest, axis=-1, keepdims=True)
    i2 = jnp.min(jnp.where(rest == l2, lanef, big), axis=-1, keepdims=True)
    e2 = jnp.exp(l2 - l1)
    w1 = 1.0 / (1.0 + e2)
    w2 = e2 / (1.0 + e2)
    out = jnp.where(lane == 0, i1, jnp.where(lane == 1, i2, jnp.where(lane == 2, w1, jnp.where(lane == 3, w2, 0.0))))
    o_ref[...] = out


def moe_router(x, norm_gain, w_router_pad, layer, widx):
    T, D = x.shape
    tm = _tile(T, 512)
    return pl.pallas_call(
        _router_kernel,
        grid=(T // tm,),
        in_specs=[pl.BlockSpec((tm, D), lambda i: (i, 0)),
                  pl.BlockSpec((None, 1, D), lambda i: (layer, 0, 0)),
                  pl.BlockSpec((None, D, LANES), lambda i: (widx, 0, 0))],
        out_specs=pl.BlockSpec((tm, LANES), lambda i: (i, 0)),
        out_shape=jax.ShapeDtypeStruct((T, LANES), F32),
        compiler_params=_cparams("parallel"),
        name="moe_router",
    )(x, norm_gain, w_router_pad)


def _gather_rows_kernel(idx_hbm, x_hbm, g_ref, o_ref, idx_smem, rows_vmem, idx_sem, row_sem, *, bm):
    blk = pl.program_id(0)
    idx_copy = pltpu.make_async_copy(idx_hbm.at[blk], idx_smem, idx_sem)
    idx_copy.start()
    idx_copy.wait()

    def row_copy(r):
        return pltpu.make_async_copy(x_hbm.at[pl.ds(idx_smem[r], 1)], rows_vmem.at[pl.ds(r, 1)], row_sem)

    def issue(r, c):
        row_copy(r).start()
        return c

    lax.fori_loop(0, bm, issue, 0, unroll=8)
    pltpu.make_async_copy(x_hbm.at[pl.ds(0, bm)], rows_vmem, row_sem).wait()
    o_ref[...] = _rms(rows_vmem[...], g_ref[...]).astype(o_ref.dtype)


def moe_gather_norm(x, row_tok, norm_gain, layer, bm):
    T, D = x.shape
    n_blocks = row_tok.shape[0]
    return pl.pallas_call(
        functools.partial(_gather_rows_kernel, bm=bm),
        grid=(n_blocks,),
        in_specs=[pl.BlockSpec(memory_space=pl.ANY),
                  pl.BlockSpec(memory_space=pl.ANY),
                  pl.BlockSpec((None, 1, D), lambda i: (layer, 0, 0))],
        out_specs=pl.BlockSpec((bm, D), lambda i: (i, 0)),
        out_shape=jax.ShapeDtypeStruct((n_blocks * bm, D), BF16),
        scratch_shapes=[pltpu.SMEM((bm,), jnp.int32),
                        pltpu.VMEM((bm, D), F32),
                        pltpu.SemaphoreType.DMA(()),
                        pltpu.SemaphoreType.DMA(())],
        compiler_params=_cparams("arbitrary"),
        name="moe_gather_norm",
    )(row_tok, x, norm_gain)


def _expert_up_kernel(be_ref, x_ref, wg_ref, wu_ref, o_ref, wgb_ref, wub_ref):
    m = pl.program_id(1)
    new_expert = jnp.logical_or(m == 0, be_ref[m] != be_ref[jnp.maximum(m - 1, 0)])

    @pl.when(new_expert)
    def _():
        wgb_ref[...] = wg_ref[...].astype(BF16)
        wub_ref[...] = wu_ref[...].astype(BF16)

    x = x_ref[...]
    gate = _dot(x, wgb_ref[...])
    up = _dot(x, wub_ref[...])
    o_ref[...] = (gate * _sigmoid(gate) * up).astype(o_ref.dtype)


def moe_expert_up(xs, block_e, wg, wu, widx, bm):
    R, D = xs.shape
    F = wg.shape[-1]
    tf = _tile(F, 512)
    nb = R // bm
    return pl.pallas_call(
        _expert_up_kernel,
        grid_spec=pltpu.PrefetchScalarGridSpec(
            num_scalar_prefetch=1,
            grid=(F // tf, nb),
            in_specs=[pl.BlockSpec((bm, D), lambda f, m, be: (m, 0)),
                      pl.BlockSpec((None, None, D, tf), lambda f, m, be: (widx, be[m], 0, f)),
                      pl.BlockSpec((None, None, D, tf), lambda f, m, be: (widx, be[m], 0, f))],
            out_specs=pl.BlockSpec((bm, tf), lambda f, m, be: (m, f)),
            scratch_shapes=[pltpu.VMEM((D, tf), BF16), pltpu.VMEM((D, tf), BF16)]),
        out_shape=jax.ShapeDtypeStruct((R, F), BF16),
        compiler_params=_cparams("arbitrary", "arbitrary"),
        name="moe_expert_up",
    )(block_e, xs, wg, wu)


def _expert_down_kernel(be_ref, a_ref, wd_ref, o_ref):
    o_ref[...] = _dot(a_ref[...], wd_ref[...]).astype(o_ref.dtype)


def moe_expert_down(act, block_e, wd, widx, bm):
    R, F = act.shape
    D = wd.shape[-1]
    tn = _tile(D, 1024)
    nb = R // bm
    return pl.pallas_call(
        _expert_down_kernel,
        grid_spec=pltpu.PrefetchScalarGridSpec(
            num_scalar_prefetch=1,
            grid=(D // tn, nb),
            in_specs=[pl.BlockSpec((bm, F), lambda j, m, be: (m, 0)),
                      pl.BlockSpec((None, None, F, tn), lambda j, m, be: (widx, be[m], 0, j))],
            out_specs=pl.BlockSpec((bm, tn), lambda j, m, be: (m, j))),
        out_shape=jax.ShapeDtypeStruct((R, D), F32),
        compiler_params=_cparams("arbitrary", "arbitrary"),
        name="moe_expert_down",
    )(block_e, act, wd)


def _combine_kernel(dest_hbm, yb_hbm, x_ref, rt_ref, g_ref, xo_ref, ho_ref, idx_smem, rows_vmem, idx_sem, row_sem, *, tm):
    blk = pl.program_id(0)
    idx_copy = pltpu.make_async_copy(dest_hbm.at[blk], idx_smem, idx_sem)
    idx_copy.start()
    idx_copy.wait()

    def row_copy(r):
        return pltpu.make_async_copy(yb_hbm.at[pl.ds(idx_smem[r], 1)], rows_vmem.at[pl.ds(r, 1)], row_sem)

    def issue(r, c):
        row_copy(r).start()
        return c

    lax.fori_loop(0, TOP_K * tm, issue, 0, unroll=8)
    pltpu.make_async_copy(yb_hbm.at[pl.ds(0, TOP_K * tm)], rows_vmem, row_sem).wait()
    rt = rt_ref[...]
    xn = x_ref[...] + rt[:, 2:3] * rows_vmem[0:tm, :] + rt[:, 3:4] * rows_vmem[tm:2 * tm, :]
    xo_ref[...] = xn
    ho_ref[...] = _rms(xn, g_ref[...]).astype(ho_ref.dtype)


def moe_combine(x, yb, dest_blocks, route, norm_gain, layer, tm):
    T, D = x.shape
    return pl.pallas_call(
        functools.partial(_combine_kernel, tm=tm),
        grid=(T // tm,),
        in_specs=[pl.BlockSpec(memory_space=pl.ANY),
                  pl.BlockSpec(memory_space=pl.ANY),
                  pl.BlockSpec((tm, D), lambda i: (i, 0)),
                  pl.BlockSpec((tm, LANES), lambda i: (i, 0)),
                  pl.BlockSpec((None, 1, D), lambda i: (layer, 0, 0))],
        out_specs=[pl.BlockSpec((tm, D), lambda i: (i, 0)),
                   pl.BlockSpec((tm, D), lambda i: (i, 0))],
        out_shape=[jax.ShapeDtypeStruct((T, D), F32), jax.ShapeDtypeStruct((T, D), BF16)],
        scratch_shapes=[pltpu.SMEM((TOP_K * tm,), jnp.int32),
                        pltpu.VMEM((TOP_K * tm, D), F32),
                        pltpu.SemaphoreType.DMA(()),
                        pltpu.SemaphoreType.DMA(())],
        compiler_params=_cparams("arbitrary"),
        name="moe_combine",
    )(dest_blocks, yb, x, route, norm_gain)


def moe_layer(x, norm_ffn, norm_ple, w_router_pad, wg, wu, wd, layer, widx):
    T, D = x.shape
    E = N_EXPERTS
    bm = _tile(T, 512)
    route = moe_router(x, norm_ffn, w_router_pad, layer, widx)
    flat_e = route[:, :TOP_K].astype(jnp.int32).reshape(-1)
    onehot = (flat_e[:, None] == jnp.arange(E, dtype=jnp.int32)[None, :]).astype(jnp.int32)
    incl = jnp.cumsum(onehot, axis=0)
    counts = incl[-1]
    rank = jnp.sum((incl - onehot) * onehot, axis=1)
    padded = (counts + bm - 1) // bm * bm
    pad_end = jnp.cumsum(padded)
    pad_start = pad_end - padded
    dest = pad_start[flat_e] + rank
    n_rows = -(-(T * TOP_K + E * (bm - 1)) // bm) * bm
    n_blocks = n_rows // bm
    flat_tok = jnp.repeat(jnp.arange(T, dtype=jnp.int32), TOP_K)
    row_tok = jnp.zeros((n_rows,), jnp.int32).at[dest].set(flat_tok)
    block_e = jnp.minimum(
        jnp.sum(jnp.arange(n_blocks, dtype=jnp.int32)[:, None] * bm >= pad_end[None, :], axis=1), E - 1
    ).astype(jnp.int32)

    xs = moe_gather_norm(x, row_tok.reshape(n_blocks, bm), norm_ffn, layer, bm)
    act = moe_expert_up(xs, block_e, wg, wu, widx, bm)
    yb = moe_expert_down(act, block_e, wd, widx, bm)
    tmc = _tile(T, 256)
    dest_blocks = dest.reshape(T // tmc, tmc, TOP_K).transpose(0, 2, 1).reshape(T // tmc, TOP_K * tmc)
    return moe_combine(x, yb, dest_blocks, route, norm_ple, layer, tmc)


def _ple_kernel(x_ref, hp_ref, p_ref, wg_ref, wp_ref, g_ref, xo_ref, ho_ref):
    gate = _sigmoid(_dot(hp_ref[...], wg_ref[...]))
    emb = _dot(p_ref[...].astype(BF16), wp_ref[...])
    xn = x_ref[...] + emb * gate
    xo_ref[...] = xn
    ho_ref[...] = _rms(xn, g_ref[...]).astype(ho_ref.dtype)


def ple_layer(x, hp, p, w_gate_bf, w_proj_bf, next_gain, layer, gain_idx, out_dtype):
    T, D = x.shape
    Pd = p.shape[-1]
    tm = _tile(T, 512)
    return pl.pallas_call(
        _ple_kernel,
        grid=(T // tm,),
        in_specs=[pl.BlockSpec((tm, D), lambda i: (i, 0)),
                  pl.BlockSpec((tm, D), lambda i: (i, 0)),
                  pl.BlockSpec((None, tm, Pd), lambda i: (layer, i, 0)),
                  pl.BlockSpec((None, D, D), lambda i: (layer, 0, 0)),
                  pl.BlockSpec((None, Pd, D), lambda i: (layer, 0, 0)),
                  pl.BlockSpec((None, 1, D), lambda i: (gain_idx, 0, 0))],
        out_specs=[pl.BlockSpec((tm, D), lambda i: (i, 0)),
                   pl.BlockSpec((tm, D), lambda i: (i, 0))],
        out_shape=[jax.ShapeDtypeStruct((T, D), F32), jax.ShapeDtypeStruct((T, D), out_dtype)],
        compiler_params=_cparams("parallel"),
        name="ple_layer",
    )(x, hp, p, w_gate_bf, w_proj_bf, next_gain)


def _split_points(D):
    swa_q = SWA_Q_HEADS * SWA_HEAD_DIM
    swa_kv = SWA_KV_HEADS * SWA_HEAD_DIM
    dn_w = DN_HEADS * DN_HEAD_DIM
    return swa_q + 2 * swa_kv, swa_q + 2 * swa_kv + 4 * dn_w


def _pack_small_in_proj(w_in_t, q_lora, kv_lora):
    _, b_end = _split_points(w_in_t.shape[2])
    H = DN_HEADS
    o = b_end
    beta = w_in_t[:, o:o + H]
    decay = w_in_t[:, o + H:o + 2 * H]
    o += 2 * H
    cq = w_in_t[:, o:o + q_lora]
    o += q_lora
    ckv = w_in_t[:, o:o + kv_lora]
    o += kv_lora
    half = MLA_ROPE // 2
    kr1 = w_in_t[:, o:o + half]
    kr2 = w_in_t[:, o + half:o + 2 * half]
    o += MLA_ROPE
    pad = jnp.zeros((w_in_t.shape[0], LANES - 2 * H, w_in_t.shape[2]), w_in_t.dtype)
    packed = jnp.concatenate([cq, ckv, kr1, kr2, kr2, kr1, beta, decay, pad], axis=1)
    return packed, o


def _pack_w_uq(w_uq):
    Ld, R, _ = w_uq.shape
    H, half = MLA_HEADS, MLA_ROPE // 2
    w = w_uq.reshape(Ld, R, H, MLA_NOPE + MLA_ROPE)
    nope = w[..., :MLA_NOPE].reshape(Ld, R, H * MLA_NOPE)
    r1 = w[..., MLA_NOPE:MLA_NOPE + half]
    r2 = w[..., MLA_NOPE + half:]
    rope = jnp.concatenate([r1, r2, r2, r1], axis=-1).reshape(Ld, R, H * LANES)
    return jnp.concatenate([nope, rope], axis=-1).astype(BF16)


def _pack_w_ukv(w_ukv):
    Ld, R, _ = w_ukv.shape
    H = MLA_HEADS
    w = w_ukv.reshape(Ld, R, H, MLA_NOPE + MLA_V)
    kn = w[..., :MLA_NOPE].reshape(Ld, R, H * MLA_NOPE)
    vv = w[..., MLA_NOPE:].reshape(Ld, R, H * MLA_V)
    return jnp.concatenate([kn, vv], axis=-1).astype(BF16)


def _rope_tables(positions):
    half = MLA_ROPE // 2
    inv_freq = ROPE_THETA ** (-jnp.arange(half, dtype=F32) / half)
    ang = positions.astype(F32).reshape(-1)[:, None] * inv_freq
    cos, sin = jnp.cos(ang), jnp.sin(ang)
    zeros = jnp.zeros_like(cos)
    return (jnp.concatenate([cos, cos, zeros, zeros], axis=-1),
            jnp.concatenate([-sin, sin, zeros, zeros], axis=-1))


def _lane_vec(v, offset):
    Ld, n = v.shape
    out = jnp.zeros((Ld, 1, LANES), F32)
    return out.at[:, 0, offset:offset + n].set(v.astype(F32))


def kernel(x, p, positions, norm_mix, w_in, conv_w, dn_a_log, dn_dt_bias, dn_norm, swa_sinks, mla_q_norm, w_uq,
           mla_kv_norm, w_ukv, w_branch, w_out, norm_ffn, w_ffn_gate, w_ffn_up, w_ffn_down, w_router, w_exp_gate,
           w_exp_up, w_exp_down, norm_ple, w_ple_gate, w_ple_proj, final_norm):
    B, S, D = x.shape
    T = B * S
    depth = w_in.shape[0]
    q_lora, kv_lora = w_uq.shape[1], w_ukv.shape[1]
    a_end, b_end = _split_points(D)

    w_in_t = jnp.swapaxes(w_in, 1, 2)
    w_small_t, gate_row0 = _pack_small_in_proj(w_in_t, q_lora, kv_lora)
    w_gates_t = w_in_t[:, gate_row0:]
    wq_packed = _pack_w_uq(w_uq)
    wkv_packed = _pack_w_ukv(w_ukv)
    w_out_bf = w_out.astype(BF16)
    w_ple_gate_bf = w_ple_gate.astype(BF16)
    w_ple_proj_bf = w_ple_proj.astype(BF16)
    wfg, wfu, wfd = w_ffn_gate.astype(BF16), w_ffn_up.astype(BF16), w_ffn_down.astype(BF16)
    weg, weu, wed = w_exp_gate, w_exp_up, w_exp_down.astype(BF16)
    w_router_pad = jnp.pad(w_router, ((0, 0), (0, 0), (0, LANES - w_router.shape[-1])))
    cos_tab, sin_tab = _rope_tables(positions)
    alog_vec = _lane_vec(dn_a_log, DN_HEADS)
    dtb_vec = _lane_vec(dn_dt_bias, DN_HEADS)
    row = lambda g: g.reshape(g.shape[0], 1, g.shape[-1])
    norm_mix3, norm_ffn3, norm_ple3 = row(norm_mix), row(norm_ffn), row(norm_ple)
    dn_norm3, mla_q_norm3, mla_kv_norm3 = row(dn_norm), row(mla_q_norm), row(mla_kv_norm)
    final3 = final_norm.reshape(1, 1, D)

    xf = x.reshape(T, D)
    pf = p.reshape(depth, T, p.shape[-1])
    h = rmsnorm_rows(xf, norm_mix3, 0, BF16)
    tn_in = 512
    for i in range(depth):
        proj_a = matmul_ws(h, w_in_t, i, n_cols=a_end, row_block_offset=0, tn=tn_in, out_dtype=BF16)
        proj_b = matmul_ws(h, w_in_t, i, n_cols=b_end - a_end, row_block_offset=a_end // tn_in, tn=tn_in,
                           out_dtype=BF16)
        proj_small = matmul_ws(h, w_small_t, i, n_cols=w_small_t.shape[1], row_block_offset=0,
                               tn=w_small_t.shape[1], out_dtype=F32)
        gates = matmul_ws(h, w_gates_t, i, n_cols=w_gates_t.shape[1], row_block_offset=0, tn=1024,
                          out_dtype=BF16, act="sigmoid")

        y_a = swa_attention(proj_a, swa_sinks, i, S)

        qkvn, gb = dn_prep(proj_b, proj_small, conv_w, alog_vec, dtb_vec, i, S)
        nchunks = T // DN_CHUNK
        grow = gb[:, DN_HEADS:2 * DN_HEADS].reshape(nchunks, DN_CHUNK, DN_HEADS).transpose(0, 2, 1)
        y_b = dn_chunked(qkvn, proj_b, gb, grow, dn_norm3, i, B, S)

        qfull = mla_q(proj_small, mla_q_norm3, wq_packed, cos_tab, sin_tab, i)
        kfull, vfull = mla_kv(proj_small, mla_kv_norm3, wkv_packed, cos_tab, sin_tab, i)
        y_c = mla_flash(qfull, kfull, vfull, B, S)

        merged = merge_branches(y_a, y_b, y_c, gates, w_branch, i)
        xf, h2 = out_proj_residual(merged, w_out_bf, xf, norm_ffn3, i)

        j = i // 2
        if i % 2 == 0:
            xf, hp = ffn_swiglu(h2, wfg, wfu, wfd, xf, norm_ple3, j, i)
        else:
            xf, hp = moe_layer(xf, norm_ffn3, norm_ple3, w_router_pad, weg, weu, wed, i, j)

        if i + 1 < depth:
            xf, h = ple_layer(xf, hp, pf, w_ple_gate_bf, w_ple_proj_bf, norm_mix3, i, i + 1, BF16)
        else:
            xf, out = ple_layer(xf, hp, pf, w_ple_gate_bf, w_ple_proj_bf, final3, i, 0, F32)
    return out.reshape(B, S, D)
```

```python
import functools

import jax
import jax.numpy as jnp
import numpy as np
from jax import lax
from jax.experimental import pallas as pl
from jax.experimental.pallas import tpu as pltpu

BF16 = jnp.bfloat16
F32 = jnp.float32
NEG_INF = float("-inf")

NORM_EPS = 1e-6
SWA_Q_HEADS, SWA_KV_HEADS, SWA_HEAD_DIM, SWA_WINDOW = 16, 4, 64, 128
DN_HEADS, DN_HEAD_DIM, DN_CONV, DN_CHUNK = 8, 128, 4, 64
MLA_HEADS, MLA_NOPE, MLA_ROPE, MLA_V = 8, 128, 64, 128
ROPE_THETA = 10000.0
N_EXPERTS, TOP_K = 8, 2

LANES = 128
VMEM_LIMIT_BYTES = 56 * 1024 * 1024


def _cparams(*semantics):
    return pltpu.CompilerParams(dimension_semantics=semantics, vmem_limit_bytes=VMEM_LIMIT_BYTES)


def _tile(n, pref):
    t = min(n, pref)
    while n % t:
        t //= 2
    return t


def _dot(a, b):
    return jnp.dot(a, b, preferred_element_type=F32)


def _dot_nt(a, b):
    return lax.dot_general(a, b, (((1,), (1,)), ((), ())), preferred_element_type=F32)


def _dot_tn(a, b):
    return lax.dot_general(a, b, (((0,), (0,)), ((), ())), preferred_element_type=F32)


def _rms(x, gain):
    inv = lax.rsqrt(jnp.mean(x * x, axis=-1, keepdims=True) + NORM_EPS)
    return x * inv * gain


def _sigmoid(x):
    return 1.0 / (1.0 + jnp.exp(-x))


def _lane_repeat(x, n):
    return jnp.concatenate([x] * n, axis=1)


def _rmsnorm_kernel(x_ref, g_ref, o_ref):
    o_ref[...] = _rms(x_ref[...], g_ref[...]).astype(o_ref.dtype)


def rmsnorm_rows(x, gains, layer, out_dtype):
    T, D = x.shape
    tm = _tile(T, 512)
    return pl.pallas_call(
        _rmsnorm_kernel,
        grid=(T // tm,),
        in_specs=[pl.BlockSpec((tm, D), lambda i: (i, 0)),
                  pl.BlockSpec((None, 1, D), lambda i: (layer, 0, 0))],
        out_specs=pl.BlockSpec((tm, D), lambda i: (i, 0)),
        out_shape=jax.ShapeDtypeStruct((T, D), out_dtype),
        compiler_params=_cparams("parallel"),
        name="rmsnorm_rows",
    )(x, gains)


def _matmul_ws_kernel(x_ref, wt_ref, o_ref, wb_ref, *, act):
    @pl.when(pl.program_id(1) == 0)
    def _():
        wb_ref[...] = wt_ref[...].T.astype(BF16)

    acc = _dot(x_ref[...], wb_ref[...])
    if act == "sigmoid":
        acc = _sigmoid(acc)
    o_ref[...] = acc.astype(o_ref.dtype)


def matmul_ws(x, w_t, layer, *, n_cols, row_block_offset, tn, out_dtype, act=None, tm_pref=1024):
    M, K = x.shape
    tm = _tile(M, tm_pref)
    assert n_cols % tn == 0
    return pl.pallas_call(
        functools.partial(_matmul_ws_kernel, act=act),
        grid=(n_cols // tn, M // tm),
        in_specs=[pl.BlockSpec((tm, K), lambda j, i: (i, 0)),
                  pl.BlockSpec((None, tn, K), lambda j, i: (layer, j + row_block_offset, 0))],
        out_specs=pl.BlockSpec((tm, tn), lambda j, i: (i, j)),
        out_shape=jax.ShapeDtypeStruct((M, n_cols), out_dtype),
        scratch_shapes=[pltpu.VMEM((K, tn), BF16)],
        compiler_params=_cparams("arbitrary", "arbitrary"),
        name="matmul_ws",
    )(x, w_t)


def _swa_kernel(sinks_ref, q_ref, kc_ref, vc_ref, kp_ref, vp_ref, o_ref, *, layer, tiles_per_seq, rows):
    L = SWA_WINDOW
    dh = SWA_HEAD_DIM
    G = SWA_Q_HEADS // SWA_KV_HEADS
    first = (pl.program_id(0) % tiles_per_seq) == 0
    kall = jnp.concatenate([kp_ref[...], kc_ref[...]], axis=0)
    vall = jnp.concatenate([vp_ref[...], vc_ref[...]], axis=0)
    qi = lax.broadcasted_iota(jnp.int32, (L, 2 * L), 0)
    kj = lax.broadcasted_iota(jnp.int32, (L, 2 * L), 1)
    rel = qi + L - kj
    band = jnp.logical_and(rel >= 0, rel < SWA_WINDOW)
    kj_min = jnp.where(first, L, 0)
    scale = dh ** -0.5
    problems = [(b, h) for b in range(rows // L) for h in range(SWA_KV_HEADS)]
    scores = []
    for b, h in problems:
        qs = jnp.concatenate(
            [q_ref[b * L:(b + 1) * L, (h * G + g) * dh:(h * G + g + 1) * dh] for g in range(G)], axis=0)
        scores.append(_dot_nt(qs, kall[b * L:(b + 2) * L, h * dh:(h + 1) * dh]))
    probs, inv_denoms = [], []
    for (b, h), s in zip(problems, scores):
        valid = jnp.logical_and(band, kj >= kj_min) if b == 0 else band
        es, rs = [], []
        for g in range(G):
            sg = jnp.where(valid, s[g * L:(g + 1) * L] * scale, NEG_INF)
            sink = sinks_ref[layer, h * G + g]
            m = jnp.maximum(jnp.full((L, LANES), sink, F32), jnp.max(sg, axis=-1, keepdims=True))
            e = jnp.exp(sg - _lane_repeat(m, 2 * L // LANES))
            denom = jnp.sum(e, axis=-1, keepdims=True) + jnp.exp(sink - m[:, :1])
            es.append(e.astype(BF16))
            rs.append(1.0 / denom)
        probs.append(jnp.concatenate(es, axis=0))
        inv_denoms.append(rs)
    for (b, h), e, rs in zip(problems, probs, inv_denoms):
        o = _dot(e, vall[b * L:(b + 2) * L, h * dh:(h + 1) * dh])
        for g in range(G):
            hq = h * G + g
            o_ref[b * L:(b + 1) * L, hq * dh:(hq + 1) * dh] = (o[g * L:(g + 1) * L] * rs[g]).astype(o_ref.dtype)


def swa_attention(qkv, sinks, layer, seq_len):
    T = qkv.shape[0]
    L = SWA_WINDOW
    rows = _tile(seq_len, 512)
    wq = SWA_Q_HEADS * SWA_HEAD_DIM
    wkv = SWA_KV_HEADS * SWA_HEAD_DIM
    kcol, vcol = wq // wkv, wq // wkv + 1
    rpl = rows // L

    def prev_map(col):
        return lambda i: (jnp.maximum(i * rpl - 1, 0), col)

    return pl.pallas_call(
        functools.partial(_swa_kernel, layer=layer, tiles_per_seq=seq_len // rows, rows=rows),
        grid=(T // rows,),
        in_specs=[pl.BlockSpec(memory_space=pltpu.SMEM),
                  pl.BlockSpec((rows, wq), lambda i: (i, 0)),
                  pl.BlockSpec((rows, wkv), lambda i: (i, kcol)),
                  pl.BlockSpec((rows, wkv), lambda i: (i, vcol)),
                  pl.BlockSpec((L, wkv), prev_map(kcol)),
                  pl.BlockSpec((L, wkv), prev_map(vcol))],
        out_specs=pl.BlockSpec((rows, wq), lambda i: (i, 0)),
        out_shape=jax.ShapeDtypeStruct((T, wq), BF16),
        compiler_params=_cparams("parallel"),
        name="swa_attention",
    )(sinks, qkv, qkv, qkv, qkv, qkv)


def _dn_prep_kernel(cur_ref, prev_ref, cw_ref, t2_ref, alog_ref, dtb_ref, qkv_ref, gb_ref, *, tiles_per_seq, tm):
    H, dh, W = DN_HEADS, DN_HEAD_DIM, DN_CONV
    first = (pl.program_id(0) % tiles_per_seq) == 0
    prev = prev_ref[...].astype(F32)
    prev = jnp.where(first, 0.0, prev)
    xcat = jnp.concatenate([prev, cur_ref[...].astype(F32)], axis=0)
    P = prev.shape[0]
    cw = cw_ref[...]
    y = None
    for j in range(W):
        term = xcat[P - (W - 1) + j:P - (W - 1) + j + tm] * cw[j:j + 1]
        y = term if y is None else y + term
    y = y * _sigmoid(y)
    width = H * dh
    for h in range(H):
        qh = y[:, h * dh:(h + 1) * dh]
        kh = y[:, width + h * dh:width + (h + 1) * dh]
        qn = qh * lax.rsqrt(jnp.sum(qh * qh, axis=-1, keepdims=True) + 1e-6) * (dh ** -0.5)
        kn = kh * lax.rsqrt(jnp.sum(kh * kh, axis=-1, keepdims=True) + 1e-6)
        qkv_ref[:, h * dh:(h + 1) * dh] = qn.astype(qkv_ref.dtype)
        qkv_ref[:, width + h * dh:width + (h + 1) * dh] = kn.astype(qkv_ref.dtype)
    qkv_ref[:, 2 * width:] = y[:, 2 * width:].astype(qkv_ref.dtype)

    t2 = t2_ref[...]
    xs = t2 + dtb_ref[...]
    softplus = jnp.maximum(xs, 0.0) + jnp.log(1.0 + jnp.exp(-jnp.abs(xs)))
    g = -jnp.exp(alog_ref[...]) * softplus
    ri = lax.broadcasted_iota(jnp.int32, (tm, tm), 0)
    ci = lax.broadcasted_iota(jnp.int32, (tm, tm), 1)
    same_chunk = (ri // DN_CHUNK) == (ci // DN_CHUNK)
    tri = jnp.where(jnp.logical_and(same_chunk, ri >= ci), 1.0, 0.0)
    gc = jnp.dot(tri, g, preferred_element_type=F32, precision=lax.Precision.HIGHEST)
    lane = lax.broadcasted_iota(jnp.int32, t2.shape, 1)
    gb_ref[...] = jnp.where(lane < H, _sigmoid(t2), gc)


def dn_prep(proj_b, proj_small, conv_w, alog_vec, dtb_vec, layer, seq_len):
    T = proj_b.shape[0]
    width3 = 3 * DN_HEADS * DN_HEAD_DIM
    tm = _tile(seq_len, 256)
    P = 16
    tail2_col = proj_small.shape[1] // LANES - 1
    return pl.pallas_call(
        functools.partial(_dn_prep_kernel, tiles_per_seq=seq_len // tm, tm=tm),
        grid=(T // tm,),
        in_specs=[pl.BlockSpec((tm, width3), lambda i: (i, 0)),
                  pl.BlockSpec((P, width3), lambda i: (jnp.maximum(i * (tm // P) - 1, 0), 0)),
                  pl.BlockSpec((None, DN_CONV, width3), lambda i: (layer, 0, 0)),
                  pl.BlockSpec((tm, LANES), lambda i: (i, tail2_col)),
                  pl.BlockSpec((None, 1, LANES), lambda i: (layer, 0, 0)),
                  pl.BlockSpec((None, 1, LANES), lambda i: (layer, 0, 0))],
        out_specs=[pl.BlockSpec((tm, width3), lambda i: (i, 0)),
                   pl.BlockSpec((tm, LANES), lambda i: (i, 0))],
        out_shape=[jax.ShapeDtypeStruct((T, width3), BF16),
                   jax.ShapeDtypeStruct((T, LANES), F32)],
        compiler_params=_cparams("parallel"),
        name="dn_prep",
    )(proj_b, proj_b, conv_w, proj_small, alog_vec, dtb_vec)


def _unit_lower_inverse_many(ls):
    C = ls[0].shape[0]
    assert C == 64
    ri = lax.broadcasted_iota(jnp.int32, (C, C), 0)
    ci = lax.broadcasted_iota(jnp.int32, (C, C), 1)
    eye = jnp.where(ri == ci, 1.0, 0.0)

    def mm(a, b):
        return [_dot(x.astype(BF16), y.astype(BF16)) for x, y in zip(a, b)]

    l2 = mm(ls, ls)
    l4 = mm(l2, l2)
    l3 = mm(ls, l2)
    l8 = mm(l4, l4)
    x1 = [b - a - c for a, b, c in zip(ls, l2, l3)]
    l16 = mm(l8, l8)
    l12 = mm(l4, l8)
    x2 = [a + b + c for a, b, c in zip(l4, l8, l12)]
    l32 = mm(l16, l16)
    x12 = mm(x1, x2)
    y = [a + b + c for a, b, c in zip(x1, x2, x12)]
    l48 = mm(l16, l32)
    x3 = [a + b + c for a, b, c in zip(l16, l32, l48)]
    yx3 = mm(y, x3)
    return [eye + a + b + c for a, b, c in zip(y, x3, yx3)]


def _dn_chunk_kernel(qkv_ref, z_ref, gb_ref, grow_ref, norm_ref, o_ref,
                     state_ref, u_ref, w_ref, qe_ref, kd_ref, qk_ref, *, rows):
    H, dh, C = DN_HEADS, DN_HEAD_DIM, DN_CHUNK
    width = H * dh
    heads = range(H)

    @pl.when(pl.program_id(1) == 0)
    def _():
        state_ref[...] = jnp.zeros_like(state_ref)

    ri = lax.broadcasted_iota(jnp.int32, (C, C), 0)
    ci = lax.broadcasted_iota(jnp.int32, (C, C), 1)
    incl = ri >= ci
    strict = ri > ci
    gain = norm_ref[...]

    def prepare(c, carry):
        r0 = pl.multiple_of(c * C, C)
        gb = gb_ref[pl.ds(r0, C), :]
        grow_all = grow_ref[c]
        q = [qkv_ref[pl.ds(r0, C), h * dh:(h + 1) * dh] for h in heads]
        k = [qkv_ref[pl.ds(r0, C), width + h * dh:width + (h + 1) * dh] for h in heads]
        v = [qkv_ref[pl.ds(r0, C), 2 * width + h * dh:2 * width + (h + 1) * dh] for h in heads]
        beta = [gb[:, h:h + 1] for h in heads]
        gcol = [gb[:, H + h:H + h + 1] for h in heads]
        decay = [jnp.exp(jnp.where(incl, gcol[h] - grow_all[h:h + 1, :], NEG_INF)) for h in heads]
        kf = [k[h].astype(F32) for h in heads]
        kbeta = [kf[h] * beta[h] for h in heads]
        eg = [jnp.exp(gcol[h]) for h in heads]
        kk = [_dot_nt(kbeta[h].astype(BF16), k[h]) for h in heads]
        qk = [_dot_nt(q[h], k[h]) for h in heads]
        tmat = _unit_lower_inverse_many([jnp.where(strict, kk[h] * decay[h], 0.0) for h in heads])
        tb = [t.astype(BF16) for t in tmat]
        u = [_dot(tb[h], (v[h].astype(F32) * beta[h]).astype(BF16)) for h in heads]
        w = [_dot(tb[h], (kbeta[h] * eg[h]).astype(BF16)) for h in heads]
        for h in heads:
            g_last = gcol[h][C - 1:C, :]
            u_ref[h, pl.ds(r0, C), :] = u[h]
            w_ref[h, pl.ds(r0, C), :] = w[h].astype(BF16)
            qe_ref[h, pl.ds(r0, C), :] = (q[h].astype(F32) * eg[h]).astype(BF16)
            kd_ref[h, pl.ds(r0, C), :] = (kf[h] * jnp.exp(g_last - gcol[h])).astype(BF16)
            qk_ref[h, pl.ds(r0, C), :] = jnp.where(incl, qk[h] * decay[h], 0.0).astype(BF16)
        return carry

    lax.fori_loop(0, rows // C, prepare, 0)

    def scan(c, carry):
        r0 = pl.multiple_of(c * C, C)
        gb = gb_ref[pl.ds(r0, C), :]
        state = [state_ref[h] for h in heads]
        sb = [s.astype(BF16) for s in state]
        w_s = [_dot(w_ref[h, pl.ds(r0, C), :], sb[h]) for h in heads]
        q_s = [_dot(qe_ref[h, pl.ds(r0, C), :], sb[h]) for h in heads]
        vnb = [(u_ref[h, pl.ds(r0, C), :] - w_s[h]).astype(BF16) for h in heads]
        o_in = [_dot(qk_ref[h, pl.ds(r0, C), :], vnb[h]) for h in heads]
        kv = [_dot_tn(kd_ref[h, pl.ds(r0, C), :], vnb[h]) for h in heads]
        for h in heads:
            g_last = gb[C - 1:C, H + h:H + h + 1]
            state_ref[h] = state[h] * jnp.exp(g_last) + kv[h]
            z = z_ref[pl.ds(r0, C), h * dh:(h + 1) * dh].astype(F32)
            y = _rms(q_s[h] + o_in[h], gain) * (z * _sigmoid(z))
            o_ref[pl.ds(r0, C), h * dh:(h + 1) * dh] = y.astype(o_ref.dtype)
        return carry

    lax.fori_loop(0, rows // C, scan, 0)


def dn_chunked(qkvn, proj_b, gb, grow, dn_norm, layer, batch, seq_len):
    T = qkvn.shape[0]
    H, dh, C = DN_HEADS, DN_HEAD_DIM, DN_CHUNK
    width = H * dh
    rows = _tile(seq_len, 512)
    tps = seq_len // rows
    zcol = 3
    return pl.pallas_call(
        functools.partial(_dn_chunk_kernel, rows=rows),
        grid=(batch, tps),
        in_specs=[pl.BlockSpec((rows, 3 * width), lambda b, n: (b * tps + n, 0)),
                  pl.BlockSpec((rows, width), lambda b, n: (b * tps + n, zcol)),
                  pl.BlockSpec((rows, LANES), lambda b, n: (b * tps + n, 0)),
                  pl.BlockSpec((rows // C, H, C), lambda b, n: (b * tps + n, 0, 0)),
                  pl.BlockSpec((None, 1, dh), lambda b, n: (layer, 0, 0))],
        out_specs=pl.BlockSpec((rows, width), lambda b, n: (b * tps + n, 0)),
        out_shape=jax.ShapeDtypeStruct((T, width), BF16),
        scratch_shapes=[pltpu.VMEM((H, dh, dh), F32),
                        pltpu.VMEM((H, rows, dh), F32),
                        pltpu.VMEM((H, rows, dh), BF16),
                        pltpu.VMEM((H, rows, dh), BF16),
                        pltpu.VMEM((H, rows, dh), BF16),
                        pltpu.VMEM((H, rows, C), BF16)],
        compiler_params=_cparams("parallel", "arbitrary"),
        name="dn_chunked",
    )(qkvn, proj_b, gb, grow, dn_norm)


def _rotate_pairs(x, cos_tab, sin_tab):
    return x * cos_tab + pltpu.roll(x, LANES // 2, axis=1) * sin_tab


def _mla_q_kernel(c_ref, g_ref, w_ref, cos_ref, sin_ref, o_ref):
    H = MLA_HEADS
    scale = (MLA_NOPE + MLA_ROPE) ** -0.5 * float(np.log2(np.e))
    cn = _rms(c_ref[...], g_ref[...]).astype(BF16)
    acc = _dot(cn, w_ref[...]) * scale
    cos_tab, sin_tab = cos_ref[...], sin_ref[...]
    for h in range(H):
        o_ref[:, 2 * h * LANES:(2 * h + 1) * LANES] = acc[:, h * LANES:(h + 1) * LANES].astype(o_ref.dtype)
        xr = acc[:, (H + h) * LANES:(H + h + 1) * LANES]
        o_ref[:, (2 * h + 1) * LANES:(2 * h + 2) * LANES] = _rotate_pairs(xr, cos_tab, sin_tab).astype(o_ref.dtype)


def mla_q(proj_small, gains, wq, cos_tab, sin_tab, layer):
    T = proj_small.shape[0]
    R = wq.shape[1]
    tm = _tile(T, 512)
    N = wq.shape[2]
    return pl.pallas_call(
        _mla_q_kernel,
        grid=(T // tm,),
        in_specs=[pl.BlockSpec((tm, R), lambda i: (i, 0)),
                  pl.BlockSpec((None, 1, R), lambda i: (layer, 0, 0)),
                  pl.BlockSpec((None, R, N), lambda i: (layer, 0, 0)),
                  pl.BlockSpec((tm, LANES), lambda i: (i, 0)),
                  pl.BlockSpec((tm, LANES), lambda i: (i, 0))],
        out_specs=pl.BlockSpec((tm, N), lambda i: (i, 0)),
        out_shape=jax.ShapeDtypeStruct((T, N), BF16),
        compiler_params=_cparams("parallel"),
        name="mla_q",
    )(proj_small, gains, wq, cos_tab, sin_tab)


def _mla_kv_kernel(c_ref, kr_ref, g_ref, w_ref, cos_ref, sin_ref, k_ref, v_ref):
    H = MLA_HEADS
    cn = _rms(c_ref[...], g_ref[...]).astype(BF16)
    acc = _dot(cn, w_ref[...])
    kr = _rotate_pairs(kr_ref[...], cos_ref[...], sin_ref[...]).astype(k_ref.dtype)
    for h in range(H):
        k_ref[:, 2 * h * LANES:(2 * h + 1) * LANES] = acc[:, h * LANES:(h + 1) * LANES].astype(k_ref.dtype)
        k_ref[:, (2 * h + 1) * LANES:(2 * h + 2) * LANES] = kr
    v_ref[...] = acc[:, H * LANES:].astype(v_ref.dtype)


def mla_kv(proj_small, gains, wkv, cos_tab, sin_tab, layer):
    T = proj_small.shape[0]
    R = wkv.shape[1]
    tm = _tile(T, 512)
    N = wkv.shape[2]
    H = MLA_HEADS
    kr_col = (2 * R) // LANES
    return pl.pallas_call(
        _mla_kv_kernel,
        grid=(T // tm,),
        in_specs=[pl.BlockSpec((tm, R), lambda i: (i, 1)),
                  pl.BlockSpec((tm, LANES), lambda i: (i, kr_col)),
                  pl.BlockSpec((None, 1, R), lambda i: (layer, 0, 0)),
                  pl.BlockSpec((None, R, N), lambda i: (layer, 0, 0)),
                  pl.BlockSpec((tm, LANES), lambda i: (i, 0)),
                  pl.BlockSpec((tm, LANES), lambda i: (i, 0))],
        out_specs=[pl.BlockSpec((tm, 2 * H * LANES), lambda i: (i, 0)),
                   pl.BlockSpec((tm, H * MLA_V), lambda i: (i, 0))],
        out_shape=[jax.ShapeDtypeStruct((T, 2 * H * LANES), BF16),
                   jax.ShapeDtypeStruct((T, H * MLA_V), BF16)],
        compiler_params=_cparams("parallel"),
        name="mla_kv",
    )(proj_small, proj_small, gains, wkv, cos_tab, sin_tab)


def _flash_kernel(qi_ref, kj_ref, q_ref, k_ref, v_ref, o_ref, m_ref, l_ref, acc_ref, *, tq, tk):
    H, dv = MLA_HEADS, MLA_V
    dqk = 2 * LANES
    qi = qi_ref[pl.program_id(1)]
    kj = kj_ref[pl.program_id(1)]

    @pl.when(kj == 0)
    def _():
        m_ref[...] = jnp.full_like(m_ref, NEG_INF)
        l_ref[...] = jnp.zeros_like(l_ref)
        acc_ref[...] = jnp.zeros_like(acc_ref)

    def step(masked):
        if masked:
            row = lax.broadcasted_iota(jnp.int32, (tq, tk), 0)
            col = lax.broadcasted_iota(jnp.int32, (tq, tk), 1)
            keep = col <= row
        scores = [_dot_nt(q_ref[:, h * dqk:(h + 1) * dqk], k_ref[:, h * dqk:(h + 1) * dqk]) for h in range(H)]
        probs = []
        for h in range(H):
            s = scores[h]
            if masked:
                s = jnp.where(keep, s, NEG_INF)
            m_prev = m_ref[h]
            m_new = jnp.maximum(m_prev, jnp.max(s, axis=-1, keepdims=True))
            alpha = jnp.exp2(m_prev - m_new)
            p = jnp.exp2(s - _lane_repeat(m_new, tk // LANES))
            l_part = p[:, 0:LANES]
            for t in range(1, tk // LANES):
                l_part = l_part + p[:, t * LANES:(t + 1) * LANES]
            l_ref[h] = alpha * l_ref[h] + l_part
            m_ref[h] = m_new
            acc_ref[h] = alpha * acc_ref[h]
            probs.append(p.astype(BF16))
        for h in range(H):
            acc_ref[h] += _dot(probs[h], v_ref[:, h * dv:(h + 1) * dv])

    @pl.when(kj < qi)
    def _():
        step(False)

    @pl.when(kj == qi)
    def _():
        step(True)
        for h in range(H):
            l = jnp.sum(l_ref[h], axis=-1, keepdims=True)
            o_ref[:, h * dv:(h + 1) * dv] = (acc_ref[h] / l).astype(o_ref.dtype)


def mla_flash(qf, kf, v, batch, seq_len):
    T = qf.shape[0]
    H, dv = MLA_HEADS, MLA_V
    t = _tile(seq_len, 512)
    nq = seq_len // t
    pairs = [(i, j) for i in range(nq) for j in range(i + 1)]
    qi_tab = jnp.asarray([i for i, _ in pairs], jnp.int32)
    kj_tab = jnp.asarray([j for _, j in pairs], jnp.int32)
    return pl.pallas_call(
        functools.partial(_flash_kernel, tq=t, tk=t),
        grid_spec=pltpu.PrefetchScalarGridSpec(
            num_scalar_prefetch=2,
            grid=(batch, len(pairs)),
            in_specs=[pl.BlockSpec((t, qf.shape[1]), lambda b, n, qi, kj: (b * nq + qi[n], 0)),
                      pl.BlockSpec((t, kf.shape[1]), lambda b, n, qi, kj: (b * nq + kj[n], 0)),
                      pl.BlockSpec((t, v.shape[1]), lambda b, n, qi, kj: (b * nq + kj[n], 0))],
            out_specs=pl.BlockSpec((t, H * dv), lambda b, n, qi, kj: (b * nq + qi[n], 0)),
            scratch_shapes=[pltpu.VMEM((H, t, LANES), F32),
                            pltpu.VMEM((H, t, LANES), F32),
                            pltpu.VMEM((H, t, dv), F32)]),
        out_shape=jax.ShapeDtypeStruct((T, H * dv), BF16),
        compiler_params=_cparams("parallel", "arbitrary"),
        name="mla_flash",
    )(qi_tab, kj_tab, qf, kf, v)


def _merge_kernel(ya_ref, yb_ref, yc_ref, ga_ref, gb_ref, gc_ref, wa_ref, wb_ref, wc_ref, o_ref, sa_ref, sb_ref, sc_ref):
    @pl.when(pl.program_id(1) == 0)
    def _():
        sa_ref[...] = wa_ref[...].astype(BF16)
        sb_ref[...] = wb_ref[...].astype(BF16)
        sc_ref[...] = wc_ref[...].astype(BF16)

    acc = ga_ref[...].astype(F32) * _dot(ya_ref[...], sa_ref[...])
    acc = acc + gb_ref[...].astype(F32) * _dot(yb_ref[...], sb_ref[...])
    acc = acc + gc_ref[...].astype(F32) * _dot(yc_ref[...], sc_ref[...])
    o_ref[...] = acc.astype(o_ref.dtype)


def merge_branches(ya, yb, yc, gates, w_branch, layer):
    T, Wb = ya.shape
    D = w_branch.shape[-1]
    tm = _tile(T, 1024)
    tn = _tile(D, 512)
    nj = D // tn
    y_spec = pl.BlockSpec((tm, Wb), lambda j, i: (i, 0))

    def g_spec(n):
        return pl.BlockSpec((tm, tn), lambda j, i: (i, n * nj + j))

    def w_spec(n):
        return pl.BlockSpec((None, None, Wb, tn), lambda j, i: (layer, n, 0, j))

    return pl.pallas_call(
        _merge_kernel,
        grid=(nj, T // tm),
        in_specs=[y_spec, y_spec, y_spec, g_spec(0), g_spec(1), g_spec(2), w_spec(0), w_spec(1), w_spec(2)],
        out_specs=pl.BlockSpec((tm, tn), lambda j, i: (i, j)),
        out_shape=jax.ShapeDtypeStruct((T, D), BF16),
        scratch_shapes=[pltpu.VMEM((Wb, tn), BF16)] * 3,
        compiler_params=_cparams("arbitrary", "arbitrary"),
        name="merge_branches",
    )(ya, yb, yc, gates, gates, gates, w_branch, w_branch, w_branch)


def _out_proj_kernel(m_ref, w_ref, x_ref, g_ref, xo_ref, *maybe_ho_ref):
    xn = x_ref[...] + _dot(m_ref[...], w_ref[...])
    xo_ref[...] = xn
    for ho_ref in maybe_ho_ref:
        ho_ref[...] = _rms(xn, g_ref[...]).astype(ho_ref.dtype)


def out_proj_residual(merged, w_out_bf, x, norm_gain, layer, emit_norm):
    T, D = x.shape
    tm = _tile(T, 512)
    row_spec = pl.BlockSpec((tm, D), lambda i: (i, 0))
    out_specs = [row_spec, row_spec] if emit_norm else [row_spec]
    out_shape = [jax.ShapeDtypeStruct((T, D), F32)] + ([jax.ShapeDtypeStruct((T, D), BF16)] if emit_norm else [])
    return pl.pallas_call(
        _out_proj_kernel,
        grid=(T // tm,),
        in_specs=[row_spec,
                  pl.BlockSpec((None, D, D), lambda i: (layer, 0, 0)),
                  row_spec,
                  pl.BlockSpec((None, 1, D), lambda i: (layer, 0, 0))],
        out_specs=out_specs,
        out_shape=out_shape,
        compiler_params=_cparams("parallel"),
        name="out_proj_residual",
    )(merged, w_out_bf, x, norm_gain)


def _ffn_kernel(h_ref, wg_ref, wu_ref, wd_ref, y_ref, acc_ref):
    f = pl.program_id(1)

    @pl.when(f == 0)
    def _():
        acc_ref[...] = jnp.zeros_like(acc_ref)

    h = h_ref[...]
    gate = _dot(h, wg_ref[...])
    up = _dot(h, wu_ref[...])
    act = (gate * _sigmoid(gate) * up).astype(BF16)
    acc_ref[...] += _dot(act, wd_ref[...])

    @pl.when(f == pl.num_programs(1) - 1)
    def _():
        y_ref[...] = acc_ref[...].astype(y_ref.dtype)


def ffn_swiglu(h, wg, wu, wd, widx):
    T, D = h.shape
    F = wg.shape[-1]
    tm = _tile(T, 512)
    tf = _tile(F, 1024)
    return pl.pallas_call(
        _ffn_kernel,
        grid=(T // tm, F // tf),
        in_specs=[pl.BlockSpec((tm, D), lambda i, f: (i, 0)),
                  pl.BlockSpec((None, D, tf), lambda i, f: (widx, 0, f)),
                  pl.BlockSpec((None, D, tf), lambda i, f: (widx, 0, f)),
                  pl.BlockSpec((None, tf, D), lambda i, f: (widx, f, 0))],
        out_specs=pl.BlockSpec((tm, D), lambda i, f: (i, 0)),
        out_shape=jax.ShapeDtypeStruct((T, D), BF16),
        scratch_shapes=[pltpu.VMEM((tm, D), F32)],
        compiler_params=_cparams("parallel", "arbitrary"),
        name="ffn_swiglu",
    )(h, wg, wu, wd)


def _router_kernel(x_ref, g_ref, w_ref, o_ref):
    h = _rms(x_ref[...], g_ref[...])
    logits = jnp.dot(h, w_ref[...], preferred_element_type=F32, precision=lax.Precision.HIGHEST)
    lane = lax.broadcasted_iota(jnp.int32, logits.shape, 1)
    lanef = lane.astype(F32)
    big = float(LANES)
    logits = jnp.where(lane < N_EXPERTS, logits, NEG_INF)
    l1 = jnp.max(logits, axis=-1, keepdims=True)
    i1 = jnp.min(jnp.where(logits == l1, lanef, big), axis=-1, keepdims=True)
    rest = jnp.where(lanef == i1, NEG_INF, logits)
    l2 = jnp.max(rest, axis=-1, keepdims=True)
    i2 = jnp.min(jnp.where(rest == l2, lanef, big), axis=-1, keepdims=True)
    e2 = jnp.exp(l2 - l1)
    w1 = 1.0 / (1.0 + e2)
    w2 = e2 / (1.0 + e2)
    out = jnp.where(lane == 0, i1, jnp.where(lane == 1, i2, jnp.where(lane == 2, w1, jnp.where(lane == 3, w2, 0.0))))
    o_ref[...] = out


def moe_router(x, norm_gain, w_router_pad, layer, widx):
    T, D = x.shape
    tm = _tile(T, 512)
    return pl.pallas_call(
        _router_kernel,
        grid=(T // tm,),
        in_specs=[pl.BlockSpec((tm, D), lambda i: (i, 0)),
                  pl.BlockSpec((None, 1, D), lambda i: (layer, 0, 0)),
                  pl.BlockSpec((None, D, LANES), lambda i: (widx, 0, 0))],
        out_specs=pl.BlockSpec((tm, LANES), lambda i: (i, 0)),
        out_shape=jax.ShapeDtypeStruct((T, LANES), F32),
        compiler_params=_cparams("parallel"),
        name="moe_router",
    )(x, norm_gain, w_router_pad)


def _start_row_gather(idx_hbm, src_hbm, blk, slot, idx_smem, rows_vmem, idx_sem, row_sem, n):
    idx_copy = pltpu.make_async_copy(idx_hbm.at[blk], idx_smem.at[slot], idx_sem.at[slot])
    idx_copy.start()
    idx_copy.wait()

    def issue(i, c):
        for priority in range(2):
            r = 2 * i + priority
            pltpu.make_async_copy(src_hbm.at[pl.ds(idx_smem[slot, r], 1)],
                                  rows_vmem.at[slot, pl.ds(r, 1)],
                                  row_sem.at[slot]).start(priority=priority)
        return c

    lax.fori_loop(0, n // 2, issue, 0, unroll=4)


def _wait_row_gather(src_hbm, slot, rows_vmem, row_sem, n):
    pltpu.make_async_copy(src_hbm.at[pl.ds(0, n)], rows_vmem.at[slot], row_sem.at[slot]).wait()


def _gather_rows_kernel(meta_ref, idx_hbm, x_hbm, g_ref, o_ref, idx_smem, rows_vmem, idx_sem, row_sem, *, bm):
    m = pl.program_id(0)
    n_used = meta_ref[pl.num_programs(0)]
    slot = m % 2
    start = functools.partial(_start_row_gather, idx_hbm, x_hbm, idx_smem=idx_smem, rows_vmem=rows_vmem,
                              idx_sem=idx_sem, row_sem=row_sem, n=bm)

    @pl.when(m == 0)
    def _():
        start(blk=0, slot=0)

    @pl.when(m + 1 < n_used)
    def _():
        start(blk=m + 1, slot=1 - slot)

    @pl.when(m < n_used)
    def _():
        _wait_row_gather(x_hbm, slot, rows_vmem, row_sem, bm)
        o_ref[...] = _rms(rows_vmem[slot], g_ref[...]).astype(o_ref.dtype)

    @pl.when(m >= n_used)
    def _():
        o_ref[...] = jnp.zeros_like(o_ref)


def moe_gather_norm(x, row_tok, meta, norm_gain, layer, bm):
    T, D = x.shape
    n_blocks = row_tok.shape[0]
    return pl.pallas_call(
        functools.partial(_gather_rows_kernel, bm=bm),
        grid_spec=pltpu.PrefetchScalarGridSpec(
            num_scalar_prefetch=1,
            grid=(n_blocks,),
            in_specs=[pl.BlockSpec(memory_space=pl.ANY),
                      pl.BlockSpec(memory_space=pl.ANY),
                      pl.BlockSpec((None, 1, D), lambda i, meta: (layer, 0, 0))],
            out_specs=pl.BlockSpec((bm, D), lambda i, meta: (i, 0)),
            scratch_shapes=[pltpu.SMEM((2, bm), jnp.int32),
                            pltpu.VMEM((2, bm, D), F32),
                            pltpu.SemaphoreType.DMA((2,)),
                            pltpu.SemaphoreType.DMA((2,))]),
        out_shape=jax.ShapeDtypeStruct((n_blocks * bm, D), BF16),
        compiler_params=_cparams("arbitrary"),
        name="moe_gather_norm",
    )(meta, row_tok, x, norm_gain)


def _expert_up_kernel(meta_ref, x_ref, wg_ref, wu_ref, o_ref, wgb_ref, wub_ref):
    m = pl.program_id(1)
    n_used = meta_ref[pl.num_programs(1)]
    new_expert = jnp.logical_or(m == 0, meta_ref[m] != meta_ref[jnp.maximum(m - 1, 0)])

    @pl.when(jnp.logical_and(new_expert, m < n_used))
    def _():
        wgb_ref[...] = wg_ref[...].astype(BF16)
        wub_ref[...] = wu_ref[...].astype(BF16)

    @pl.when(m < n_used)
    def _():
        x = x_ref[...]
        gate = _dot(x, wgb_ref[...])
        up = _dot(x, wub_ref[...])
        o_ref[...] = (gate * _sigmoid(gate) * up).astype(o_ref.dtype)

    @pl.when(m >= n_used)
    def _():
        o_ref[...] = jnp.zeros_like(o_ref)


def _last_used(m, meta, nb):
    return jnp.minimum(m, meta[nb] - 1)


def moe_expert_up(xs, meta, wg, wu, widx, bm):
    R, D = xs.shape
    F = wg.shape[-1]
    tf = _tile(F, 1024)
    nb = R // bm

    def w_map(f, m, meta):
        return (widx, meta[_last_used(m, meta, nb)], 0, f)

    return pl.pallas_call(
        _expert_up_kernel,
        grid_spec=pltpu.PrefetchScalarGridSpec(
            num_scalar_prefetch=1,
            grid=(F // tf, nb),
            in_specs=[pl.BlockSpec((bm, D), lambda f, m, meta: (_last_used(m, meta, nb), 0)),
                      pl.BlockSpec((None, None, D, tf), w_map),
                      pl.BlockSpec((None, None, D, tf), w_map)],
            out_specs=pl.BlockSpec((bm, tf), lambda f, m, meta: (m, f)),
            scratch_shapes=[pltpu.VMEM((D, tf), BF16), pltpu.VMEM((D, tf), BF16)]),
        out_shape=jax.ShapeDtypeStruct((R, F), BF16),
        compiler_params=_cparams("arbitrary", "arbitrary"),
        name="moe_expert_up",
    )(meta, xs, wg, wu)


def _expert_down_kernel(meta_ref, a_ref, wd_ref, o_ref):
    m = pl.program_id(1)
    n_used = meta_ref[pl.num_programs(1)]

    @pl.when(m < n_used)
    def _():
        o_ref[...] = _dot(a_ref[...], wd_ref[...]).astype(o_ref.dtype)

    @pl.when(m >= n_used)
    def _():
        o_ref[...] = jnp.zeros_like(o_ref)


def moe_expert_down(act, meta, wd, widx, bm):
    R, F = act.shape
    D = wd.shape[-1]
    tn = _tile(D, 1024)
    nb = R // bm
    return pl.pallas_call(
        _expert_down_kernel,
        grid_spec=pltpu.PrefetchScalarGridSpec(
            num_scalar_prefetch=1,
            grid=(D // tn, nb),
            in_specs=[pl.BlockSpec((bm, F), lambda j, m, meta: (_last_used(m, meta, nb), 0)),
                      pl.BlockSpec((None, None, F, tn),
                                   lambda j, m, meta: (widx, meta[_last_used(m, meta, nb)], 0, j))],
            out_specs=pl.BlockSpec((bm, tn), lambda j, m, meta: (m, j))),
        out_shape=jax.ShapeDtypeStruct((R, D), F32),
        compiler_params=_cparams("arbitrary", "arbitrary"),
        name="moe_expert_down",
    )(meta, act, wd)


def _combine_kernel(dest_hbm, yb_hbm, rt_ref, y_ref, idx_smem, rows_vmem, idx_sem, row_sem, *, tm):
    m = pl.program_id(0)
    slot = m % 2
    n = TOP_K * tm
    start = functools.partial(_start_row_gather, dest_hbm, yb_hbm, idx_smem=idx_smem, rows_vmem=rows_vmem,
                              idx_sem=idx_sem, row_sem=row_sem, n=n)

    @pl.when(m == 0)
    def _():
        start(blk=0, slot=0)

    @pl.when(m + 1 < pl.num_programs(0))
    def _():
        start(blk=m + 1, slot=1 - slot)

    _wait_row_gather(yb_hbm, slot, rows_vmem, row_sem, n)
    rt = rt_ref[...]
    y = rt[:, 2:3] * rows_vmem[slot, 0:tm, :] + rt[:, 3:4] * rows_vmem[slot, tm:2 * tm, :]
    y_ref[...] = y.astype(y_ref.dtype)


def moe_combine(yb, dest_blocks, route, tm):
    T = route.shape[0]
    D = yb.shape[1]
    return pl.pallas_call(
        functools.partial(_combine_kernel, tm=tm),
        grid=(T // tm,),
        in_specs=[pl.BlockSpec(memory_space=pl.ANY),
                  pl.BlockSpec(memory_space=pl.ANY),
                  pl.BlockSpec((tm, LANES), lambda i: (i, 0))],
        out_specs=pl.BlockSpec((tm, D), lambda i: (i, 0)),
        out_shape=jax.ShapeDtypeStruct((T, D), BF16),
        scratch_shapes=[pltpu.SMEM((2, TOP_K * tm), jnp.int32),
                        pltpu.VMEM((2, TOP_K * tm, D), F32),
                        pltpu.SemaphoreType.DMA((2,)),
                        pltpu.SemaphoreType.DMA((2,))],
        compiler_params=_cparams("arbitrary"),
        name="moe_combine",
    )(dest_blocks, yb, route)


def moe_layer(x, norm_ffn, w_router_pad, wg, wu, wd, layer, widx):
    T, D = x.shape
    E = N_EXPERTS
    bm = _tile(T, 512)
    route = moe_router(x, norm_ffn, w_router_pad, layer, widx)
    flat_e = route[:, :TOP_K].astype(jnp.int32).reshape(-1)
    onehot = (flat_e[:, None] == jnp.arange(E, dtype=jnp.int32)[None, :]).astype(jnp.int32)
    incl = jnp.cumsum(onehot, axis=0)
    counts = incl[-1]
    rank = jnp.sum((incl - onehot) * onehot, axis=1)
    padded = (counts + bm - 1) // bm * bm
    pad_end = jnp.cumsum(padded)
    pad_start = pad_end - padded
    dest = pad_start[flat_e] + rank
    n_rows = -(-(T * TOP_K + E * (bm - 1)) // bm) * bm
    n_blocks = n_rows // bm
    flat_tok = jnp.repeat(jnp.arange(T, dtype=jnp.int32), TOP_K)
    row_tok = jnp.zeros((n_rows,), jnp.int32).at[dest].set(flat_tok)
    block_e = jnp.minimum(
        jnp.sum(jnp.arange(n_blocks, dtype=jnp.int32)[:, None] * bm >= pad_end[None, :], axis=1), E - 1
    ).astype(jnp.int32)
    meta = jnp.concatenate([block_e, (pad_end[-1:] // bm).astype(jnp.int32)])

    xs = moe_gather_norm(x, row_tok.reshape(n_blocks, bm), meta, norm_ffn, layer, bm)
    act = moe_expert_up(xs, meta, wg, wu, widx, bm)
    yb = moe_expert_down(act, meta, wd, widx, bm)
    tmc = _tile(T, 256)
    dest_blocks = dest.reshape(T // tmc, tmc, TOP_K).transpose(0, 2, 1).reshape(T // tmc, TOP_K * tmc)
    return moe_combine(yb, dest_blocks, route, tmc)


def _ple_kernel(x_ref, y_ref, p_ref, wg_ref, wp_ref, gp_ref, gn_ref, xo_ref, ho_ref):
    x = x_ref[...] + y_ref[...].astype(F32)
    hp = _rms(x, gp_ref[...]).astype(BF16)
    gate = _sigmoid(_dot(hp, wg_ref[...]))
    emb = _dot(p_ref[...].astype(BF16), wp_ref[...])
    xn = x + emb * gate
    xo_ref[...] = xn
    ho_ref[...] = _rms(xn, gn_ref[...]).astype(ho_ref.dtype)


def ple_layer(x, y, p, w_gate_bf, w_proj_bf, ple_gain, next_gain, layer, gain_idx, out_dtype):
    T, D = x.shape
    Pd = p.shape[-1]
    tm = _tile(T, 512)
    row_spec = pl.BlockSpec((tm, D), lambda i: (i, 0))
    return pl.pallas_call(
        _ple_kernel,
        grid=(T // tm,),
        in_specs=[row_spec,
                  row_spec,
                  pl.BlockSpec((None, tm, Pd), lambda i: (layer, i, 0)),
                  pl.BlockSpec((None, D, D), lambda i: (layer, 0, 0)),
                  pl.BlockSpec((None, Pd, D), lambda i: (layer, 0, 0)),
                  pl.BlockSpec((None, 1, D), lambda i: (layer, 0, 0)),
                  pl.BlockSpec((None, 1, D), lambda i: (gain_idx, 0, 0))],
        out_specs=[row_spec, row_spec],
        out_shape=[jax.ShapeDtypeStruct((T, D), F32), jax.ShapeDtypeStruct((T, D), out_dtype)],
        compiler_params=_cparams("parallel"),
        name="ple_layer",
    )(x, y, p, w_gate_bf, w_proj_bf, ple_gain, next_gain)


def _split_points(D):
    swa_q = SWA_Q_HEADS * SWA_HEAD_DIM
    swa_kv = SWA_KV_HEADS * SWA_HEAD_DIM
    dn_w = DN_HEADS * DN_HEAD_DIM
    return swa_q + 2 * swa_kv, swa_q + 2 * swa_kv + 4 * dn_w


def _pack_small_in_proj(w_in_t, q_lora, kv_lora):
    _, b_end = _split_points(w_in_t.shape[2])
    H = DN_HEADS
    o = b_end
    beta = w_in_t[:, o:o + H]
    decay = w_in_t[:, o + H:o + 2 * H]
    o += 2 * H
    cq = w_in_t[:, o:o + q_lora]
    o += q_lora
    ckv = w_in_t[:, o:o + kv_lora]
    o += kv_lora
    half = MLA_ROPE // 2
    kr1 = w_in_t[:, o:o + half]
    kr2 = w_in_t[:, o + half:o + 2 * half]
    o += MLA_ROPE
    pad = jnp.zeros((w_in_t.shape[0], LANES - 2 * H, w_in_t.shape[2]), w_in_t.dtype)
    packed = jnp.concatenate([cq, ckv, kr1, kr2, kr2, kr1, beta, decay, pad], axis=1)
    return packed, o


def _pack_w_uq(w_uq):
    Ld, R, _ = w_uq.shape
    H, half = MLA_HEADS, MLA_ROPE // 2
    w = w_uq.reshape(Ld, R, H, MLA_NOPE + MLA_ROPE)
    nope = w[..., :MLA_NOPE].reshape(Ld, R, H * MLA_NOPE)
    r1 = w[..., MLA_NOPE:MLA_NOPE + half]
    r2 = w[..., MLA_NOPE + half:]
    rope = jnp.concatenate([r1, r2, r2, r1], axis=-1).reshape(Ld, R, H * LANES)
    return jnp.concatenate([nope, rope], axis=-1).astype(BF16)


def _pack_w_ukv(w_ukv):
    Ld, R, _ = w_ukv.shape
    H = MLA_HEADS
    w = w_ukv.reshape(Ld, R, H, MLA_NOPE + MLA_V)
    kn = w[..., :MLA_NOPE].reshape(Ld, R, H * MLA_NOPE)
    vv = w[..., MLA_NOPE:].reshape(Ld, R, H * MLA_V)
    return jnp.concatenate([kn, vv], axis=-1).astype(BF16)


def _rope_tables(positions):
    half = MLA_ROPE // 2
    inv_freq = ROPE_THETA ** (-jnp.arange(half, dtype=F32) / half)
    ang = positions.astype(F32).reshape(-1)[:, None] * inv_freq
    cos, sin = jnp.cos(ang), jnp.sin(ang)
    zeros = jnp.zeros_like(cos)
    return (jnp.concatenate([cos, cos, zeros, zeros], axis=-1),
            jnp.concatenate([-sin, sin, zeros, zeros], axis=-1))


def _lane_vec(v, offset):
    Ld, n = v.shape
    out = jnp.zeros((Ld, 1, LANES), F32)
    return out.at[:, 0, offset:offset + n].set(v.astype(F32))


def kernel(x, p, positions, norm_mix, w_in, conv_w, dn_a_log, dn_dt_bias, dn_norm, swa_sinks, mla_q_norm, w_uq,
           mla_kv_norm, w_ukv, w_branch, w_out, norm_ffn, w_ffn_gate, w_ffn_up, w_ffn_down, w_router, w_exp_gate,
           w_exp_up, w_exp_down, norm_ple, w_ple_gate, w_ple_proj, final_norm):
    B, S, D = x.shape
    T = B * S
    depth = w_in.shape[0]
    q_lora, kv_lora = w_uq.shape[1], w_ukv.shape[1]
    a_end, b_end = _split_points(D)

    w_in_t = jnp.swapaxes(w_in, 1, 2)
    w_small_t, gate_row0 = _pack_small_in_proj(w_in_t, q_lora, kv_lora)
    w_gates_t = w_in_t[:, gate_row0:]
    wq_packed = _pack_w_uq(w_uq)
    wkv_packed = _pack_w_ukv(w_ukv)
    w_out_bf = w_out.astype(BF16)
    w_ple_gate_bf = w_ple_gate.astype(BF16)
    w_ple_proj_bf = w_ple_proj.astype(BF16)
    wfg, wfu, wfd = w_ffn_gate.astype(BF16), w_ffn_up.astype(BF16), w_ffn_down.astype(BF16)
    weg, weu, wed = w_exp_gate, w_exp_up, w_exp_down.astype(BF16)
    w_router_pad = jnp.pad(w_router, ((0, 0), (0, 0), (0, LANES - w_router.shape[-1])))
    cos_tab, sin_tab = _rope_tables(positions)
    alog_vec = _lane_vec(dn_a_log, DN_HEADS)
    dtb_vec = _lane_vec(dn_dt_bias, DN_HEADS)
    row = lambda g: g.reshape(g.shape[0], 1, g.shape[-1])
    norm_mix3, norm_ffn3, norm_ple3 = row(norm_mix), row(norm_ffn), row(norm_ple)
    dn_norm3, mla_q_norm3, mla_kv_norm3 = row(dn_norm), row(mla_q_norm), row(mla_kv_norm)
    final3 = final_norm.reshape(1, 1, D)

    xf = x.reshape(T, D)
    pf = p.reshape(depth, T, p.shape[-1])
    h = rmsnorm_rows(xf, norm_mix3, 0, BF16)
    for i in range(depth):
        proj_a = matmul_ws(h, w_in_t, i, n_cols=a_end, row_block_offset=0, tn=a_end // 2, out_dtype=BF16,
                           tm_pref=2048)
        tn_b = 512
        proj_b = matmul_ws(h, w_in_t, i, n_cols=b_end - a_end, row_block_offset=a_end // tn_b, tn=tn_b,
                           out_dtype=BF16, tm_pref=2048)
        proj_small = matmul_ws(h, w_small_t, i, n_cols=w_small_t.shape[1], row_block_offset=0,
                               tn=w_small_t.shape[1], out_dtype=F32)
        gates = matmul_ws(h, w_gates_t, i, n_cols=w_gates_t.shape[1], row_block_offset=0, tn=1024,
                          out_dtype=BF16, act="sigmoid")

        y_a = swa_attention(proj_a, swa_sinks, i, S)

        qkvn, gb = dn_prep(proj_b, proj_small, conv_w, alog_vec, dtb_vec, i, S)
        nchunks = T // DN_CHUNK
        grow = gb[:, DN_HEADS:2 * DN_HEADS].reshape(nchunks, DN_CHUNK, DN_HEADS).transpose(0, 2, 1)
        y_b = dn_chunked(qkvn, proj_b, gb, grow, dn_norm3, i, B, S)

        qfull = mla_q(proj_small, mla_q_norm3, wq_packed, cos_tab, sin_tab, i)
        kfull, vfull = mla_kv(proj_small, mla_kv_norm3, wkv_packed, cos_tab, sin_tab, i)
        y_c = mla_flash(qfull, kfull, vfull, B, S)

        merged = merge_branches(y_a, y_b, y_c, gates, w_branch, i)
        j = i // 2
        if i % 2 == 0:
            xf, h2 = out_proj_residual(merged, w_out_bf, xf, norm_ffn3, i, True)
            y = ffn_swiglu(h2, wfg, wfu, wfd, j)
        else:
            (xf,) = out_proj_residual(merged, w_out_bf, xf, norm_ffn3, i, False)
            y = moe_layer(xf, norm_ffn3, w_router_pad, weg, weu, wed, i, j)

        if i + 1 < depth:
            xf, h = ple_layer(xf, y, pf, w_ple_gate_bf, w_ple_proj_bf, norm_ple3, norm_mix3, i, i + 1, BF16)
        else:
            xf, out = ple_layer(xf, y, pf, w_ple_gate_bf, w_ple_proj_bf, norm_ple3, final3, i, 0, F32)
    return out.reshape(B, S, D)
```

```python
import functools

import jax
import jax.numpy as jnp
import numpy as np
from jax import lax
from jax.experimental import pallas as pl
from jax.experimental.pallas import tpu as pltpu

BF16 = jnp.bfloat16
F32 = jnp.float32
NEG_INF = float("-inf")

NORM_EPS = 1e-6
SWA_Q_HEADS, SWA_KV_HEADS, SWA_HEAD_DIM, SWA_WINDOW = 16, 4, 64, 128
DN_HEADS, DN_HEAD_DIM, DN_CONV, DN_CHUNK = 8, 128, 4, 64
MLA_HEADS, MLA_NOPE, MLA_ROPE, MLA_V = 8, 128, 64, 128
ROPE_THETA = 10000.0
N_EXPERTS, TOP_K = 8, 2

LANES = 128
VMEM_LIMIT_BYTES = 56 * 1024 * 1024


def _cparams(*semantics):
    return pltpu.CompilerParams(dimension_semantics=semantics, vmem_limit_bytes=VMEM_LIMIT_BYTES)


def _tile(n, pref):
    t = min(n, pref)
    while n % t:
        t //= 2
    return t


def _dot(a, b):
    return jnp.dot(a, b, preferred_element_type=F32)


def _dot_nt(a, b):
    return lax.dot_general(a, b, (((1,), (1,)), ((), ())), preferred_element_type=F32)


def _dot_tn(a, b):
    return lax.dot_general(a, b, (((0,), (0,)), ((), ())), preferred_element_type=F32)


def _rms(x, gain):
    inv = lax.rsqrt(jnp.mean(x * x, axis=-1, keepdims=True) + NORM_EPS)
    return x * inv * gain


def _sigmoid(x):
    return 1.0 / (1.0 + jnp.exp(-x))


def _lane_repeat(x, n):
    return jnp.concatenate([x] * n, axis=1)


def _rmsnorm_kernel(x_ref, g_ref, o_ref):
    o_ref[...] = _rms(x_ref[...], g_ref[...]).astype(o_ref.dtype)


def rmsnorm_rows(x, gains, layer, out_dtype):
    T, D = x.shape
    tm = _tile(T, 512)
    return pl.pallas_call(
        _rmsnorm_kernel,
        grid=(T // tm,),
        in_specs=[pl.BlockSpec((tm, D), lambda i: (i, 0)),
                  pl.BlockSpec((None, 1, D), lambda i: (layer, 0, 0))],
        out_specs=pl.BlockSpec((tm, D), lambda i: (i, 0)),
        out_shape=jax.ShapeDtypeStruct((T, D), out_dtype),
        compiler_params=_cparams("parallel"),
        name="rmsnorm_rows",
    )(x, gains)


def _matmul_ws_kernel(x_ref, wt_ref, o_ref, wb_ref, *, act):
    @pl.when(pl.program_id(1) == 0)
    def _():
        wb_ref[...] = wt_ref[...].T.astype(BF16)

    acc = _dot(x_ref[...], wb_ref[...])
    if act == "sigmoid":
        acc = _sigmoid(acc)
    o_ref[...] = acc.astype(o_ref.dtype)


def matmul_ws(x, w_t, layer, *, n_cols, row_block_offset, tn, out_dtype, act=None, tm_pref=1024):
    M, K = x.shape
    tm = _tile(M, tm_pref)
    assert n_cols % tn == 0
    return pl.pallas_call(
        functools.partial(_matmul_ws_kernel, act=act),
        grid=(n_cols // tn, M // tm),
        in_specs=[pl.BlockSpec((tm, K), lambda j, i: (i, 0)),
                  pl.BlockSpec((None, tn, K), lambda j, i: (layer, j + row_block_offset, 0))],
        out_specs=pl.BlockSpec((tm, tn), lambda j, i: (i, j)),
        out_shape=jax.ShapeDtypeStruct((M, n_cols), out_dtype),
        scratch_shapes=[pltpu.VMEM((K, tn), BF16)],
        compiler_params=_cparams("arbitrary", "arbitrary"),
        name="matmul_ws",
    )(x, w_t)


def _swa_kernel(sinks_ref, q_ref, kc_ref, vc_ref, kp_ref, vp_ref, o_ref, *, layer, tiles_per_seq, rows):
    L = SWA_WINDOW
    dh = SWA_HEAD_DIM
    G = SWA_Q_HEADS // SWA_KV_HEADS
    first = (pl.program_id(0) % tiles_per_seq) == 0
    kall = jnp.concatenate([kp_ref[...], kc_ref[...]], axis=0)
    vall = jnp.concatenate([vp_ref[...], vc_ref[...]], axis=0)
    qi = lax.broadcasted_iota(jnp.int32, (L, 2 * L), 0)
    kj = lax.broadcasted_iota(jnp.int32, (L, 2 * L), 1)
    rel = qi + L - kj
    band = jnp.logical_and(rel >= 0, rel < SWA_WINDOW)
    kj_min = jnp.where(first, L, 0)
    scale = dh ** -0.5
    problems = [(b, h) for b in range(rows // L) for h in range(SWA_KV_HEADS)]
    scores = []
    for b, h in problems:
        qs = jnp.concatenate(
            [q_ref[b * L:(b + 1) * L, (h * G + g) * dh:(h * G + g + 1) * dh] for g in range(G)], axis=0)
        scores.append(_dot_nt(qs, kall[b * L:(b + 2) * L, h * dh:(h + 1) * dh]))
    probs, inv_denoms = [], []
    for (b, h), s in zip(problems, scores):
        valid = jnp.logical_and(band, kj >= kj_min) if b == 0 else band
        es, rs = [], []
        for g in range(G):
            sg = jnp.where(valid, s[g * L:(g + 1) * L] * scale, NEG_INF)
            sink = sinks_ref[layer, h * G + g]
            m = jnp.maximum(jnp.full((L, LANES), sink, F32), jnp.max(sg, axis=-1, keepdims=True))
            e = jnp.exp(sg - _lane_repeat(m, 2 * L // LANES))
            denom = jnp.sum(e, axis=-1, keepdims=True) + jnp.exp(sink - m[:, :1])
            es.append(e.astype(BF16))
            rs.append(1.0 / denom)
        probs.append(jnp.concatenate(es, axis=0))
        inv_denoms.append(rs)
    for (b, h), e, rs in zip(problems, probs, inv_denoms):
        o = _dot(e, vall[b * L:(b + 2) * L, h * dh:(h + 1) * dh])
        for g in range(G):
            hq = h * G + g
            o_ref[b * L:(b + 1) * L, hq * dh:(hq + 1) * dh] = (o[g * L:(g + 1) * L] * rs[g]).astype(o_ref.dtype)


def swa_attention(qkv, sinks, layer, seq_len):
    T = qkv.shape[0]
    L = SWA_WINDOW
    rows = _tile(seq_len, 512)
    wq = SWA_Q_HEADS * SWA_HEAD_DIM
    wkv = SWA_KV_HEADS * SWA_HEAD_DIM
    kcol, vcol = wq // wkv, wq // wkv + 1
    rpl = rows // L

    def prev_map(col):
        return lambda i: (jnp.maximum(i * rpl - 1, 0), col)

    return pl.pallas_call(
        functools.partial(_swa_kernel, layer=layer, tiles_per_seq=seq_len // rows, rows=rows),
        grid=(T // rows,),
        in_specs=[pl.BlockSpec(memory_space=pltpu.SMEM),
                  pl.BlockSpec((rows, wq), lambda i: (i, 0)),
                  pl.BlockSpec((rows, wkv), lambda i: (i, kcol)),
                  pl.BlockSpec((rows, wkv), lambda i: (i, vcol)),
                  pl.BlockSpec((L, wkv), prev_map(kcol)),
                  pl.BlockSpec((L, wkv), prev_map(vcol))],
        out_specs=pl.BlockSpec((rows, wq), lambda i: (i, 0)),
        out_shape=jax.ShapeDtypeStruct((T, wq), BF16),
        compiler_params=_cparams("parallel"),
        name="swa_attention",
    )(sinks, qkv, qkv, qkv, qkv, qkv)


def _dn_prep_kernel(cur_ref, prev_ref, cw_ref, t2_ref, alog_ref, dtb_ref, qkv_ref, gb_ref, *, tiles_per_seq, tm):
    H, dh, W = DN_HEADS, DN_HEAD_DIM, DN_CONV
    first = (pl.program_id(0) % tiles_per_seq) == 0
    prev = prev_ref[...].astype(F32)
    prev = jnp.where(first, 0.0, prev)
    xcat = jnp.concatenate([prev, cur_ref[...].astype(F32)], axis=0)
    P = prev.shape[0]
    cw = cw_ref[...]
    y = None
    for j in range(W):
        term = xcat[P - (W - 1) + j:P - (W - 1) + j + tm] * cw[j:j + 1]
        y = term if y is None else y + term
    y = y * _sigmoid(y)
    width = H * dh
    for h in range(H):
        qh = y[:, h * dh:(h + 1) * dh]
        kh = y[:, width + h * dh:width + (h + 1) * dh]
        qn = qh * lax.rsqrt(jnp.sum(qh * qh, axis=-1, keepdims=True) + 1e-6) * (dh ** -0.5)
        kn = kh * lax.rsqrt(jnp.sum(kh * kh, axis=-1, keepdims=True) + 1e-6)
        qkv_ref[:, h * dh:(h + 1) * dh] = qn.astype(qkv_ref.dtype)
        qkv_ref[:, width + h * dh:width + (h + 1) * dh] = kn.astype(qkv_ref.dtype)
    qkv_ref[:, 2 * width:] = y[:, 2 * width:].astype(qkv_ref.dtype)

    t2 = t2_ref[...]
    xs = t2 + dtb_ref[...]
    softplus = jnp.maximum(xs, 0.0) + jnp.log(1.0 + jnp.exp(-jnp.abs(xs)))
    g = -jnp.exp(alog_ref[...]) * softplus
    ri = lax.broadcasted_iota(jnp.int32, (tm, tm), 0)
    ci = lax.broadcasted_iota(jnp.int32, (tm, tm), 1)
    same_chunk = (ri // DN_CHUNK) == (ci // DN_CHUNK)
    tri = jnp.where(jnp.logical_and(same_chunk, ri >= ci), 1.0, 0.0)
    gc = jnp.dot(tri, g, preferred_element_type=F32, precision=lax.Precision.HIGHEST)
    lane = lax.broadcasted_iota(jnp.int32, t2.shape, 1)
    gb_ref[...] = jnp.where(lane < H, _sigmoid(t2), gc)


def dn_prep(proj_b, proj_small, conv_w, alog_vec, dtb_vec, layer, seq_len):
    T = proj_b.shape[0]
    width3 = 3 * DN_HEADS * DN_HEAD_DIM
    tm = _tile(seq_len, 256)
    P = 16
    tail2_col = proj_small.shape[1] // LANES - 1
    return pl.pallas_call(
        functools.partial(_dn_prep_kernel, tiles_per_seq=seq_len // tm, tm=tm),
        grid=(T // tm,),
        in_specs=[pl.BlockSpec((tm, width3), lambda i: (i, 0)),
                  pl.BlockSpec((P, width3), lambda i: (jnp.maximum(i * (tm // P) - 1, 0), 0)),
                  pl.BlockSpec((None, DN_CONV, width3), lambda i: (layer, 0, 0)),
                  pl.BlockSpec((tm, LANES), lambda i: (i, tail2_col)),
                  pl.BlockSpec((None, 1, LANES), lambda i: (layer, 0, 0)),
                  pl.BlockSpec((None, 1, LANES), lambda i: (layer, 0, 0))],
        out_specs=[pl.BlockSpec((tm, width3), lambda i: (i, 0)),
                   pl.BlockSpec((tm, LANES), lambda i: (i, 0))],
        out_shape=[jax.ShapeDtypeStruct((T, width3), BF16),
                   jax.ShapeDtypeStruct((T, LANES), F32)],
        compiler_params=_cparams("parallel"),
        name="dn_prep",
    )(proj_b, proj_b, conv_w, proj_small, alog_vec, dtb_vec)


def _unit_lower_inverse_many(ls):
    C = ls[0].shape[0]
    assert C == 64
    ri = lax.broadcasted_iota(jnp.int32, (C, C), 0)
    ci = lax.broadcasted_iota(jnp.int32, (C, C), 1)
    eye = jnp.where(ri == ci, 1.0, 0.0)

    def mm(a, b):
        return [_dot(x.astype(BF16), y.astype(BF16)) for x, y in zip(a, b)]

    l2 = mm(ls, ls)
    l4 = mm(l2, l2)
    l3 = mm(ls, l2)
    l8 = mm(l4, l4)
    x1 = [b - a - c for a, b, c in zip(ls, l2, l3)]
    l16 = mm(l8, l8)
    l12 = mm(l4, l8)
    x2 = [a + b + c for a, b, c in zip(l4, l8, l12)]
    l32 = mm(l16, l16)
    x12 = mm(x1, x2)
    y = [a + b + c for a, b, c in zip(x1, x2, x12)]
    l48 = mm(l16, l32)
    x3 = [a + b + c for a, b, c in zip(l16, l32, l48)]
    yx3 = mm(y, x3)
    return [eye + a + b + c for a, b, c in zip(y, x3, yx3)]


def _dn_chunk_kernel(qkv_ref, z_ref, gb_ref, grow_ref, norm_ref, o_ref,
                     state_ref, u_ref, w_ref, qe_ref, kd_ref, qk_ref, *, rows):
    H, dh, C = DN_HEADS, DN_HEAD_DIM, DN_CHUNK
    width = H * dh
    heads = range(H)

    @pl.when(pl.program_id(1) == 0)
    def _():
        state_ref[...] = jnp.zeros_like(state_ref)

    ri = lax.broadcasted_iota(jnp.int32, (C, C), 0)
    ci = lax.broadcasted_iota(jnp.int32, (C, C), 1)
    incl = ri >= ci
    strict = ri > ci
    gain = norm_ref[...]

    def prepare(c2, carry):
        r0s = [pl.multiple_of((par * c2 + j) * C, C) for j in range(par)]
        gbs = [gb_ref[pl.ds(r0, C), :] for r0 in r0s]
        grows = [grow_ref[par * c2 + j] for j in range(par)]
        probs = [(j, h) for j in range(par) for h in heads]
        q = [qkv_ref[pl.ds(r0s[j], C), h * dh:(h + 1) * dh] for j, h in probs]
        k = [qkv_ref[pl.ds(r0s[j], C), width + h * dh:width + (h + 1) * dh] for j, h in probs]
        v = [qkv_ref[pl.ds(r0s[j], C), 2 * width + h * dh:2 * width + (h + 1) * dh] for j, h in probs]
        beta = [gbs[j][:, h:h + 1] for j, h in probs]
        gcol = [gbs[j][:, H + h:H + h + 1] for j, h in probs]
        np_ = range(len(probs))
        decay = [jnp.exp(jnp.where(incl, gcol[p] - grows[j][h:h + 1, :], NEG_INF)) for p, (j, h) in enumerate(probs)]
        kf = [k[p].astype(F32) for p in np_]
        kbeta = [kf[p] * beta[p] for p in np_]
        eg = [jnp.exp(gcol[p]) for p in np_]
        kk = [_dot_nt(kbeta[p].astype(BF16), k[p]) for p in np_]
        qk = [_dot_nt(q[p], k[p]) for p in np_]
        tmat = _unit_lower_inverse_many([jnp.where(strict, kk[p] * decay[p], 0.0) for p in np_])
        tb = [t.astype(BF16) for t in tmat]
        u = [_dot(tb[p], (v[p].astype(F32) * beta[p]).astype(BF16)) for p in np_]
        w = [_dot(tb[p], (kbeta[p] * eg[p]).astype(BF16)) for p in np_]
        for p, (j, h) in enumerate(probs):
            r0 = r0s[j]
            g_last = gcol[p][C - 1:C, :]
            u_ref[h, pl.ds(r0, C), :] = u[p]
            w_ref[h, pl.ds(r0, C), :] = w[p].astype(BF16)
            qe_ref[h, pl.ds(r0, C), :] = (q[p].astype(F32) * eg[p]).astype(BF16)
            kd_ref[h, pl.ds(r0, C), :] = (kf[p] * jnp.exp(g_last - gcol[p])).astype(BF16)
            qk_ref[h, pl.ds(r0, C), :] = jnp.where(incl, qk[p] * decay[p], 0.0).astype(BF16)
        return carry

    par = 4 if (rows // C) % 4 == 0 else 1
    lax.fori_loop(0, rows // (C * par), prepare, 0)

    def scan(c, carry):
        r0 = pl.multiple_of(c * C, C)
        gb = gb_ref[pl.ds(r0, C), :]
        state = [state_ref[h] for h in heads]
        sb = [s.astype(BF16) for s in state]
        w_s = [_dot(w_ref[h, pl.ds(r0, C), :], sb[h]) for h in heads]
        q_s = [_dot(qe_ref[h, pl.ds(r0, C), :], sb[h]) for h in heads]
        vnb = [(u_ref[h, pl.ds(r0, C), :] - w_s[h]).astype(BF16) for h in heads]
        o_in = [_dot(qk_ref[h, pl.ds(r0, C), :], vnb[h]) for h in heads]
        kv = [_dot_tn(kd_ref[h, pl.ds(r0, C), :], vnb[h]) for h in heads]
        for h in heads:
            g_last = gb[C - 1:C, H + h:H + h + 1]
            state_ref[h] = state[h] * jnp.exp(g_last) + kv[h]
            z = z_ref[pl.ds(r0, C), h * dh:(h + 1) * dh].astype(F32)
            y = _rms(q_s[h] + o_in[h], gain) * (z * _sigmoid(z))
            o_ref[pl.ds(r0, C), h * dh:(h + 1) * dh] = y.astype(o_ref.dtype)
        return carry

    lax.fori_loop(0, rows // C, scan, 0, unroll=2)


def dn_chunked(qkvn, proj_b, gb, grow, dn_norm, layer, batch, seq_len):
    T = qkvn.shape[0]
    H, dh, C = DN_HEADS, DN_HEAD_DIM, DN_CHUNK
    width = H * dh
    rows = _tile(seq_len, 512)
    tps = seq_len // rows
    zcol = 3
    return pl.pallas_call(
        functools.partial(_dn_chunk_kernel, rows=rows),
        grid=(batch, tps),
        in_specs=[pl.BlockSpec((rows, 3 * width), lambda b, n: (b * tps + n, 0)),
                  pl.BlockSpec((rows, width), lambda b, n: (b * tps + n, zcol)),
                  pl.BlockSpec((rows, LANES), lambda b, n: (b * tps + n, 0)),
                  pl.BlockSpec((rows // C, H, C), lambda b, n: (b * tps + n, 0, 0)),
                  pl.BlockSpec((None, 1, dh), lambda b, n: (layer, 0, 0))],
        out_specs=pl.BlockSpec((rows, width), lambda b, n: (b * tps + n, 0)),
        out_shape=jax.ShapeDtypeStruct((T, width), BF16),
        scratch_shapes=[pltpu.VMEM((H, dh, dh), F32),
                        pltpu.VMEM((H, rows, dh), F32),
                        pltpu.VMEM((H, rows, dh), BF16),
                        pltpu.VMEM((H, rows, dh), BF16),
                        pltpu.VMEM((H, rows, dh), BF16),
                        pltpu.VMEM((H, rows, C), BF16)],
        compiler_params=_cparams("parallel", "arbitrary"),
        name="dn_chunked",
    )(qkvn, proj_b, gb, grow, dn_norm)


def _rotate_pairs(x, cos_tab, sin_tab):
    return x * cos_tab + pltpu.roll(x, LANES // 2, axis=1) * sin_tab


def _mla_q_kernel(c_ref, g_ref, w_ref, cos_ref, sin_ref, o_ref):
    H = MLA_HEADS
    scale = (MLA_NOPE + MLA_ROPE) ** -0.5 * float(np.log2(np.e))
    cn = _rms(c_ref[...], g_ref[...]).astype(BF16)
    acc = _dot(cn, w_ref[...]) * scale
    cos_tab, sin_tab = cos_ref[...], sin_ref[...]
    for h in range(H):
        o_ref[:, 2 * h * LANES:(2 * h + 1) * LANES] = acc[:, h * LANES:(h + 1) * LANES].astype(o_ref.dtype)
        xr = acc[:, (H + h) * LANES:(H + h + 1) * LANES]
        o_ref[:, (2 * h + 1) * LANES:(2 * h + 2) * LANES] = _rotate_pairs(xr, cos_tab, sin_tab).astype(o_ref.dtype)


def mla_q(proj_small, gains, wq, cos_tab, sin_tab, layer):
    T = proj_small.shape[0]
    R = wq.shape[1]
    tm = _tile(T, 512)
    N = wq.shape[2]
    return pl.pallas_call(
        _mla_q_kernel,
        grid=(T // tm,),
        in_specs=[pl.BlockSpec((tm, R), lambda i: (i, 0)),
                  pl.BlockSpec((None, 1, R), lambda i: (layer, 0, 0)),
                  pl.BlockSpec((None, R, N), lambda i: (layer, 0, 0)),
                  pl.BlockSpec((tm, LANES), lambda i: (i, 0)),
                  pl.BlockSpec((tm, LANES), lambda i: (i, 0))],
        out_specs=pl.BlockSpec((tm, N), lambda i: (i, 0)),
        out_shape=jax.ShapeDtypeStruct((T, N), BF16),
        compiler_params=_cparams("parallel"),
        name="mla_q",
    )(proj_small, gains, wq, cos_tab, sin_tab)


def _mla_kv_kernel(c_ref, kr_ref, g_ref, w_ref, cos_ref, sin_ref, k_ref, v_ref):
    H = MLA_HEADS
    cn = _rms(c_ref[...], g_ref[...]).astype(BF16)
    acc = _dot(cn, w_ref[...])
    kr = _rotate_pairs(kr_ref[...], cos_ref[...], sin_ref[...]).astype(k_ref.dtype)
    for h in range(H):
        k_ref[:, 2 * h * LANES:(2 * h + 1) * LANES] = acc[:, h * LANES:(h + 1) * LANES].astype(k_ref.dtype)
        k_ref[:, (2 * h + 1) * LANES:(2 * h + 2) * LANES] = kr
    v_ref[...] = acc[:, H * LANES:].astype(v_ref.dtype)


def mla_kv(proj_small, gains, wkv, cos_tab, sin_tab, layer):
    T = proj_small.shape[0]
    R = wkv.shape[1]
    tm = _tile(T, 512)
    N = wkv.shape[2]
    H = MLA_HEADS
    kr_col = (2 * R) // LANES
    return pl.pallas_call(
        _mla_kv_kernel,
        grid=(T // tm,),
        in_specs=[pl.BlockSpec((tm, R), lambda i: (i, 1)),
                  pl.BlockSpec((tm, LANES), lambda i: (i, kr_col)),
                  pl.BlockSpec((None, 1, R), lambda i: (layer, 0, 0)),
                  pl.BlockSpec((None, R, N), lambda i: (layer, 0, 0)),
                  pl.BlockSpec((tm, LANES), lambda i: (i, 0)),
                  pl.BlockSpec((tm, LANES), lambda i: (i, 0))],
        out_specs=[pl.BlockSpec((tm, 2 * H * LANES), lambda i: (i, 0)),
                   pl.BlockSpec((tm, H * MLA_V), lambda i: (i, 0))],
        out_shape=[jax.ShapeDtypeStruct((T, 2 * H * LANES), BF16),
                   jax.ShapeDtypeStruct((T, H * MLA_V), BF16)],
        compiler_params=_cparams("parallel"),
        name="mla_kv",
    )(proj_small, proj_small, gains, wkv, cos_tab, sin_tab)


def _flash_kernel(qi_ref, kj_ref, q_ref, k_ref, v_ref, o_ref, m_ref, l_ref, acc_ref, *, tq, tk):
    H, dv = MLA_HEADS, MLA_V
    dqk = 2 * LANES
    qi = qi_ref[pl.program_id(1)]
    kj = kj_ref[pl.program_id(1)]

    @pl.when(kj == 0)
    def _():
        m_ref[...] = jnp.full_like(m_ref, NEG_INF)
        l_ref[...] = jnp.zeros_like(l_ref)
        acc_ref[...] = jnp.zeros_like(acc_ref)

    def step(masked):
        if masked:
            row = lax.broadcasted_iota(jnp.int32, (tq, tk), 0)
            col = lax.broadcasted_iota(jnp.int32, (tq, tk), 1)
            keep = col <= row
        scores = [_dot_nt(q_ref[:, h * dqk:(h + 1) * dqk], k_ref[:, h * dqk:(h + 1) * dqk]) for h in range(H)]
        probs, alphas = [], []
        for h in range(H):
            s = scores[h]
            if masked:
                s = jnp.where(keep, s, NEG_INF)
            m_prev = m_ref[h]
            m_new = jnp.maximum(m_prev, jnp.max(s, axis=-1, keepdims=True))
            alpha = jnp.exp2(m_prev - m_new)
            p = jnp.exp2(s - _lane_repeat(m_new, tk // LANES))
            l_part = p[:, 0:LANES]
            for t in range(1, tk // LANES):
                l_part = l_part + p[:, t * LANES:(t + 1) * LANES]
            l_ref[h] = alpha * l_ref[h] + l_part
            m_ref[h] = m_new
            alphas.append(alpha)
            probs.append(p.astype(BF16))
        for h in range(H):
            acc_ref[h] = alphas[h] * acc_ref[h] + _dot(probs[h], v_ref[:, h * dv:(h + 1) * dv])

    @pl.when(kj < qi)
    def _():
        step(False)

    @pl.when(kj == qi)
    def _():
        step(True)
        for h in range(H):
            l = jnp.sum(l_ref[h], axis=-1, keepdims=True)
            o_ref[:, h * dv:(h + 1) * dv] = (acc_ref[h] / l).astype(o_ref.dtype)


def mla_flash(qf, kf, v, batch, seq_len):
    T = qf.shape[0]
    H, dv = MLA_HEADS, MLA_V
    t = _tile(seq_len, 512)
    nq = seq_len // t
    pairs = [(i, j) for i in range(nq) for j in range(i + 1)]
    qi_tab = jnp.asarray([i for i, _ in pairs], jnp.int32)
    kj_tab = jnp.asarray([j for _, j in pairs], jnp.int32)
    return pl.pallas_call(
        functools.partial(_flash_kernel, tq=t, tk=t),
        grid_spec=pltpu.PrefetchScalarGridSpec(
            num_scalar_prefetch=2,
            grid=(batch, len(pairs)),
            in_specs=[pl.BlockSpec((t, qf.shape[1]), lambda b, n, qi, kj: (b * nq + qi[n], 0)),
                      pl.BlockSpec((t, kf.shape[1]), lambda b, n, qi, kj: (b * nq + kj[n], 0)),
                      pl.BlockSpec((t, v.shape[1]), lambda b, n, qi, kj: (b * nq + kj[n], 0))],
            out_specs=pl.BlockSpec((t, H * dv), lambda b, n, qi, kj: (b * nq + qi[n], 0)),
            scratch_shapes=[pltpu.VMEM((H, t, LANES), F32),
                            pltpu.VMEM((H, t, LANES), F32),
                            pltpu.VMEM((H, t, dv), F32)]),
        out_shape=jax.ShapeDtypeStruct((T, H * dv), BF16),
        compiler_params=_cparams("parallel", "arbitrary"),
        name="mla_flash",
    )(qi_tab, kj_tab, qf, kf, v)


def _merge_kernel(ya_ref, yb_ref, yc_ref, ga_ref, gb_ref, gc_ref, wa_ref, wb_ref, wc_ref, o_ref, sa_ref, sb_ref, sc_ref):
    @pl.when(pl.program_id(1) == 0)
    def _():
        sa_ref[...] = wa_ref[...].astype(BF16)
        sb_ref[...] = wb_ref[...].astype(BF16)
        sc_ref[...] = wc_ref[...].astype(BF16)

    acc = ga_ref[...].astype(F32) * _dot(ya_ref[...], sa_ref[...])
    acc = acc + gb_ref[...].astype(F32) * _dot(yb_ref[...], sb_ref[...])
    acc = acc + gc_ref[...].astype(F32) * _dot(yc_ref[...], sc_ref[...])
    o_ref[...] = acc.astype(o_ref.dtype)


def merge_branches(ya, yb, yc, gates, w_branch, layer):
    T, Wb = ya.shape
    D = w_branch.shape[-1]
    tm = _tile(T, 1024)
    tn = _tile(D, 512)
    nj = D // tn
    y_spec = pl.BlockSpec((tm, Wb), lambda j, i: (i, 0))

    def g_spec(n):
        return pl.BlockSpec((tm, tn), lambda j, i: (i, n * nj + j))

    def w_spec(n):
        return pl.BlockSpec((None, None, Wb, tn), lambda j, i: (layer, n, 0, j))

    return pl.pallas_call(
        _merge_kernel,
        grid=(nj, T // tm),
        in_specs=[y_spec, y_spec, y_spec, g_spec(0), g_spec(1), g_spec(2), w_spec(0), w_spec(1), w_spec(2)],
        out_specs=pl.BlockSpec((tm, tn), lambda j, i: (i, j)),
        out_shape=jax.ShapeDtypeStruct((T, D), BF16),
        scratch_shapes=[pltpu.VMEM((Wb, tn), BF16)] * 3,
        compiler_params=_cparams("arbitrary", "arbitrary"),
        name="merge_branches",
    )(ya, yb, yc, gates, gates, gates, w_branch, w_branch, w_branch)


def _out_proj_kernel(m_ref, w_ref, x_ref, g_ref, xo_ref, *maybe_ho_ref):
    xn = x_ref[...] + _dot(m_ref[...], w_ref[...])
    xo_ref[...] = xn
    for ho_ref in maybe_ho_ref:
        ho_ref[...] = _rms(xn, g_ref[...]).astype(ho_ref.dtype)


def out_proj_residual(merged, w_out_bf, x, norm_gain, layer, emit_norm):
    T, D = x.shape
    tm = _tile(T, 512)
    row_spec = pl.BlockSpec((tm, D), lambda i: (i, 0))
    out_specs = [row_spec, row_spec] if emit_norm else [row_spec]
    out_shape = [jax.ShapeDtypeStruct((T, D), F32)] + ([jax.ShapeDtypeStruct((T, D), BF16)] if emit_norm else [])
    return pl.pallas_call(
        _out_proj_kernel,
        grid=(T // tm,),
        in_specs=[row_spec,
                  pl.BlockSpec((None, D, D), lambda i: (layer, 0, 0)),
                  row_spec,
                  pl.BlockSpec((None, 1, D), lambda i: (layer, 0, 0))],
        out_specs=out_specs,
        out_shape=out_shape,
        compiler_params=_cparams("parallel"),
        name="out_proj_residual",
    )(merged, w_out_bf, x, norm_gain)


def _ffn_kernel(h_ref, wg_ref, wu_ref, wd_ref, y_ref, acc_ref):
    f = pl.program_id(1)

    @pl.when(f == 0)
    def _():
        acc_ref[...] = jnp.zeros_like(acc_ref)

    h = h_ref[...]
    gate = _dot(h, wg_ref[...])
    up = _dot(h, wu_ref[...])
    act = (gate * _sigmoid(gate) * up).astype(BF16)
    acc_ref[...] += _dot(act, wd_ref[...])

    @pl.when(f == pl.num_programs(1) - 1)
    def _():
        y_ref[...] = acc_ref[...].astype(y_ref.dtype)


def ffn_swiglu(h, wg, wu, wd, widx):
    T, D = h.shape
    F = wg.shape[-1]
    tm = _tile(T, 512)
    tf = _tile(F, 1024)
    return pl.pallas_call(
        _ffn_kernel,
        grid=(T // tm, F // tf),
        in_specs=[pl.BlockSpec((tm, D), lambda i, f: (i, 0)),
                  pl.BlockSpec((None, D, tf), lambda i, f: (widx, 0, f)),
                  pl.BlockSpec((None, D, tf), lambda i, f: (widx, 0, f)),
                  pl.BlockSpec((None, tf, D), lambda i, f: (widx, f, 0))],
        out_specs=pl.BlockSpec((tm, D), lambda i, f: (i, 0)),
        out_shape=jax.ShapeDtypeStruct((T, D), BF16),
        scratch_shapes=[pltpu.VMEM((tm, D), F32)],
        compiler_params=_cparams("parallel", "arbitrary"),
        name="ffn_swiglu",
    )(h, wg, wu, wd)


def _router_kernel(x_ref, g_ref, w_ref, o_ref):
    h = _rms(x_ref[...], g_ref[...])
    logits = jnp.dot(h, w_ref[...], preferred_element_type=F32, precision=lax.Precision.HIGHEST)
    lane = lax.broadcasted_iota(jnp.int32, logits.shape, 1)
    lanef = lane.astype(F32)
    big = float(LANES)
    logits = jnp.where(lane < N_EXPERTS, logits, NEG_INF)
    l1 = jnp.max(logits, axis=-1, keepdims=True)
    i1 = jnp.min(jnp.where(logits == l1, lanef, big), axis=-1, keepdims=True)
    rest = jnp.where(lanef == i1, NEG_INF, logits)
    l2 = jnp.max(rest, axis=-1, keepdims=True)
    i2 = jnp.min(jnp.where(rest == l2, lanef, big), axis=-1, keepdims=True)
    e2 = jnp.exp(l2 - l1)
    w1 = 1.0 / (1.0 + e2)
    w2 = e2 / (1.0 + e2)
    out = jnp.where(lane == 0, i1, jnp.where(lane == 1, i2, jnp.where(lane == 2, w1, jnp.where(lane == 3, w2, 0.0))))
    o_ref[...] = out


def moe_router(x, norm_gain, w_router_pad, layer, widx):
    T, D = x.shape
    tm = _tile(T, 512)
    return pl.pallas_call(
        _router_kernel,
        grid=(T // tm,),
        in_specs=[pl.BlockSpec((tm, D), lambda i: (i, 0)),
                  pl.BlockSpec((None, 1, D), lambda i: (layer, 0, 0)),
                  pl.BlockSpec((None, D, LANES), lambda i: (widx, 0, 0))],
        out_specs=pl.BlockSpec((tm, LANES), lambda i: (i, 0)),
        out_shape=jax.ShapeDtypeStruct((T, LANES), F32),
        compiler_params=_cparams("parallel"),
        name="moe_router",
    )(x, norm_gain, w_router_pad)


def _start_row_gather(idx_hbm, src_hbm, blk, slot, idx_smem, rows_vmem, idx_sem, row_sem, n):
    idx_copy = pltpu.make_async_copy(idx_hbm.at[blk], idx_smem.at[slot], idx_sem.at[slot])
    idx_copy.start()
    idx_copy.wait()

    def issue(i, c):
        for priority in range(2):
            r = 2 * i + priority
            pltpu.make_async_copy(src_hbm.at[pl.ds(idx_smem[slot, r], 1)],
                                  rows_vmem.at[slot, pl.ds(r, 1)],
                                  row_sem.at[slot]).start(priority=priority)
        return c

    lax.fori_loop(0, n // 2, issue, 0, unroll=4)


def _wait_row_gather(src_hbm, slot, rows_vmem, row_sem, n):
    pltpu.make_async_copy(src_hbm.at[pl.ds(0, n)], rows_vmem.at[slot], row_sem.at[slot]).wait()


def _gather_rows_kernel(meta_ref, idx_hbm, x_hbm, g_ref, o_ref, idx_smem, rows_vmem, idx_sem, row_sem, *, bm):
    m = pl.program_id(0)
    n_used = meta_ref[pl.num_programs(0)]
    slot = m % 2
    start = functools.partial(_start_row_gather, idx_hbm, x_hbm, idx_smem=idx_smem, rows_vmem=rows_vmem,
                              idx_sem=idx_sem, row_sem=row_sem, n=bm)

    @pl.when(m == 0)
    def _():
        start(blk=0, slot=0)

    @pl.when(m + 1 < n_used)
    def _():
        start(blk=m + 1, slot=1 - slot)

    @pl.when(m < n_used)
    def _():
        _wait_row_gather(x_hbm, slot, rows_vmem, row_sem, bm)
        o_ref[...] = _rms(rows_vmem[slot], g_ref[...]).astype(o_ref.dtype)

    @pl.when(m >= n_used)
    def _():
        o_ref[...] = jnp.zeros_like(o_ref)


def moe_gather_norm(x, row_tok, meta, norm_gain, layer, bm):
    T, D = x.shape
    n_blocks = row_tok.shape[0]
    return pl.pallas_call(
        functools.partial(_gather_rows_kernel, bm=bm),
        grid_spec=pltpu.PrefetchScalarGridSpec(
            num_scalar_prefetch=1,
            grid=(n_blocks,),
            in_specs=[pl.BlockSpec(memory_space=pl.ANY),
                      pl.BlockSpec(memory_space=pl.ANY),
                      pl.BlockSpec((None, 1, D), lambda i, meta: (layer, 0, 0))],
            out_specs=pl.BlockSpec((bm, D), lambda i, meta: (i, 0)),
            scratch_shapes=[pltpu.SMEM((2, bm), jnp.int32),
                            pltpu.VMEM((2, bm, D), F32),
                            pltpu.SemaphoreType.DMA((2,)),
                            pltpu.SemaphoreType.DMA((2,))]),
        out_shape=jax.ShapeDtypeStruct((n_blocks * bm, D), BF16),
        compiler_params=_cparams("arbitrary"),
        name="moe_gather_norm",
    )(meta, row_tok, x, norm_gain)


def _expert_up_kernel(meta_ref, x_ref, wg_ref, wu_ref, o_ref, wgb_ref, wub_ref):
    m = pl.program_id(1)
    n_used = meta_ref[pl.num_programs(1)]
    new_expert = jnp.logical_or(m == 0, meta_ref[m] != meta_ref[jnp.maximum(m - 1, 0)])

    @pl.when(jnp.logical_and(new_expert, m < n_used))
    def _():
        wgb_ref[...] = wg_ref[...].astype(BF16)
        wub_ref[...] = wu_ref[...].astype(BF16)

    @pl.when(m < n_used)
    def _():
        x = x_ref[...]
        gate = _dot(x, wgb_ref[...])
        up = _dot(x, wub_ref[...])
        o_ref[...] = (gate * _sigmoid(gate) * up).astype(o_ref.dtype)

    @pl.when(m >= n_used)
    def _():
        o_ref[...] = jnp.zeros_like(o_ref)


def _last_used(m, meta, nb):
    return jnp.minimum(m, meta[nb] - 1)


def moe_expert_up(xs, meta, wg, wu, widx, bm):
    R, D = xs.shape
    F = wg.shape[-1]
    tf = _tile(F, 1024)
    nb = R // bm

    def w_map(f, m, meta):
        return (widx, meta[_last_used(m, meta, nb)], 0, f)

    return pl.pallas_call(
        _expert_up_kernel,
        grid_spec=pltpu.PrefetchScalarGridSpec(
            num_scalar_prefetch=1,
            grid=(F // tf, nb),
            in_specs=[pl.BlockSpec((bm, D), lambda f, m, meta: (_last_used(m, meta, nb), 0)),
                      pl.BlockSpec((None, None, D, tf), w_map),
                      pl.BlockSpec((None, None, D, tf), w_map)],
            out_specs=pl.BlockSpec((bm, tf), lambda f, m, meta: (m, f)),
            scratch_shapes=[pltpu.VMEM((D, tf), BF16), pltpu.VMEM((D, tf), BF16)]),
        out_shape=jax.ShapeDtypeStruct((R, F), BF16),
        compiler_params=_cparams("arbitrary", "arbitrary"),
        name="moe_expert_up",
    )(meta, xs, wg, wu)


def _expert_down_kernel(meta_ref, a_ref, wd_ref, o_ref, wdb_ref):
    m = pl.program_id(1)
    n_used = meta_ref[pl.num_programs(1)]
    new_expert = jnp.logical_or(m == 0, meta_ref[m] != meta_ref[jnp.maximum(m - 1, 0)])

    @pl.when(jnp.logical_and(new_expert, m < n_used))
    def _():
        wdb_ref[...] = wd_ref[...].astype(BF16)

    @pl.when(m < n_used)
    def _():
        o_ref[...] = _dot(a_ref[...], wdb_ref[...]).astype(o_ref.dtype)

    @pl.when(m >= n_used)
    def _():
        o_ref[...] = jnp.zeros_like(o_ref)


def moe_expert_down(act, meta, wd, widx, bm):
    R, F = act.shape
    D = wd.shape[-1]
    tn = _tile(D, 512)
    nb = R // bm
    return pl.pallas_call(
        _expert_down_kernel,
        grid_spec=pltpu.PrefetchScalarGridSpec(
            num_scalar_prefetch=1,
            grid=(D // tn, nb),
            in_specs=[pl.BlockSpec((bm, F), lambda j, m, meta: (_last_used(m, meta, nb), 0)),
                      pl.BlockSpec((None, None, F, tn),
                                   lambda j, m, meta: (widx, meta[_last_used(m, meta, nb)], 0, j))],
            out_specs=pl.BlockSpec((bm, tn), lambda j, m, meta: (m, j)),
            scratch_shapes=[pltpu.VMEM((F, tn), BF16)]),
        out_shape=jax.ShapeDtypeStruct((R, D), F32),
        compiler_params=_cparams("arbitrary", "arbitrary"),
        name="moe_expert_down",
    )(meta, act, wd)


def _combine_kernel(dest_hbm, yb_hbm, rt_ref, y_ref, idx_smem, rows_vmem, idx_sem, row_sem, *, tm):
    m = pl.program_id(0)
    slot = m % 2
    n = TOP_K * tm
    start = functools.partial(_start_row_gather, dest_hbm, yb_hbm, idx_smem=idx_smem, rows_vmem=rows_vmem,
                              idx_sem=idx_sem, row_sem=row_sem, n=n)

    @pl.when(m == 0)
    def _():
        start(blk=0, slot=0)

    @pl.when(m + 1 < pl.num_programs(0))
    def _():
        start(blk=m + 1, slot=1 - slot)

    _wait_row_gather(yb_hbm, slot, rows_vmem, row_sem, n)
    rt = rt_ref[...]
    y = rt[:, 2:3] * rows_vmem[slot, 0:tm, :] + rt[:, 3:4] * rows_vmem[slot, tm:2 * tm, :]
    y_ref[...] = y.astype(y_ref.dtype)


def moe_combine(yb, dest_blocks, route, tm):
    T = route.shape[0]
    D = yb.shape[1]
    return pl.pallas_call(
        functools.partial(_combine_kernel, tm=tm),
        grid=(T // tm,),
        in_specs=[pl.BlockSpec(memory_space=pl.ANY),
                  pl.BlockSpec(memory_space=pl.ANY),
                  pl.BlockSpec((tm, LANES), lambda i: (i, 0))],
        out_specs=pl.BlockSpec((tm, D), lambda i: (i, 0)),
        out_shape=jax.ShapeDtypeStruct((T, D), BF16),
        scratch_shapes=[pltpu.SMEM((2, TOP_K * tm), jnp.int32),
                        pltpu.VMEM((2, TOP_K * tm, D), F32),
                        pltpu.SemaphoreType.DMA((2,)),
                        pltpu.SemaphoreType.DMA((2,))],
        compiler_params=_cparams("arbitrary"),
        name="moe_combine",
    )(dest_blocks, yb, route)


def moe_layer(x, norm_ffn, w_router_pad, wg, wu, wd, layer, widx):
    T, D = x.shape
    E = N_EXPERTS
    bm = _tile(T, 512)
    route = moe_router(x, norm_ffn, w_router_pad, layer, widx)
    flat_e = route[:, :TOP_K].astype(jnp.int32).reshape(-1)
    onehot = (flat_e[:, None] == jnp.arange(E, dtype=jnp.int32)[None, :]).astype(jnp.int32)
    incl = jnp.cumsum(onehot, axis=0)
    counts = incl[-1]
    rank = jnp.sum((incl - onehot) * onehot, axis=1)
    padded = (counts + bm - 1) // bm * bm
    pad_end = jnp.cumsum(padded)
    pad_start = pad_end - padded
    dest = pad_start[flat_e] + rank
    n_rows = -(-(T * TOP_K + E * (bm - 1)) // bm) * bm
    n_blocks = n_rows // bm
    flat_tok = jnp.repeat(jnp.arange(T, dtype=jnp.int32), TOP_K)
    row_tok = jnp.zeros((n_rows,), jnp.int32).at[dest].set(flat_tok)
    block_e = jnp.minimum(
        jnp.sum(jnp.arange(n_blocks, dtype=jnp.int32)[:, None] * bm >= pad_end[None, :], axis=1), E - 1
    ).astype(jnp.int32)
    meta = jnp.concatenate([block_e, (pad_end[-1:] // bm).astype(jnp.int32)])

    xs = moe_gather_norm(x, row_tok.reshape(n_blocks, bm), meta, norm_ffn, layer, bm)
    act = moe_expert_up(xs, meta, wg, wu, widx, bm)
    yb = moe_expert_down(act, meta, wd, widx, bm)
    tmc = _tile(T, 256)
    dest_blocks = dest.reshape(T // tmc, tmc, TOP_K).transpose(0, 2, 1).reshape(T // tmc, TOP_K * tmc)
    return moe_combine(yb, dest_blocks, route, tmc)


def _ple_kernel(x_ref, y_ref, p_ref, wg_ref, wp_ref, gp_ref, gn_ref, xo_ref, ho_ref):
    x = x_ref[...] + y_ref[...].astype(F32)
    hp = _rms(x, gp_ref[...]).astype(BF16)
    gate = _sigmoid(_dot(hp, wg_ref[...]))
    emb = _dot(p_ref[...].astype(BF16), wp_ref[...])
    xn = x + emb * gate
    xo_ref[...] = xn
    ho_ref[...] = _rms(xn, gn_ref[...]).astype(ho_ref.dtype)


def ple_layer(x, y, p, w_gate_bf, w_proj_bf, ple_gain, next_gain, layer, gain_idx, out_dtype):
    T, D = x.shape
    Pd = p.shape[-1]
    tm = _tile(T, 512)
    row_spec = pl.BlockSpec((tm, D), lambda i: (i, 0))
    return pl.pallas_call(
        _ple_kernel,
        grid=(T // tm,),
        in_specs=[row_spec,
                  row_spec,
                  pl.BlockSpec((None, tm, Pd), lambda i: (layer, i, 0)),
                  pl.BlockSpec((None, D, D), lambda i: (layer, 0, 0)),
                  pl.BlockSpec((None, Pd, D), lambda i: (layer, 0, 0)),
                  pl.BlockSpec((None, 1, D), lambda i: (layer, 0, 0)),
                  pl.BlockSpec((None, 1, D), lambda i: (gain_idx, 0, 0))],
        out_specs=[row_spec, row_spec],
        out_shape=[jax.ShapeDtypeStruct((T, D), F32), jax.ShapeDtypeStruct((T, D), out_dtype)],
        compiler_params=_cparams("parallel"),
        name="ple_layer",
    )(x, y, p, w_gate_bf, w_proj_bf, ple_gain, next_gain)


def _split_points(D):
    swa_q = SWA_Q_HEADS * SWA_HEAD_DIM
    swa_kv = SWA_KV_HEADS * SWA_HEAD_DIM
    dn_w = DN_HEADS * DN_HEAD_DIM
    return swa_q + 2 * swa_kv, swa_q + 2 * swa_kv + 4 * dn_w


def _pack_small_in_proj(w_in_t, q_lora, kv_lora):
    _, b_end = _split_points(w_in_t.shape[2])
    H = DN_HEADS
    o = b_end
    beta = w_in_t[:, o:o + H]
    decay = w_in_t[:, o + H:o + 2 * H]
    o += 2 * H
    cq = w_in_t[:, o:o + q_lora]
    o += q_lora
    ckv = w_in_t[:, o:o + kv_lora]
    o += kv_lora
    half = MLA_ROPE // 2
    kr1 = w_in_t[:, o:o + half]
    kr2 = w_in_t[:, o + half:o + 2 * half]
    o += MLA_ROPE
    pad = jnp.zeros((w_in_t.shape[0], LANES - 2 * H, w_in_t.shape[2]), w_in_t.dtype)
    packed = jnp.concatenate([cq, ckv, kr1, kr2, kr2, kr1, beta, decay, pad], axis=1)
    return packed, o


def _pack_w_uq(w_uq):
    Ld, R, _ = w_uq.shape
    H, half = MLA_HEADS, MLA_ROPE // 2
    w = w_uq.reshape(Ld, R, H, MLA_NOPE + MLA_ROPE)
    nope = w[..., :MLA_NOPE].reshape(Ld, R, H * MLA_NOPE)
    r1 = w[..., MLA_NOPE:MLA_NOPE + half]
    r2 = w[..., MLA_NOPE + half:]
    rope = jnp.concatenate([r1, r2, r2, r1], axis=-1).reshape(Ld, R, H * LANES)
    return jnp.concatenate([nope, rope], axis=-1).astype(BF16)


def _pack_w_ukv(w_ukv):
    Ld, R, _ = w_ukv.shape
    H = MLA_HEADS
    w = w_ukv.reshape(Ld, R, H, MLA_NOPE + MLA_V)
    kn = w[..., :MLA_NOPE].reshape(Ld, R, H * MLA_NOPE)
    vv = w[..., MLA_NOPE:].reshape(Ld, R, H * MLA_V)
    return jnp.concatenate([kn, vv], axis=-1).astype(BF16)


def _rope_tables(positions):
    half = MLA_ROPE // 2
    inv_freq = ROPE_THETA ** (-jnp.arange(half, dtype=F32) / half)
    ang = positions.astype(F32).reshape(-1)[:, None] * inv_freq
    cos, sin = jnp.cos(ang), jnp.sin(ang)
    zeros = jnp.zeros_like(cos)
    return (jnp.concatenate([cos, cos, zeros, zeros], axis=-1),
            jnp.concatenate([-sin, sin, zeros, zeros], axis=-1))


def _lane_vec(v, offset):
    Ld, n = v.shape
    out = jnp.zeros((Ld, 1, LANES), F32)
    return out.at[:, 0, offset:offset + n].set(v.astype(F32))


def kernel(x, p, positions, norm_mix, w_in, conv_w, dn_a_log, dn_dt_bias, dn_norm, swa_sinks, mla_q_norm, w_uq,
           mla_kv_norm, w_ukv, w_branch, w_out, norm_ffn, w_ffn_gate, w_ffn_up, w_ffn_down, w_router, w_exp_gate,
           w_exp_up, w_exp_down, norm_ple, w_ple_gate, w_ple_proj, final_norm):
    B, S, D = x.shape
    T = B * S
    depth = w_in.shape[0]
    q_lora, kv_lora = w_uq.shape[1], w_ukv.shape[1]
    a_end, b_end = _split_points(D)

    w_in_t = jnp.swapaxes(w_in, 1, 2)
    w_small_t, gate_row0 = _pack_small_in_proj(w_in_t, q_lora, kv_lora)
    w_gates_t = w_in_t[:, gate_row0:]
    wq_packed = _pack_w_uq(w_uq)
    wkv_packed = _pack_w_ukv(w_ukv)
    w_out_bf = w_out.astype(BF16)
    w_ple_gate_bf = w_ple_gate.astype(BF16)
    w_ple_proj_bf = w_ple_proj.astype(BF16)
    wfg, wfu, wfd = w_ffn_gate.astype(BF16), w_ffn_up.astype(BF16), w_ffn_down.astype(BF16)
    weg, weu, wed = w_exp_gate, w_exp_up, w_exp_down
    w_router_pad = jnp.pad(w_router, ((0, 0), (0, 0), (0, LANES - w_router.shape[-1])))
    cos_tab, sin_tab = _rope_tables(positions)
    alog_vec = _lane_vec(dn_a_log, DN_HEADS)
    dtb_vec = _lane_vec(dn_dt_bias, DN_HEADS)
    row = lambda g: g.reshape(g.shape[0], 1, g.shape[-1])
    norm_mix3, norm_ffn3, norm_ple3 = row(norm_mix), row(norm_ffn), row(norm_ple)
    dn_norm3, mla_q_norm3, mla_kv_norm3 = row(dn_norm), row(mla_q_norm), row(mla_kv_norm)
    final3 = final_norm.reshape(1, 1, D)

    xf = x.reshape(T, D)
    pf = p.reshape(depth, T, p.shape[-1])
    h = rmsnorm_rows(xf, norm_mix3, 0, BF16)
    for i in range(depth):
        proj_a = matmul_ws(h, w_in_t, i, n_cols=a_end, row_block_offset=0, tn=a_end // 2, out_dtype=BF16,
                           tm_pref=2048)
        tn_b = 512
        proj_b = matmul_ws(h, w_in_t, i, n_cols=b_end - a_end, row_block_offset=a_end // tn_b, tn=tn_b,
                           out_dtype=BF16, tm_pref=2048)
        proj_small = matmul_ws(h, w_small_t, i, n_cols=w_small_t.shape[1], row_block_offset=0,
                               tn=w_small_t.shape[1], out_dtype=F32)
        gates = matmul_ws(h, w_gates_t, i, n_cols=w_gates_t.shape[1], row_block_offset=0, tn=1024,
                          out_dtype=BF16, act="sigmoid")

        y_a = swa_attention(proj_a, swa_sinks, i, S)

        qkvn, gb = dn_prep(proj_b, proj_small, conv_w, alog_vec, dtb_vec, i, S)
        nchunks = T // DN_CHUNK
        grow = gb[:, DN_HEADS:2 * DN_HEADS].reshape(nchunks, DN_CHUNK, DN_HEADS).transpose(0, 2, 1)
        y_b = dn_chunked(qkvn, proj_b, gb, grow, dn_norm3, i, B, S)

        qfull = mla_q(proj_small, mla_q_norm3, wq_packed, cos_tab, sin_tab, i)
        kfull, vfull = mla_kv(proj_small, mla_kv_norm3, wkv_packed, cos_tab, sin_tab, i)
        y_c = mla_flash(qfull, kfull, vfull, B, S)

        merged = merge_branches(y_a, y_b, y_c, gates, w_branch, i)
        j = i // 2
        if i % 2 == 0:
            xf, h2 = out_proj_residual(merged, w_out_bf, xf, norm_ffn3, i, True)
            y = ffn_swiglu(h2, wfg, wfu, wfd, j)
        else:
            (xf,) = out_proj_residual(merged, w_out_bf, xf, norm_ffn3, i, False)
            y = moe_layer(xf, norm_ffn3, w_router_pad, weg, weu, wed, i, j)

        if i + 1 < depth:
            xf, h = ple_layer(xf, y, pf, w_ple_gate_bf, w_ple_proj_bf, norm_ple3, norm_mix3, i, i + 1, BF16)
        else:
            xf, out = ple_layer(xf, y, pf, w_ple_gate_bf, w_ple_proj_bf, norm_ple3, final3, i, 0, F32)
    return out.reshape(B, S, D)
```

```python
import functools

import jax
import jax.numpy as jnp
import numpy as np
from jax import lax
from jax.experimental import pallas as pl
from jax.experimental.pallas import tpu as pltpu

BF16 = jnp.bfloat16
F32 = jnp.float32
NEG_INF = float("-inf")

NORM_EPS = 1e-6
SWA_Q_HEADS, SWA_KV_HEADS, SWA_HEAD_DIM, SWA_WINDOW = 16, 4, 64, 128
DN_HEADS, DN_HEAD_DIM, DN_CONV, DN_CHUNK = 8, 128, 4, 64
MLA_HEADS, MLA_NOPE, MLA_ROPE, MLA_V = 8, 128, 64, 128
ROPE_THETA = 10000.0
N_EXPERTS, TOP_K = 8, 2

LANES = 128
VMEM_LIMIT_BYTES = 56 * 1024 * 1024


def _cparams(*semantics):
    return pltpu.CompilerParams(dimension_semantics=semantics, vmem_limit_bytes=VMEM_LIMIT_BYTES)


def _tile(n, pref):
    t = min(n, pref)
    while n % t:
        t //= 2
    return t


def _dot(a, b):
    return jnp.dot(a, b, preferred_element_type=F32)


def _dot_nt(a, b):
    return lax.dot_general(a, b, (((1,), (1,)), ((), ())), preferred_element_type=F32)


def _dot_tn(a, b):
    return lax.dot_general(a, b, (((0,), (0,)), ((), ())), preferred_element_type=F32)


def _rms(x, gain):
    inv = lax.rsqrt(jnp.mean(x * x, axis=-1, keepdims=True) + NORM_EPS)
    return x * inv * gain


def _sigmoid(x):
    return 1.0 / (1.0 + jnp.exp(-x))


def _lane_repeat(x, n):
    return jnp.concatenate([x] * n, axis=1)


def _rmsnorm_kernel(x_ref, g_ref, o_ref):
    o_ref[...] = _rms(x_ref[...], g_ref[...]).astype(o_ref.dtype)


def rmsnorm_rows(x, gains, layer, out_dtype):
    T, D = x.shape
    tm = _tile(T, 512)
    return pl.pallas_call(
        _rmsnorm_kernel,
        grid=(T // tm,),
        in_specs=[pl.BlockSpec((tm, D), lambda i: (i, 0)),
                  pl.BlockSpec((None, 1, D), lambda i: (layer, 0, 0))],
        out_specs=pl.BlockSpec((tm, D), lambda i: (i, 0)),
        out_shape=jax.ShapeDtypeStruct((T, D), out_dtype),
        compiler_params=_cparams("parallel"),
        name="rmsnorm_rows",
    )(x, gains)


def _matmul_ws_kernel(x_ref, wt_ref, o_ref, wb_ref, *, act):
    @pl.when(pl.program_id(1) == 0)
    def _():
        wb_ref[...] = wt_ref[...].T.astype(BF16)

    acc = _dot(x_ref[...], wb_ref[...])
    if act == "sigmoid":
        acc = _sigmoid(acc)
    o_ref[...] = acc.astype(o_ref.dtype)


def matmul_ws(x, w_t, layer, *, n_cols, row_block_offset, tn, out_dtype, act=None, tm_pref=1024):
    M, K = x.shape
    tm = _tile(M, tm_pref)
    assert n_cols % tn == 0
    return pl.pallas_call(
        functools.partial(_matmul_ws_kernel, act=act),
        grid=(n_cols // tn, M // tm),
        in_specs=[pl.BlockSpec((tm, K), lambda j, i: (i, 0)),
                  pl.BlockSpec((None, tn, K), lambda j, i: (layer, j + row_block_offset, 0))],
        out_specs=pl.BlockSpec((tm, tn), lambda j, i: (i, j)),
        out_shape=jax.ShapeDtypeStruct((M, n_cols), out_dtype),
        scratch_shapes=[pltpu.VMEM((K, tn), BF16)],
        compiler_params=_cparams("arbitrary", "arbitrary"),
        name="matmul_ws",
    )(x, w_t)


def _swa_kernel(sinks_ref, q_ref, kc_ref, vc_ref, kp_ref, vp_ref, o_ref, *, layer, tiles_per_seq, rows):
    L = SWA_WINDOW
    dh = SWA_HEAD_DIM
    G = SWA_Q_HEADS // SWA_KV_HEADS
    first = (pl.program_id(0) % tiles_per_seq) == 0
    kall = jnp.concatenate([kp_ref[...], kc_ref[...]], axis=0)
    vall = jnp.concatenate([vp_ref[...], vc_ref[...]], axis=0)
    qi = lax.broadcasted_iota(jnp.int32, (L, 2 * L), 0)
    kj = lax.broadcasted_iota(jnp.int32, (L, 2 * L), 1)
    rel = qi + L - kj
    band = jnp.logical_and(rel >= 0, rel < SWA_WINDOW)
    kj_min = jnp.where(first, L, 0)
    scale = dh ** -0.5
    problems = [(b, h) for b in range(rows // L) for h in range(SWA_KV_HEADS)]
    scores = []
    for b, h in problems:
        qs = jnp.concatenate(
            [q_ref[b * L:(b + 1) * L, (h * G + g) * dh:(h * G + g + 1) * dh] for g in range(G)], axis=0)
        scores.append(_dot_nt(qs, kall[b * L:(b + 2) * L, h * dh:(h + 1) * dh]))
    probs, inv_denoms = [], []
    for (b, h), s in zip(problems, scores):
        valid = jnp.logical_and(band, kj >= kj_min) if b == 0 else band
        es, rs = [], []
        for g in range(G):
            sg = jnp.where(valid, s[g * L:(g + 1) * L] * scale, NEG_INF)
            sink = sinks_ref[layer, h * G + g]
            m = jnp.maximum(jnp.full((L, LANES), sink, F32), jnp.max(sg, axis=-1, keepdims=True))
            e = jnp.exp(sg - _lane_repeat(m, 2 * L // LANES))
            denom = jnp.sum(e, axis=-1, keepdims=True) + jnp.exp(sink - m[:, :1])
            es.append(e.astype(BF16))
            rs.append(1.0 / denom)
        probs.append(jnp.concatenate(es, axis=0))
        inv_denoms.append(rs)
    for (b, h), e, rs in zip(problems, probs, inv_denoms):
        o = _dot(e, vall[b * L:(b + 2) * L, h * dh:(h + 1) * dh])
        for g in range(G):
            hq = h * G + g
            o_ref[b * L:(b + 1) * L, hq * dh:(hq + 1) * dh] = (o[g * L:(g + 1) * L] * rs[g]).astype(o_ref.dtype)


def swa_attention(qkv, sinks, layer, seq_len):
    T = qkv.shape[0]
    L = SWA_WINDOW
    rows = _tile(seq_len, 512)
    wq = SWA_Q_HEADS * SWA_HEAD_DIM
    wkv = SWA_KV_HEADS * SWA_HEAD_DIM
    kcol, vcol = wq // wkv, wq // wkv + 1
    rpl = rows // L

    def prev_map(col):
        return lambda i: (jnp.maximum(i * rpl - 1, 0), col)

    return pl.pallas_call(
        functools.partial(_swa_kernel, layer=layer, tiles_per_seq=seq_len // rows, rows=rows),
        grid=(T // rows,),
        in_specs=[pl.BlockSpec(memory_space=pltpu.SMEM),
                  pl.BlockSpec((rows, wq), lambda i: (i, 0)),
                  pl.BlockSpec((rows, wkv), lambda i: (i, kcol)),
                  pl.BlockSpec((rows, wkv), lambda i: (i, vcol)),
                  pl.BlockSpec((L, wkv), prev_map(kcol)),
                  pl.BlockSpec((L, wkv), prev_map(vcol))],
        out_specs=pl.BlockSpec((rows, wq), lambda i: (i, 0)),
        out_shape=jax.ShapeDtypeStruct((T, wq), BF16),
        compiler_params=_cparams("parallel"),
        name="swa_attention",
    )(sinks, qkv, qkv, qkv, qkv, qkv)


def _dn_prep_kernel(cur_ref, prev_ref, cw_ref, t2_ref, alog_ref, dtb_ref, qkv_ref, gb_ref, *, tiles_per_seq, tm):
    H, dh, W = DN_HEADS, DN_HEAD_DIM, DN_CONV
    first = (pl.program_id(0) % tiles_per_seq) == 0
    prev = prev_ref[...].astype(F32)
    prev = jnp.where(first, 0.0, prev)
    xcat = jnp.concatenate([prev, cur_ref[...].astype(F32)], axis=0)
    P = prev.shape[0]
    cw = cw_ref[...]
    y = None
    for j in range(W):
        shift = W - 1 - j
        rolled = xcat if shift == 0 else pltpu.roll(xcat, shift, axis=0)
        term = rolled[P:P + tm] * cw[j:j + 1]
        y = term if y is None else y + term
    y = y * _sigmoid(y)
    width = H * dh
    for h in range(H):
        qh = y[:, h * dh:(h + 1) * dh]
        kh = y[:, width + h * dh:width + (h + 1) * dh]
        qn = qh * lax.rsqrt(jnp.sum(qh * qh, axis=-1, keepdims=True) + 1e-6) * (dh ** -0.5)
        kn = kh * lax.rsqrt(jnp.sum(kh * kh, axis=-1, keepdims=True) + 1e-6)
        qkv_ref[:, h * dh:(h + 1) * dh] = qn.astype(qkv_ref.dtype)
        qkv_ref[:, width + h * dh:width + (h + 1) * dh] = kn.astype(qkv_ref.dtype)
    qkv_ref[:, 2 * width:] = y[:, 2 * width:].astype(qkv_ref.dtype)

    t2 = t2_ref[...]
    xs = t2 + dtb_ref[...]
    softplus = jnp.maximum(xs, 0.0) + jnp.log(1.0 + jnp.exp(-jnp.abs(xs)))
    g = -jnp.exp(alog_ref[...]) * softplus
    ri = lax.broadcasted_iota(jnp.int32, (tm, tm), 0)
    ci = lax.broadcasted_iota(jnp.int32, (tm, tm), 1)
    same_chunk = (ri // DN_CHUNK) == (ci // DN_CHUNK)
    tri = jnp.where(jnp.logical_and(same_chunk, ri >= ci), 1.0, 0.0)
    gc = jnp.dot(tri, g, preferred_element_type=F32, precision=lax.Precision.HIGHEST)
    lane = lax.broadcasted_iota(jnp.int32, t2.shape, 1)
    gb_ref[...] = jnp.where(lane < H, _sigmoid(t2), gc)


def dn_prep(proj_b, proj_small, conv_w, alog_vec, dtb_vec, layer, seq_len):
    T = proj_b.shape[0]
    width3 = 3 * DN_HEADS * DN_HEAD_DIM
    tm = _tile(seq_len, 256)
    P = 16
    tail2_col = proj_small.shape[1] // LANES - 1
    return pl.pallas_call(
        functools.partial(_dn_prep_kernel, tiles_per_seq=seq_len // tm, tm=tm),
        grid=(T // tm,),
        in_specs=[pl.BlockSpec((tm, width3), lambda i: (i, 0)),
                  pl.BlockSpec((P, width3), lambda i: (jnp.maximum(i * (tm // P) - 1, 0), 0)),
                  pl.BlockSpec((None, DN_CONV, width3), lambda i: (layer, 0, 0)),
                  pl.BlockSpec((tm, LANES), lambda i: (i, tail2_col)),
                  pl.BlockSpec((None, 1, LANES), lambda i: (layer, 0, 0)),
                  pl.BlockSpec((None, 1, LANES), lambda i: (layer, 0, 0))],
        out_specs=[pl.BlockSpec((tm, width3), lambda i: (i, 0)),
                   pl.BlockSpec((tm, LANES), lambda i: (i, 0))],
        out_shape=[jax.ShapeDtypeStruct((T, width3), BF16),
                   jax.ShapeDtypeStruct((T, LANES), F32)],
        compiler_params=_cparams("parallel"),
        name="dn_prep",
    )(proj_b, proj_b, conv_w, proj_small, alog_vec, dtb_vec)


def _unit_lower_inverse_many(ls):
    C = ls[0].shape[0]
    assert C == 64
    ri = lax.broadcasted_iota(jnp.int32, (C, C), 0)
    ci = lax.broadcasted_iota(jnp.int32, (C, C), 1)
    eye = jnp.where(ri == ci, 1.0, 0.0)

    def mm(a, b):
        return [_dot(x.astype(BF16), y.astype(BF16)) for x, y in zip(a, b)]

    l2 = mm(ls, ls)
    l4 = mm(l2, l2)
    l3 = mm(ls, l2)
    l8 = mm(l4, l4)
    x1 = [b - a - c for a, b, c in zip(ls, l2, l3)]
    l16 = mm(l8, l8)
    l12 = mm(l4, l8)
    x2 = [a + b + c for a, b, c in zip(l4, l8, l12)]
    l32 = mm(l16, l16)
    x12 = mm(x1, x2)
    y = [a + b + c for a, b, c in zip(x1, x2, x12)]
    l48 = mm(l16, l32)
    x3 = [a + b + c for a, b, c in zip(l16, l32, l48)]
    yx3 = mm(y, x3)
    return [eye + a + b + c for a, b, c in zip(y, x3, yx3)]


def _dn_chunk_kernel(qkv_ref, z_ref, gb_ref, grow_ref, norm_ref, o_ref,
                     state_ref, u_ref, w_ref, qe_ref, kd_ref, qk_ref, *, rows):
    H, dh, C = DN_HEADS, DN_HEAD_DIM, DN_CHUNK
    width = H * dh
    heads = range(H)

    @pl.when(pl.program_id(1) == 0)
    def _():
        state_ref[...] = jnp.zeros_like(state_ref)

    ri = lax.broadcasted_iota(jnp.int32, (C, C), 0)
    ci = lax.broadcasted_iota(jnp.int32, (C, C), 1)
    incl = ri >= ci
    strict = ri > ci
    gain = norm_ref[...]

    def prepare(c2, carry):
        r0s = [pl.multiple_of((par * c2 + j) * C, C) for j in range(par)]
        gbs = [gb_ref[pl.ds(r0, C), :] for r0 in r0s]
        grows = [grow_ref[par * c2 + j] for j in range(par)]
        probs = [(j, h) for j in range(par) for h in heads]
        q = [qkv_ref[pl.ds(r0s[j], C), h * dh:(h + 1) * dh] for j, h in probs]
        k = [qkv_ref[pl.ds(r0s[j], C), width + h * dh:width + (h + 1) * dh] for j, h in probs]
        v = [qkv_ref[pl.ds(r0s[j], C), 2 * width + h * dh:2 * width + (h + 1) * dh] for j, h in probs]
        beta = [gbs[j][:, h:h + 1] for j, h in probs]
        gcol = [gbs[j][:, H + h:H + h + 1] for j, h in probs]
        np_ = range(len(probs))
        decay = [jnp.exp(jnp.where(incl, gcol[p] - grows[j][h:h + 1, :], NEG_INF)) for p, (j, h) in enumerate(probs)]
        kf = [k[p].astype(F32) for p in np_]
        kbeta = [kf[p] * beta[p] for p in np_]
        eg = [jnp.exp(gcol[p]) for p in np_]
        kk = [_dot_nt(kbeta[p].astype(BF16), k[p]) for p in np_]
        qk = [_dot_nt(q[p], k[p]) for p in np_]
        tmat = _unit_lower_inverse_many([jnp.where(strict, kk[p] * decay[p], 0.0) for p in np_])
        tb = [t.astype(BF16) for t in tmat]
        u = [_dot(tb[p], (v[p].astype(F32) * beta[p]).astype(BF16)) for p in np_]
        w = [_dot(tb[p], (kbeta[p] * eg[p]).astype(BF16)) for p in np_]
        for p, (j, h) in enumerate(probs):
            r0 = r0s[j]
            g_last = gcol[p][C - 1:C, :]
            u_ref[h, pl.ds(r0, C), :] = u[p]
            w_ref[h, pl.ds(r0, C), :] = w[p].astype(BF16)
            qe_ref[h, pl.ds(r0, C), :] = (q[p].astype(F32) * eg[p]).astype(BF16)
            kd_ref[h, pl.ds(r0, C), :] = (kf[p] * jnp.exp(g_last - gcol[p])).astype(BF16)
            qk_ref[h, pl.ds(r0, C), :] = jnp.where(incl, qk[p] * decay[p], 0.0).astype(BF16)
        return carry

    par = 4 if (rows // C) % 4 == 0 else 1
    lax.fori_loop(0, rows // (C * par), prepare, 0)

    def scan(c, carry):
        r0 = pl.multiple_of(c * C, C)
        gb = gb_ref[pl.ds(r0, C), :]
        state = [state_ref[h] for h in heads]
        sb = [s.astype(BF16) for s in state]
        w_s = [_dot(w_ref[h, pl.ds(r0, C), :], sb[h]) for h in heads]
        q_s = [_dot(qe_ref[h, pl.ds(r0, C), :], sb[h]) for h in heads]
        vnb = [(u_ref[h, pl.ds(r0, C), :] - w_s[h]).astype(BF16) for h in heads]
        o_in = [_dot(qk_ref[h, pl.ds(r0, C), :], vnb[h]) for h in heads]
        kv = [_dot_tn(kd_ref[h, pl.ds(r0, C), :], vnb[h]) for h in heads]
        for h in heads:
            g_last = gb[C - 1:C, H + h:H + h + 1]
            state_ref[h] = state[h] * jnp.exp(g_last) + kv[h]
            z = z_ref[pl.ds(r0, C), h * dh:(h + 1) * dh].astype(F32)
            y = _rms(q_s[h] + o_in[h], gain) * (z * _sigmoid(z))
            o_ref[pl.ds(r0, C), h * dh:(h + 1) * dh] = y.astype(o_ref.dtype)
        return carry

    lax.fori_loop(0, rows // C, scan, 0, unroll=2)


def dn_chunked(qkvn, proj_b, gb, grow, dn_norm, layer, batch, seq_len):
    T = qkvn.shape[0]
    H, dh, C = DN_HEADS, DN_HEAD_DIM, DN_CHUNK
    width = H * dh
    rows = _tile(seq_len, 512)
    tps = seq_len // rows
    zcol = 3
    return pl.pallas_call(
        functools.partial(_dn_chunk_kernel, rows=rows),
        grid=(batch, tps),
        in_specs=[pl.BlockSpec((rows, 3 * width), lambda b, n: (b * tps + n, 0)),
                  pl.BlockSpec((rows, width), lambda b, n: (b * tps + n, zcol)),
                  pl.BlockSpec((rows, LANES), lambda b, n: (b * tps + n, 0)),
                  pl.BlockSpec((rows // C, H, C), lambda b, n: (b * tps + n, 0, 0)),
                  pl.BlockSpec((None, 1, dh), lambda b, n: (layer, 0, 0))],
        out_specs=pl.BlockSpec((rows, width), lambda b, n: (b * tps + n, 0)),
        out_shape=jax.ShapeDtypeStruct((T, width), BF16),
        scratch_shapes=[pltpu.VMEM((H, dh, dh), F32),
                        pltpu.VMEM((H, rows, dh), F32),
                        pltpu.VMEM((H, rows, dh), BF16),
                        pltpu.VMEM((H, rows, dh), BF16),
                        pltpu.VMEM((H, rows, dh), BF16),
                        pltpu.VMEM((H, rows, C), BF16)],
        compiler_params=_cparams("parallel", "arbitrary"),
        name="dn_chunked",
    )(qkvn, proj_b, gb, grow, dn_norm)


def _rotate_pairs(x, cos_tab, sin_tab):
    return x * cos_tab + pltpu.roll(x, LANES // 2, axis=1) * sin_tab


def _mla_q_kernel(c_ref, g_ref, w_ref, cos_ref, sin_ref, o_ref):
    H = MLA_HEADS
    scale = (MLA_NOPE + MLA_ROPE) ** -0.5 * float(np.log2(np.e))
    cn = _rms(c_ref[...], g_ref[...]).astype(BF16)
    acc = _dot(cn, w_ref[...]) * scale
    cos_tab, sin_tab = cos_ref[...], sin_ref[...]
    for h in range(H):
        o_ref[:, 2 * h * LANES:(2 * h + 1) * LANES] = acc[:, h * LANES:(h + 1) * LANES].astype(o_ref.dtype)
        xr = acc[:, (H + h) * LANES:(H + h + 1) * LANES]
        o_ref[:, (2 * h + 1) * LANES:(2 * h + 2) * LANES] = _rotate_pairs(xr, cos_tab, sin_tab).astype(o_ref.dtype)


def mla_q(proj_small, gains, wq, cos_tab, sin_tab, layer):
    T = proj_small.shape[0]
    R = wq.shape[1]
    tm = _tile(T, 512)
    N = wq.shape[2]
    return pl.pallas_call(
        _mla_q_kernel,
        grid=(T // tm,),
        in_specs=[pl.BlockSpec((tm, R), lambda i: (i, 0)),
                  pl.BlockSpec((None, 1, R), lambda i: (layer, 0, 0)),
                  pl.BlockSpec((None, R, N), lambda i: (layer, 0, 0)),
                  pl.BlockSpec((tm, LANES), lambda i: (i, 0)),
                  pl.BlockSpec((tm, LANES), lambda i: (i, 0))],
        out_specs=pl.BlockSpec((tm, N), lambda i: (i, 0)),
        out_shape=jax.ShapeDtypeStruct((T, N), BF16),
        compiler_params=_cparams("parallel"),
        name="mla_q",
    )(proj_small, gains, wq, cos_tab, sin_tab)


def _mla_kv_kernel(c_ref, kr_ref, g_ref, w_ref, cos_ref, sin_ref, k_ref, v_ref):
    H = MLA_HEADS
    cn = _rms(c_ref[...], g_ref[...]).astype(BF16)
    acc = _dot(cn, w_ref[...])
    kr = _rotate_pairs(kr_ref[...], cos_ref[...], sin_ref[...]).astype(k_ref.dtype)
    for h in range(H):
        k_ref[:, 2 * h * LANES:(2 * h + 1) * LANES] = acc[:, h * LANES:(h + 1) * LANES].astype(k_ref.dtype)
        k_ref[:, (2 * h + 1) * LANES:(2 * h + 2) * LANES] = kr
    v_ref[...] = acc[:, H * LANES:].astype(v_ref.dtype)


def mla_kv(proj_small, gains, wkv, cos_tab, sin_tab, layer):
    T = proj_small.shape[0]
    R = wkv.shape[1]
    tm = _tile(T, 512)
    N = wkv.shape[2]
    H = MLA_HEADS
    kr_col = (2 * R) // LANES
    return pl.pallas_call(
        _mla_kv_kernel,
        grid=(T // tm,),
        in_specs=[pl.BlockSpec((tm, R), lambda i: (i, 1)),
                  pl.BlockSpec((tm, LANES), lambda i: (i, kr_col)),
                  pl.BlockSpec((None, 1, R), lambda i: (layer, 0, 0)),
                  pl.BlockSpec((None, R, N), lambda i: (layer, 0, 0)),
                  pl.BlockSpec((tm, LANES), lambda i: (i, 0)),
                  pl.BlockSpec((tm, LANES), lambda i: (i, 0))],
        out_specs=[pl.BlockSpec((tm, 2 * H * LANES), lambda i: (i, 0)),
                   pl.BlockSpec((tm, H * MLA_V), lambda i: (i, 0))],
        out_shape=[jax.ShapeDtypeStruct((T, 2 * H * LANES), BF16),
                   jax.ShapeDtypeStruct((T, H * MLA_V), BF16)],
        compiler_params=_cparams("parallel"),
        name="mla_kv",
    )(proj_small, proj_small, gains, wkv, cos_tab, sin_tab)


def _flash_kernel(qi_ref, kj_ref, q_ref, k_ref, v_ref, o_ref, m_ref, acc_ref, *, tq, tk):
    H, dv = MLA_HEADS, MLA_V
    dqk = 2 * LANES
    qi = qi_ref[pl.program_id(1)]
    kj = kj_ref[pl.program_id(1)]

    @pl.when(kj == 0)
    def _():
        m_ref[...] = jnp.full_like(m_ref, NEG_INF)
        acc_ref[...] = jnp.zeros_like(acc_ref)

    def step(masked):
        if masked:
            row = lax.broadcasted_iota(jnp.int32, (tq, tk), 0)
            col = lax.broadcasted_iota(jnp.int32, (tq, tk), 1)
            keep = col <= row
        ones = jnp.ones((tk, dv), BF16)
        scores = [_dot_nt(q_ref[:, h * dqk:(h + 1) * dqk], k_ref[:, h * dqk:(h + 1) * dqk]) for h in range(H)]
        probs, alphas = [], []
        for h in range(H):
            s = scores[h]
            if masked:
                s = jnp.where(keep, s, NEG_INF)
            m_prev = m_ref[h]
            m_new = jnp.maximum(m_prev, jnp.max(s, axis=-1, keepdims=True))
            alpha = jnp.exp2(m_prev - m_new)
            p = jnp.exp2(s - _lane_repeat(m_new, tk // LANES))
            m_ref[h] = m_new
            alphas.append(alpha)
            probs.append(p.astype(BF16))
        for h in range(H):
            v_ext = jnp.concatenate([v_ref[:, h * dv:(h + 1) * dv], ones], axis=1)
            acc_ref[h] = _lane_repeat(alphas[h], 2) * acc_ref[h] + _dot(probs[h], v_ext)

    @pl.when(kj < qi)
    def _():
        step(False)

    @pl.when(kj == qi)
    def _():
        step(True)
        for h in range(H):
            acc = acc_ref[h]
            o_ref[:, h * dv:(h + 1) * dv] = (acc[:, :dv] / acc[:, dv:]).astype(o_ref.dtype)


def mla_flash(qf, kf, v, batch, seq_len):
    T = qf.shape[0]
    H, dv = MLA_HEADS, MLA_V
    t = _tile(seq_len, 512)
    nq = seq_len // t
    pairs = [(i, j) for i in range(nq) for j in range(i + 1)]
    qi_tab = jnp.asarray([i for i, _ in pairs], jnp.int32)
    kj_tab = jnp.asarray([j for _, j in pairs], jnp.int32)
    return pl.pallas_call(
        functools.partial(_flash_kernel, tq=t, tk=t),
        grid_spec=pltpu.PrefetchScalarGridSpec(
            num_scalar_prefetch=2,
            grid=(batch, len(pairs)),
            in_specs=[pl.BlockSpec((t, qf.shape[1]), lambda b, n, qi, kj: (b * nq + qi[n], 0)),
                      pl.BlockSpec((t, kf.shape[1]), lambda b, n, qi, kj: (b * nq + kj[n], 0)),
                      pl.BlockSpec((t, v.shape[1]), lambda b, n, qi, kj: (b * nq + kj[n], 0))],
            out_specs=pl.BlockSpec((t, H * dv), lambda b, n, qi, kj: (b * nq + qi[n], 0)),
            scratch_shapes=[pltpu.VMEM((H, t, LANES), F32),
                            pltpu.VMEM((H, t, 2 * dv), F32)]),
        out_shape=jax.ShapeDtypeStruct((T, H * dv), BF16),
        compiler_params=_cparams("parallel", "arbitrary"),
        name="mla_flash",
    )(qi_tab, kj_tab, qf, kf, v)


def _merge_kernel(ya_ref, yb_ref, yc_ref, ga_ref, gb_ref, gc_ref, wa_ref, wb_ref, wc_ref, o_ref, sa_ref, sb_ref, sc_ref):
    @pl.when(pl.program_id(1) == 0)
    def _():
        sa_ref[...] = wa_ref[...].astype(BF16)
        sb_ref[...] = wb_ref[...].astype(BF16)
        sc_ref[...] = wc_ref[...].astype(BF16)

    acc = ga_ref[...].astype(F32) * _dot(ya_ref[...], sa_ref[...])
    acc = acc + gb_ref[...].astype(F32) * _dot(yb_ref[...], sb_ref[...])
    acc = acc + gc_ref[...].astype(F32) * _dot(yc_ref[...], sc_ref[...])
    o_ref[...] = acc.astype(o_ref.dtype)


def merge_branches(ya, yb, yc, gates, w_branch, layer):
    T, Wb = ya.shape
    D = w_branch.shape[-1]
    tm = _tile(T, 1024)
    tn = _tile(D, 512)
    nj = D // tn
    y_spec = pl.BlockSpec((tm, Wb), lambda j, i: (i, 0))

    def g_spec(n):
        return pl.BlockSpec((tm, tn), lambda j, i: (i, n * nj + j))

    def w_spec(n):
        return pl.BlockSpec((None, None, Wb, tn), lambda j, i: (layer, n, 0, j))

    return pl.pallas_call(
        _merge_kernel,
        grid=(nj, T // tm),
        in_specs=[y_spec, y_spec, y_spec, g_spec(0), g_spec(1), g_spec(2), w_spec(0), w_spec(1), w_spec(2)],
        out_specs=pl.BlockSpec((tm, tn), lambda j, i: (i, j)),
        out_shape=jax.ShapeDtypeStruct((T, D), BF16),
        scratch_shapes=[pltpu.VMEM((Wb, tn), BF16)] * 3,
        compiler_params=_cparams("arbitrary", "arbitrary"),
        name="merge_branches",
    )(ya, yb, yc, gates, gates, gates, w_branch, w_branch, w_branch)


def _out_proj_kernel(m_ref, w_ref, x_ref, g_ref, xo_ref, *maybe_ho_ref):
    xn = x_ref[...] + _dot(m_ref[...], w_ref[...])
    xo_ref[...] = xn
    for ho_ref in maybe_ho_ref:
        ho_ref[...] = _rms(xn, g_ref[...]).astype(ho_ref.dtype)


def out_proj_residual(merged, w_out_bf, x, norm_gain, layer, emit_norm):
    T, D = x.shape
    tm = _tile(T, 512)
    row_spec = pl.BlockSpec((tm, D), lambda i: (i, 0))
    out_specs = [row_spec, row_spec] if emit_norm else [row_spec]
    out_shape = [jax.ShapeDtypeStruct((T, D), F32)] + ([jax.ShapeDtypeStruct((T, D), BF16)] if emit_norm else [])
    return pl.pallas_call(
        _out_proj_kernel,
        grid=(T // tm,),
        in_specs=[row_spec,
                  pl.BlockSpec((None, D, D), lambda i: (layer, 0, 0)),
                  row_spec,
                  pl.BlockSpec((None, 1, D), lambda i: (layer, 0, 0))],
        out_specs=out_specs,
        out_shape=out_shape,
        compiler_params=_cparams("parallel"),
        name="out_proj_residual",
    )(merged, w_out_bf, x, norm_gain)


def _ffn_kernel(h_ref, wg_ref, wu_ref, wd_ref, y_ref, acc_ref):
    f = pl.program_id(1)

    @pl.when(f == 0)
    def _():
        acc_ref[...] = jnp.zeros_like(acc_ref)

    h = h_ref[...]
    gate = _dot(h, wg_ref[...])
    up = _dot(h, wu_ref[...])
    act = (gate * _sigmoid(gate) * up).astype(BF16)
    acc_ref[...] += _dot(act, wd_ref[...])

    @pl.when(f == pl.num_programs(1) - 1)
    def _():
        y_ref[...] = acc_ref[...].astype(y_ref.dtype)


def ffn_swiglu(h, wg, wu, wd, widx):
    T, D = h.shape
    F = wg.shape[-1]
    tm = _tile(T, 512)
    tf = _tile(F, 1024)
    return pl.pallas_call(
        _ffn_kernel,
        grid=(T // tm, F // tf),
        in_specs=[pl.BlockSpec((tm, D), lambda i, f: (i, 0)),
                  pl.BlockSpec((None, D, tf), lambda i, f: (widx, 0, f)),
                  pl.BlockSpec((None, D, tf), lambda i, f: (widx, 0, f)),
                  pl.BlockSpec((None, tf, D), lambda i, f: (widx, f, 0))],
        out_specs=pl.BlockSpec((tm, D), lambda i, f: (i, 0)),
        out_shape=jax.ShapeDtypeStruct((T, D), BF16),
        scratch_shapes=[pltpu.VMEM((tm, D), F32)],
        compiler_params=_cparams("parallel", "arbitrary"),
        name="ffn_swiglu",
    )(h, wg, wu, wd)


def _router_kernel(x_ref, g_ref, w_ref, o_ref):
    h = _rms(x_ref[...], g_ref[...])
    logits = jnp.dot(h, w_ref[...], preferred_element_type=F32, precision=lax.Precision.HIGHEST)
    lane = lax.broadcasted_iota(jnp.int32, logits.shape, 1)
    lanef = lane.astype(F32)
    big = float(LANES)
    logits = jnp.where(lane < N_EXPERTS, logits, NEG_INF)
    l1 = jnp.max(logits, axis=-1, keepdims=True)
    i1 = jnp.min(jnp.where(logits == l1, lanef, big), axis=-1, keepdims=True)
    rest = jnp.where(lanef == i1, NEG_INF, logits)
    l2 = jnp.max(rest, axis=-1, keepdims=True)
    i2 = jnp.min(jnp.where(rest == l2, lanef, big), axis=-1, keepdims=True)
    e2 = jnp.exp(l2 - l1)
    w1 = 1.0 / (1.0 + e2)
    w2 = e2 / (1.0 + e2)
    out = jnp.where(lane == 0, i1, jnp.where(lane == 1, i2, jnp.where(lane == 2, w1, jnp.where(lane == 3, w2, 0.0))))
    o_ref[...] = out


def moe_router(x, norm_gain, w_router_pad, layer, widx):
    T, D = x.shape
    tm = _tile(T, 512)
    return pl.pallas_call(
        _router_kernel,
        grid=(T // tm,),
        in_specs=[pl.BlockSpec((tm, D), lambda i: (i, 0)),
                  pl.BlockSpec((None, 1, D), lambda i: (layer, 0, 0)),
                  pl.BlockSpec((None, D, LANES), lambda i: (widx, 0, 0))],
        out_specs=pl.BlockSpec((tm, LANES), lambda i: (i, 0)),
        out_shape=jax.ShapeDtypeStruct((T, LANES), F32),
        compiler_params=_cparams("parallel"),
        name="moe_router",
    )(x, norm_gain, w_router_pad)


def _start_row_gather(idx_hbm, src_hbm, blk, slot, idx_smem, rows_vmem, idx_sem, row_sem, n):
    idx_copy = pltpu.make_async_copy(idx_hbm.at[blk], idx_smem.at[slot], idx_sem.at[slot])
    idx_copy.start()
    idx_copy.wait()

    def issue(i, c):
        for priority in range(2):
            r = 2 * i + priority
            pltpu.make_async_copy(src_hbm.at[pl.ds(idx_smem[slot, r], 1)],
                                  rows_vmem.at[slot, pl.ds(r, 1)],
                                  row_sem.at[slot]).start(priority=priority)
        return c

    lax.fori_loop(0, n // 2, issue, 0, unroll=4)


def _wait_row_gather(src_hbm, slot, rows_vmem, row_sem, n):
    pltpu.make_async_copy(src_hbm.at[pl.ds(0, n)], rows_vmem.at[slot], row_sem.at[slot]).wait()


def _gather_rows_kernel(meta_ref, idx_hbm, x_hbm, g_ref, o_ref, idx_smem, rows_vmem, idx_sem, row_sem, *, bm):
    m = pl.program_id(0)
    n_used = meta_ref[pl.num_programs(0)]
    slot = m % 2
    start = functools.partial(_start_row_gather, idx_hbm, x_hbm, idx_smem=idx_smem, rows_vmem=rows_vmem,
                              idx_sem=idx_sem, row_sem=row_sem, n=bm)

    @pl.when(m == 0)
    def _():
        start(blk=0, slot=0)

    @pl.when(m + 1 < n_used)
    def _():
        start(blk=m + 1, slot=1 - slot)

    @pl.when(m < n_used)
    def _():
        _wait_row_gather(x_hbm, slot, rows_vmem, row_sem, bm)
        o_ref[...] = _rms(rows_vmem[slot], g_ref[...]).astype(o_ref.dtype)

    @pl.when(m >= n_used)
    def _():
        o_ref[...] = jnp.zeros_like(o_ref)


def moe_gather_norm(x, row_tok, meta, norm_gain, layer, bm):
    T, D = x.shape
    n_blocks = row_tok.shape[0]
    return pl.pallas_call(
        functools.partial(_gather_rows_kernel, bm=bm),
        grid_spec=pltpu.PrefetchScalarGridSpec(
            num_scalar_prefetch=1,
            grid=(n_blocks,),
            in_specs=[pl.BlockSpec(memory_space=pl.ANY),
                      pl.BlockSpec(memory_space=pl.ANY),
                      pl.BlockSpec((None, 1, D), lambda i, meta: (layer, 0, 0))],
            out_specs=pl.BlockSpec((bm, D), lambda i, meta: (i, 0)),
            scratch_shapes=[pltpu.SMEM((2, bm), jnp.int32),
                            pltpu.VMEM((2, bm, D), F32),
                            pltpu.SemaphoreType.DMA((2,)),
                            pltpu.SemaphoreType.DMA((2,))]),
        out_shape=jax.ShapeDtypeStruct((n_blocks * bm, D), BF16),
        compiler_params=_cparams("arbitrary"),
        name="moe_gather_norm",
    )(meta, row_tok, x, norm_gain)


def _expert_up_kernel(meta_ref, x_ref, wg_ref, wu_ref, o_ref, wgb_ref, wub_ref):
    m = pl.program_id(1)
    n_used = meta_ref[pl.num_programs(1)]
    new_expert = jnp.logical_or(m == 0, meta_ref[m] != meta_ref[jnp.maximum(m - 1, 0)])

    @pl.when(jnp.logical_and(new_expert, m < n_used))
    def _():
        wgb_ref[...] = wg_ref[...].astype(BF16)
        wub_ref[...] = wu_ref[...].astype(BF16)

    @pl.when(m < n_used)
    def _():
        x = x_ref[...]
        gate = _dot(x, wgb_ref[...])
        up = _dot(x, wub_ref[...])
        o_ref[...] = (gate * _sigmoid(gate) * up).astype(o_ref.dtype)

    @pl.when(m >= n_used)
    def _():
        o_ref[...] = jnp.zeros_like(o_ref)


def _last_used(m, meta, nb):
    return jnp.minimum(m, meta[nb] - 1)


def moe_expert_up(xs, meta, wg, wu, widx, bm):
    R, D = xs.shape
    F = wg.shape[-1]
    tf = _tile(F, 1024)
    nb = R // bm

    def w_map(f, m, meta):
        return (widx, meta[_last_used(m, meta, nb)], 0, f)

    return pl.pallas_call(
        _expert_up_kernel,
        grid_spec=pltpu.PrefetchScalarGridSpec(
            num_scalar_prefetch=1,
            grid=(F // tf, nb),
            in_specs=[pl.BlockSpec((bm, D), lambda f, m, meta: (_last_used(m, meta, nb), 0)),
                      pl.BlockSpec((None, None, D, tf), w_map),
                      pl.BlockSpec((None, None, D, tf), w_map)],
            out_specs=pl.BlockSpec((bm, tf), lambda f, m, meta: (m, f)),
            scratch_shapes=[pltpu.VMEM((D, tf), BF16), pltpu.VMEM((D, tf), BF16)]),
        out_shape=jax.ShapeDtypeStruct((R, F), BF16),
        compiler_params=_cparams("arbitrary", "arbitrary"),
        name="moe_expert_up",
    )(meta, xs, wg, wu)


def _expert_down_kernel(meta_ref, a_ref, wd_ref, o_ref, wdb_ref):
    m = pl.program_id(1)
    n_used = meta_ref[pl.num_programs(1)]
    new_expert = jnp.logical_or(m == 0, meta_ref[m] != meta_ref[jnp.maximum(m - 1, 0)])

    @pl.when(jnp.logical_and(new_expert, m < n_used))
    def _():
        wdb_ref[...] = wd_ref[...].astype(BF16)

    @pl.when(m < n_used)
    def _():
        o_ref[...] = _dot(a_ref[...], wdb_ref[...]).astype(o_ref.dtype)

    @pl.when(m >= n_used)
    def _():
        o_ref[...] = jnp.zeros_like(o_ref)


def moe_expert_down(act, meta, wd, widx, bm):
    R, F = act.shape
    D = wd.shape[-1]
    tn = _tile(D, 512)
    nb = R // bm
    return pl.pallas_call(
        _expert_down_kernel,
        grid_spec=pltpu.PrefetchScalarGridSpec(
            num_scalar_prefetch=1,
            grid=(D // tn, nb),
            in_specs=[pl.BlockSpec((bm, F), lambda j, m, meta: (_last_used(m, meta, nb), 0)),
                      pl.BlockSpec((None, None, F, tn),
                                   lambda j, m, meta: (widx, meta[_last_used(m, meta, nb)], 0, j))],
            out_specs=pl.BlockSpec((bm, tn), lambda j, m, meta: (m, j)),
            scratch_shapes=[pltpu.VMEM((F, tn), BF16)]),
        out_shape=jax.ShapeDtypeStruct((R, D), F32),
        compiler_params=_cparams("arbitrary", "arbitrary"),
        name="moe_expert_down",
    )(meta, act, wd)


def _combine_kernel(dest_hbm, yb_hbm, rt_ref, y_ref, idx_smem, rows_vmem, idx_sem, row_sem, *, tm):
    m = pl.program_id(0)
    slot = m % 2
    n = TOP_K * tm
    start = functools.partial(_start_row_gather, dest_hbm, yb_hbm, idx_smem=idx_smem, rows_vmem=rows_vmem,
                              idx_sem=idx_sem, row_sem=row_sem, n=n)

    @pl.when(m == 0)
    def _():
        start(blk=0, slot=0)

    @pl.when(m + 1 < pl.num_programs(0))
    def _():
        start(blk=m + 1, slot=1 - slot)

    _wait_row_gather(yb_hbm, slot, rows_vmem, row_sem, n)
    rt = rt_ref[...]
    y = rt[:, 2:3] * rows_vmem[slot, 0:tm, :] + rt[:, 3:4] * rows_vmem[slot, tm:2 * tm, :]
    y_ref[...] = y.astype(y_ref.dtype)


def moe_combine(yb, dest_blocks, route, tm):
    T = route.shape[0]
    D = yb.shape[1]
    return pl.pallas_call(
        functools.partial(_combine_kernel, tm=tm),
        grid=(T // tm,),
        in_specs=[pl.BlockSpec(memory_space=pl.ANY),
                  pl.BlockSpec(memory_space=pl.ANY),
                  pl.BlockSpec((tm, LANES), lambda i: (i, 0))],
        out_specs=pl.BlockSpec((tm, D), lambda i: (i, 0)),
        out_shape=jax.ShapeDtypeStruct((T, D), BF16),
        scratch_shapes=[pltpu.SMEM((2, TOP_K * tm), jnp.int32),
                        pltpu.VMEM((2, TOP_K * tm, D), F32),
                        pltpu.SemaphoreType.DMA((2,)),
                        pltpu.SemaphoreType.DMA((2,))],
        compiler_params=_cparams("arbitrary"),
        name="moe_combine",
    )(dest_blocks, yb, route)


def moe_layer(x, norm_ffn, w_router_pad, wg, wu, wd, layer, widx):
    T, D = x.shape
    E = N_EXPERTS
    bm = _tile(T, 512)
    route = moe_router(x, norm_ffn, w_router_pad, layer, widx)
    flat_e = route[:, :TOP_K].astype(jnp.int32).reshape(-1)
    onehot = (flat_e[:, None] == jnp.arange(E, dtype=jnp.int32)[None, :]).astype(jnp.int32)
    incl = jnp.cumsum(onehot, axis=0)
    counts = incl[-1]
    rank = jnp.sum((incl - onehot) * onehot, axis=1)
    padded = (counts + bm - 1) // bm * bm
    pad_end = jnp.cumsum(padded)
    pad_start = pad_end - padded
    dest = pad_start[flat_e] + rank
    n_rows = -(-(T * TOP_K + E * (bm - 1)) // bm) * bm
    n_blocks = n_rows // bm
    flat_tok = jnp.repeat(jnp.arange(T, dtype=jnp.int32), TOP_K)
    row_tok = jnp.zeros((n_rows,), jnp.int32).at[dest].set(flat_tok)
    block_e = jnp.minimum(
        jnp.sum(jnp.arange(n_blocks, dtype=jnp.int32)[:, None] * bm >= pad_end[None, :], axis=1), E - 1
    ).astype(jnp.int32)
    meta = jnp.concatenate([block_e, (pad_end[-1:] // bm).astype(jnp.int32)])

    xs = moe_gather_norm(x, row_tok.reshape(n_blocks, bm), meta, norm_ffn, layer, bm)
    act = moe_expert_up(xs, meta, wg, wu, widx, bm)
    yb = moe_expert_down(act, meta, wd, widx, bm)
    tmc = _tile(T, 256)
    dest_blocks = dest.reshape(T // tmc, tmc, TOP_K).transpose(0, 2, 1).reshape(T // tmc, TOP_K * tmc)
    return moe_combine(yb, dest_blocks, route, tmc)


def _ple_kernel(x_ref, y_ref, p_ref, wg_ref, wp_ref, gp_ref, gn_ref, xo_ref, ho_ref):
    x = x_ref[...] + y_ref[...].astype(F32)
    hp = _rms(x, gp_ref[...]).astype(BF16)
    gate = _sigmoid(_dot(hp, wg_ref[...]))
    emb = _dot(p_ref[...].astype(BF16), wp_ref[...])
    xn = x + emb * gate
    xo_ref[...] = xn
    ho_ref[...] = _rms(xn, gn_ref[...]).astype(ho_ref.dtype)


def ple_layer(x, y, p, w_gate_bf, w_proj_bf, ple_gain, next_gain, layer, gain_idx, out_dtype):
    T, D = x.shape
    Pd = p.shape[-1]
    tm = _tile(T, 512)
    row_spec = pl.BlockSpec((tm, D), lambda i: (i, 0))
    return pl.pallas_call(
        _ple_kernel,
        grid=(T // tm,),
        in_specs=[row_spec,
                  row_spec,
                  pl.BlockSpec((None, tm, Pd), lambda i: (layer, i, 0)),
                  pl.BlockSpec((None, D, D), lambda i: (layer, 0, 0)),
                  pl.BlockSpec((None, Pd, D), lambda i: (layer, 0, 0)),
                  pl.BlockSpec((None, 1, D), lambda i: (layer, 0, 0)),
                  pl.BlockSpec((None, 1, D), lambda i: (gain_idx, 0, 0))],
        out_specs=[row_spec, row_spec],
        out_shape=[jax.ShapeDtypeStruct((T, D), F32), jax.ShapeDtypeStruct((T, D), out_dtype)],
        compiler_params=_cparams("parallel"),
        name="ple_layer",
    )(x, y, p, w_gate_bf, w_proj_bf, ple_gain, next_gain)


def _split_points(D):
    swa_q = SWA_Q_HEADS * SWA_HEAD_DIM
    swa_kv = SWA_KV_HEADS * SWA_HEAD_DIM
    dn_w = DN_HEADS * DN_HEAD_DIM
    return swa_q + 2 * swa_kv, swa_q + 2 * swa_kv + 4 * dn_w


def _pack_small_in_proj(w_in_t, q_lora, kv_lora):
    _, b_end = _split_points(w_in_t.shape[2])
    H = DN_HEADS
    o = b_end
    beta = w_in_t[:, o:o + H]
    decay = w_in_t[:, o + H:o + 2 * H]
    o += 2 * H
    cq = w_in_t[:, o:o + q_lora]
    o += q_lora
    ckv = w_in_t[:, o:o + kv_lora]
    o += kv_lora
    half = MLA_ROPE // 2
    kr1 = w_in_t[:, o:o + half]
    kr2 = w_in_t[:, o + half:o + 2 * half]
    o += MLA_ROPE
    pad = jnp.zeros((w_in_t.shape[0], LANES - 2 * H, w_in_t.shape[2]), w_in_t.dtype)
    packed = jnp.concatenate([cq, ckv, kr1, kr2, kr2, kr1, beta, decay, pad], axis=1)
    return packed, o


def _pack_w_uq(w_uq):
    Ld, R, _ = w_uq.shape
    H, half = MLA_HEADS, MLA_ROPE // 2
    w = w_uq.reshape(Ld, R, H, MLA_NOPE + MLA_ROPE)
    nope = w[..., :MLA_NOPE].reshape(Ld, R, H * MLA_NOPE)
    r1 = w[..., MLA_NOPE:MLA_NOPE + half]
    r2 = w[..., MLA_NOPE + half:]
    rope = jnp.concatenate([r1, r2, r2, r1], axis=-1).reshape(Ld, R, H * LANES)
    return jnp.concatenate([nope, rope], axis=-1).astype(BF16)


def _pack_w_ukv(w_ukv):
    Ld, R, _ = w_ukv.shape
    H = MLA_HEADS
    w = w_ukv.reshape(Ld, R, H, MLA_NOPE + MLA_V)
    kn = w[..., :MLA_NOPE].reshape(Ld, R, H * MLA_NOPE)
    vv = w[..., MLA_NOPE:].reshape(Ld, R, H * MLA_V)
    return jnp.concatenate([kn, vv], axis=-1).astype(BF16)


def _rope_tables(positions):
    half = MLA_ROPE // 2
    inv_freq = ROPE_THETA ** (-jnp.arange(half, dtype=F32) / half)
    ang = positions.astype(F32).reshape(-1)[:, None] * inv_freq
    cos, sin = jnp.cos(ang), jnp.sin(ang)
    zeros = jnp.zeros_like(cos)
    return (jnp.concatenate([cos, cos, zeros, zeros], axis=-1),
            jnp.concatenate([-sin, sin, zeros, zeros], axis=-1))


def _lane_vec(v, offset):
    Ld, n = v.shape
    out = jnp.zeros((Ld, 1, LANES), F32)
    return out.at[:, 0, offset:offset + n].set(v.astype(F32))


def kernel(x, p, positions, norm_mix, w_in, conv_w, dn_a_log, dn_dt_bias, dn_norm, swa_sinks, mla_q_norm, w_uq,
           mla_kv_norm, w_ukv, w_branch, w_out, norm_ffn, w_ffn_gate, w_ffn_up, w_ffn_down, w_router, w_exp_gate,
           w_exp_up, w_exp_down, norm_ple, w_ple_gate, w_ple_proj, final_norm):
    B, S, D = x.shape
    T = B * S
    depth = w_in.shape[0]
    q_lora, kv_lora = w_uq.shape[1], w_ukv.shape[1]
    a_end, b_end = _split_points(D)

    w_in_t = jnp.swapaxes(w_in, 1, 2)
    w_small_t, gate_row0 = _pack_small_in_proj(w_in_t, q_lora, kv_lora)
    w_gates_t = w_in_t[:, gate_row0:]
    wq_packed = _pack_w_uq(w_uq)
    wkv_packed = _pack_w_ukv(w_ukv)
    w_out_bf = w_out.astype(BF16)
    w_ple_gate_bf = w_ple_gate.astype(BF16)
    w_ple_proj_bf = w_ple_proj.astype(BF16)
    wfg, wfu, wfd = w_ffn_gate.astype(BF16), w_ffn_up.astype(BF16), w_ffn_down.astype(BF16)
    weg, weu, wed = w_exp_gate, w_exp_up, w_exp_down
    w_router_pad = jnp.pad(w_router, ((0, 0), (0, 0), (0, LANES - w_router.shape[-1])))
    cos_tab, sin_tab = _rope_tables(positions)
    alog_vec = _lane_vec(dn_a_log, DN_HEADS)
    dtb_vec = _lane_vec(dn_dt_bias, DN_HEADS)
    row = lambda g: g.reshape(g.shape[0], 1, g.shape[-1])
    norm_mix3, norm_ffn3, norm_ple3 = row(norm_mix), row(norm_ffn), row(norm_ple)
    dn_norm3, mla_q_norm3, mla_kv_norm3 = row(dn_norm), row(mla_q_norm), row(mla_kv_norm)
    final3 = final_norm.reshape(1, 1, D)

    xf = x.reshape(T, D)
    pf = p.reshape(depth, T, p.shape[-1])
    h = rmsnorm_rows(xf, norm_mix3, 0, BF16)
    for i in range(depth):
        proj_a = matmul_ws(h, w_in_t, i, n_cols=a_end, row_block_offset=0, tn=a_end // 2, out_dtype=BF16,
                           tm_pref=2048)
        tn_b = 512
        proj_b = matmul_ws(h, w_in_t, i, n_cols=b_end - a_end, row_block_offset=a_end // tn_b, tn=tn_b,
                           out_dtype=BF16, tm_pref=2048)
        proj_small = matmul_ws(h, w_small_t, i, n_cols=w_small_t.shape[1], row_block_offset=0,
                               tn=w_small_t.shape[1], out_dtype=F32)
        gates = matmul_ws(h, w_gates_t, i, n_cols=w_gates_t.shape[1], row_block_offset=0, tn=1024,
                          out_dtype=BF16, act="sigmoid")

        y_a = swa_attention(proj_a, swa_sinks, i, S)

        qkvn, gb = dn_prep(proj_b, proj_small, conv_w, alog_vec, dtb_vec, i, S)
        nchunks = T // DN_CHUNK
        grow = gb[:, DN_HEADS:2 * DN_HEADS].reshape(nchunks, DN_CHUNK, DN_HEADS).transpose(0, 2, 1)
        y_b = dn_chunked(qkvn, proj_b, gb, grow, dn_norm3, i, B, S)

        qfull = mla_q(proj_small, mla_q_norm3, wq_packed, cos_tab, sin_tab, i)
        kfull, vfull = mla_kv(proj_small, mla_kv_norm3, wkv_packed, cos_tab, sin_tab, i)
        y_c = mla_flash(qfull, kfull, vfull, B, S)

        merged = merge_branches(y_a, y_b, y_c, gates, w_branch, i)
        j = i // 2
        if i % 2 == 0:
            xf, h2 = out_proj_residual(merged, w_out_bf, xf, norm_ffn3, i, True)
            y = ffn_swiglu(h2, wfg, wfu, wfd, j)
        else:
            (xf,) = out_proj_residual(merged, w_out_bf, xf, norm_ffn3, i, False)
            y = moe_layer(xf, norm_ffn3, w_router_pad, weg, weu, wed, i, j)

        if i + 1 < depth:
            xf, h = ple_layer(xf, y, pf, w_ple_gate_bf, w_ple_proj_bf, norm_ple3, norm_mix3, i, i + 1, BF16)
        else:
            xf, out = ple_layer(xf, y, pf, w_ple_gate_bf, w_ple_proj_bf, norm_ple3, final3, i, 0, F32)
    return out.reshape(B, S, D)
```

```python
import functools

import jax
import jax.numpy as jnp
import numpy as np
from jax import lax
from jax.experimental import pallas as pl
from jax.experimental.pallas import tpu as pltpu

BF16 = jnp.bfloat16
F32 = jnp.float32
NEG_INF = float("-inf")

NORM_EPS = 1e-6
SWA_Q_HEADS, SWA_KV_HEADS, SWA_HEAD_DIM, SWA_WINDOW = 16, 4, 64, 128
DN_HEADS, DN_HEAD_DIM, DN_CONV, DN_CHUNK = 8, 128, 4, 64
MLA_HEADS, MLA_NOPE, MLA_ROPE, MLA_V = 8, 128, 64, 128
ROPE_THETA = 10000.0
N_EXPERTS, TOP_K = 8, 2

LANES = 128
VMEM_LIMIT_BYTES = 56 * 1024 * 1024


def _cparams(*semantics):
    return pltpu.CompilerParams(dimension_semantics=semantics, vmem_limit_bytes=VMEM_LIMIT_BYTES)


def _tile(n, pref):
    t = min(n, pref)
    while n % t:
        t //= 2
    return t


def _dot(a, b):
    return jnp.dot(a, b, preferred_element_type=F32)


def _dot_nt(a, b):
    return lax.dot_general(a, b, (((1,), (1,)), ((), ())), preferred_element_type=F32)


def _dot_tn(a, b):
    return lax.dot_general(a, b, (((0,), (0,)), ((), ())), preferred_element_type=F32)


def _rms(x, gain):
    inv = lax.rsqrt(jnp.mean(x * x, axis=-1, keepdims=True) + NORM_EPS)
    return x * inv * gain


def _sigmoid(x):
    return 1.0 / (1.0 + jnp.exp(-x))


def _lane_repeat(x, n):
    return jnp.concatenate([x] * n, axis=1)


def _rmsnorm_kernel(x_ref, g_ref, o_ref):
    o_ref[...] = _rms(x_ref[...], g_ref[...]).astype(o_ref.dtype)


def rmsnorm_rows(x, gains, layer, out_dtype):
    T, D = x.shape
    tm = _tile(T, 512)
    return pl.pallas_call(
        _rmsnorm_kernel,
        grid=(T // tm,),
        in_specs=[pl.BlockSpec((tm, D), lambda i: (i, 0)),
                  pl.BlockSpec((None, 1, D), lambda i: (layer, 0, 0))],
        out_specs=pl.BlockSpec((tm, D), lambda i: (i, 0)),
        out_shape=jax.ShapeDtypeStruct((T, D), out_dtype),
        compiler_params=_cparams("parallel"),
        name="rmsnorm_rows",
    )(x, gains)


def _matmul_ws_kernel(x_ref, wt_ref, o_ref, wb_ref, *, act):
    @pl.when(pl.program_id(1) == 0)
    def _():
        wb_ref[...] = wt_ref[...].T.astype(BF16)

    acc = _dot(x_ref[...], wb_ref[...])
    if act == "sigmoid":
        acc = _sigmoid(acc)
    o_ref[...] = acc.astype(o_ref.dtype)


def matmul_ws(x, w_t, layer, *, n_cols, row_block_offset, tn, out_dtype, act=None, tm_pref=1024):
    M, K = x.shape
    tm = _tile(M, tm_pref)
    assert n_cols % tn == 0
    return pl.pallas_call(
        functools.partial(_matmul_ws_kernel, act=act),
        grid=(n_cols // tn, M // tm),
        in_specs=[pl.BlockSpec((tm, K), lambda j, i: (i, 0)),
                  pl.BlockSpec((None, tn, K), lambda j, i: (layer, j + row_block_offset, 0))],
        out_specs=pl.BlockSpec((tm, tn), lambda j, i: (i, j)),
        out_shape=jax.ShapeDtypeStruct((M, n_cols), out_dtype),
        scratch_shapes=[pltpu.VMEM((K, tn), BF16)],
        compiler_params=_cparams("arbitrary", "arbitrary"),
        name="matmul_ws",
    )(x, w_t)


def _swa_kernel(sinks_ref, q_ref, kc_ref, vc_ref, kp_ref, vp_ref, o_ref, *, layer, tiles_per_seq, rows):
    L = SWA_WINDOW
    dh = SWA_HEAD_DIM
    G = SWA_Q_HEADS // SWA_KV_HEADS
    first = (pl.program_id(0) % tiles_per_seq) == 0
    kall = jnp.concatenate([kp_ref[...], kc_ref[...]], axis=0)
    vall = jnp.concatenate([vp_ref[...], vc_ref[...]], axis=0)
    qi = lax.broadcasted_iota(jnp.int32, (L, 2 * L), 0)
    kj = lax.broadcasted_iota(jnp.int32, (L, 2 * L), 1)
    rel = qi + L - kj
    band = jnp.logical_and(rel >= 0, rel < SWA_WINDOW)
    kj_min = jnp.where(first, L, 0)
    scale = dh ** -0.5
    problems = [(b, h) for b in range(rows // L) for h in range(SWA_KV_HEADS)]
    scores = []
    for b, h in problems:
        qs = jnp.concatenate(
            [q_ref[b * L:(b + 1) * L, (h * G + g) * dh:(h * G + g + 1) * dh] for g in range(G)], axis=0)
        scores.append(_dot_nt(qs, kall[b * L:(b + 2) * L, h * dh:(h + 1) * dh]))
    probs, inv_denoms = [], []
    for (b, h), s in zip(problems, scores):
        valid = jnp.logical_and(band, kj >= kj_min) if b == 0 else band
        es, rs = [], []
        for g in range(G):
            sg = jnp.where(valid, s[g * L:(g + 1) * L] * scale, NEG_INF)
            sink = sinks_ref[layer, h * G + g]
            m = jnp.maximum(jnp.full((L, LANES), sink, F32), jnp.max(sg, axis=-1, keepdims=True))
            e = jnp.exp(sg - _lane_repeat(m, 2 * L // LANES))
            denom = jnp.sum(e, axis=-1, keepdims=True) + jnp.exp(sink - m[:, :1])
            es.append(e.astype(BF16))
            rs.append(1.0 / denom)
        probs.append(jnp.concatenate(es, axis=0))
        inv_denoms.append(rs)
    for (b, h), e, rs in zip(problems, probs, inv_denoms):
        o = _dot(e, vall[b * L:(b + 2) * L, h * dh:(h + 1) * dh])
        for g in range(G):
            hq = h * G + g
            o_ref[b * L:(b + 1) * L, hq * dh:(hq + 1) * dh] = (o[g * L:(g + 1) * L] * rs[g]).astype(o_ref.dtype)


def swa_attention(qkv, sinks, layer, seq_len):
    T = qkv.shape[0]
    L = SWA_WINDOW
    rows = _tile(seq_len, 512)
    wq = SWA_Q_HEADS * SWA_HEAD_DIM
    wkv = SWA_KV_HEADS * SWA_HEAD_DIM
    kcol, vcol = wq // wkv, wq // wkv + 1
    rpl = rows // L

    def prev_map(col):
        return lambda i: (jnp.maximum(i * rpl - 1, 0), col)

    return pl.pallas_call(
        functools.partial(_swa_kernel, layer=layer, tiles_per_seq=seq_len // rows, rows=rows),
        grid=(T // rows,),
        in_specs=[pl.BlockSpec(memory_space=pltpu.SMEM),
                  pl.BlockSpec((rows, wq), lambda i: (i, 0)),
                  pl.BlockSpec((rows, wkv), lambda i: (i, kcol)),
                  pl.BlockSpec((rows, wkv), lambda i: (i, vcol)),
                  pl.BlockSpec((L, wkv), prev_map(kcol)),
                  pl.BlockSpec((L, wkv), prev_map(vcol))],
        out_specs=pl.BlockSpec((rows, wq), lambda i: (i, 0)),
        out_shape=jax.ShapeDtypeStruct((T, wq), BF16),
        compiler_params=_cparams("parallel"),
        name="swa_attention",
    )(sinks, qkv, qkv, qkv, qkv, qkv)


def _dn_prep_kernel(cur_ref, prev_ref, cw_ref, t2_ref, alog_ref, dtb_ref, qkv_ref, gb_ref, *, tiles_per_seq, tm):
    H, dh, W = DN_HEADS, DN_HEAD_DIM, DN_CONV
    first = (pl.program_id(0) % tiles_per_seq) == 0
    prev = prev_ref[...].astype(F32)
    prev = jnp.where(first, 0.0, prev)
    xcat = jnp.concatenate([prev, cur_ref[...].astype(F32)], axis=0)
    P = prev.shape[0]
    cw = cw_ref[...]
    y = None
    for j in range(W):
        shift = W - 1 - j
        rolled = xcat if shift == 0 else pltpu.roll(xcat, shift, axis=0)
        term = rolled[P:P + tm] * cw[j:j + 1]
        y = term if y is None else y + term
    y = y * _sigmoid(y)
    width = H * dh
    for h in range(H):
        qh = y[:, h * dh:(h + 1) * dh]
        kh = y[:, width + h * dh:width + (h + 1) * dh]
        qn = qh * lax.rsqrt(jnp.sum(qh * qh, axis=-1, keepdims=True) + 1e-6) * (dh ** -0.5)
        kn = kh * lax.rsqrt(jnp.sum(kh * kh, axis=-1, keepdims=True) + 1e-6)
        qkv_ref[:, h * dh:(h + 1) * dh] = qn.astype(qkv_ref.dtype)
        qkv_ref[:, width + h * dh:width + (h + 1) * dh] = kn.astype(qkv_ref.dtype)
    qkv_ref[:, 2 * width:] = y[:, 2 * width:].astype(qkv_ref.dtype)

    t2 = t2_ref[...]
    xs = t2 + dtb_ref[...]
    softplus = jnp.maximum(xs, 0.0) + jnp.log(1.0 + jnp.exp(-jnp.abs(xs)))
    g = -jnp.exp(alog_ref[...]) * softplus
    ri = lax.broadcasted_iota(jnp.int32, (tm, tm), 0)
    ci = lax.broadcasted_iota(jnp.int32, (tm, tm), 1)
    same_chunk = (ri // DN_CHUNK) == (ci // DN_CHUNK)
    tri = jnp.where(jnp.logical_and(same_chunk, ri >= ci), 1.0, 0.0)
    gc = jnp.dot(tri, g, preferred_element_type=F32, precision=lax.Precision.HIGHEST)
    lane = lax.broadcasted_iota(jnp.int32, t2.shape, 1)
    gb_ref[...] = jnp.where(lane < H, _sigmoid(t2), gc)


def dn_prep(proj_b, proj_small, conv_w, alog_vec, dtb_vec, layer, seq_len):
    T = proj_b.shape[0]
    width3 = 3 * DN_HEADS * DN_HEAD_DIM
    tm = _tile(seq_len, 256)
    P = 16
    tail2_col = proj_small.shape[1] // LANES - 1
    return pl.pallas_call(
        functools.partial(_dn_prep_kernel, tiles_per_seq=seq_len // tm, tm=tm),
        grid=(T // tm,),
        in_specs=[pl.BlockSpec((tm, width3), lambda i: (i, 0)),
                  pl.BlockSpec((P, width3), lambda i: (jnp.maximum(i * (tm // P) - 1, 0), 0)),
                  pl.BlockSpec((None, DN_CONV, width3), lambda i: (layer, 0, 0)),
                  pl.BlockSpec((tm, LANES), lambda i: (i, tail2_col)),
                  pl.BlockSpec((None, 1, LANES), lambda i: (layer, 0, 0)),
                  pl.BlockSpec((None, 1, LANES), lambda i: (layer, 0, 0))],
        out_specs=[pl.BlockSpec((tm, width3), lambda i: (i, 0)),
                   pl.BlockSpec((tm, LANES), lambda i: (i, 0))],
        out_shape=[jax.ShapeDtypeStruct((T, width3), BF16),
                   jax.ShapeDtypeStruct((T, LANES), F32)],
        compiler_params=_cparams("parallel"),
        name="dn_prep",
    )(proj_b, proj_b, conv_w, proj_small, alog_vec, dtb_vec)


def _unit_lower_inverse_many(ls):
    C = ls[0].shape[0]
    assert C == 64
    ri = lax.broadcasted_iota(jnp.int32, (C, C), 0)
    ci = lax.broadcasted_iota(jnp.int32, (C, C), 1)
    eye = jnp.where(ri == ci, 1.0, 0.0)

    def same_block(size):
        return (ri // size) == (ci // size)

    def mm(a, b):
        return [_dot(x.astype(BF16), y.astype(BF16)) for x, y in zip(a, b)]

    d = [jnp.where(same_block(8), l, 0.0) for l in ls]
    d2 = mm(d, d)
    d4 = mm(d2, d2)
    d3 = mm(d, d2)
    x1 = [b - a - c for a, b, c in zip(d, d2, d3)]
    x1d4 = mm(x1, d4)
    t = [eye + a + b + c for a, b, c in zip(x1, d4, x1d4)]
    for size in (8, 16, 32):
        joins = jnp.logical_and(same_block(2 * size), jnp.logical_not(same_block(size)))
        b = [jnp.where(joins, l, 0.0) for l in ls]
        tbt = mm(mm(t, b), t)
        t = [a - c for a, c in zip(t, tbt)]
    return t


def _dn_chunk_kernel(qkv_ref, z_ref, gb_ref, grow_ref, norm_ref, o_ref,
                     state_ref, u_ref, w_ref, qe_ref, kd_ref, qk_ref, *, rows):
    H, dh, C = DN_HEADS, DN_HEAD_DIM, DN_CHUNK
    width = H * dh
    heads = range(H)

    @pl.when(pl.program_id(1) == 0)
    def _():
        state_ref[...] = jnp.zeros_like(state_ref)

    ri = lax.broadcasted_iota(jnp.int32, (C, C), 0)
    ci = lax.broadcasted_iota(jnp.int32, (C, C), 1)
    incl = ri >= ci
    strict = ri > ci
    gain = norm_ref[...]

    def prepare(c2, carry):
        r0s = [pl.multiple_of((par * c2 + j) * C, C) for j in range(par)]
        gbs = [gb_ref[pl.ds(r0, C), :] for r0 in r0s]
        grows = [grow_ref[par * c2 + j] for j in range(par)]
        probs = [(j, h) for j in range(par) for h in heads]
        q = [qkv_ref[pl.ds(r0s[j], C), h * dh:(h + 1) * dh] for j, h in probs]
        k = [qkv_ref[pl.ds(r0s[j], C), width + h * dh:width + (h + 1) * dh] for j, h in probs]
        v = [qkv_ref[pl.ds(r0s[j], C), 2 * width + h * dh:2 * width + (h + 1) * dh] for j, h in probs]
        beta = [gbs[j][:, h:h + 1] for j, h in probs]
        gcol = [gbs[j][:, H + h:H + h + 1] for j, h in probs]
        np_ = range(len(probs))
        decay = [jnp.exp(jnp.where(incl, gcol[p] - grows[j][h:h + 1, :], NEG_INF)) for p, (j, h) in enumerate(probs)]
        kf = [k[p].astype(F32) for p in np_]
        kbeta = [kf[p] * beta[p] for p in np_]
        eg = [jnp.exp(gcol[p]) for p in np_]
        kk = [_dot_nt(kbeta[p].astype(BF16), k[p]) for p in np_]
        qk = [_dot_nt(q[p], k[p]) for p in np_]
        tmat = _unit_lower_inverse_many([jnp.where(strict, kk[p] * decay[p], 0.0) for p in np_])
        tb = [t.astype(BF16) for t in tmat]
        u = [_dot(tb[p], (v[p].astype(F32) * beta[p]).astype(BF16)) for p in np_]
        w = [_dot(tb[p], (kbeta[p] * eg[p]).astype(BF16)) for p in np_]
        for p, (j, h) in enumerate(probs):
            r0 = r0s[j]
            g_last = gcol[p][C - 1:C, :]
            u_ref[h, pl.ds(r0, C), :] = u[p]
            w_ref[h, pl.ds(r0, C), :] = w[p].astype(BF16)
            qe_ref[h, pl.ds(r0, C), :] = (q[p].astype(F32) * eg[p]).astype(BF16)
            kd_ref[h, pl.ds(r0, C), :] = (kf[p] * jnp.exp(g_last - gcol[p])).astype(BF16)
            qk_ref[h, pl.ds(r0, C), :] = jnp.where(incl, qk[p] * decay[p], 0.0).astype(BF16)
        return carry

    par = 4 if (rows // C) % 4 == 0 else 1
    lax.fori_loop(0, rows // (C * par), prepare, 0)

    def scan(c, carry):
        r0 = pl.multiple_of(c * C, C)
        gb = gb_ref[pl.ds(r0, C), :]
        state = [state_ref[h] for h in heads]
        sb = [s.astype(BF16) for s in state]
        w_s = [_dot(w_ref[h, pl.ds(r0, C), :], sb[h]) for h in heads]
        q_s = [_dot(qe_ref[h, pl.ds(r0, C), :], sb[h]) for h in heads]
        vnb = [(u_ref[h, pl.ds(r0, C), :] - w_s[h]).astype(BF16) for h in heads]
        o_in = [_dot(qk_ref[h, pl.ds(r0, C), :], vnb[h]) for h in heads]
        kv = [_dot_tn(kd_ref[h, pl.ds(r0, C), :], vnb[h]) for h in heads]
        for h in heads:
            g_last = gb[C - 1:C, H + h:H + h + 1]
            state_ref[h] = state[h] * jnp.exp(g_last) + kv[h]
            z = z_ref[pl.ds(r0, C), h * dh:(h + 1) * dh].astype(F32)
            y = _rms(q_s[h] + o_in[h], gain) * (z * _sigmoid(z))
            o_ref[pl.ds(r0, C), h * dh:(h + 1) * dh] = y.astype(o_ref.dtype)
        return carry

    lax.fori_loop(0, rows // C, scan, 0, unroll=2)


def dn_chunked(qkvn, proj_b, gb, grow, dn_norm, layer, batch, seq_len):
    T = qkvn.shape[0]
    H, dh, C = DN_HEADS, DN_HEAD_DIM, DN_CHUNK
    width = H * dh
    rows = _tile(seq_len, 512)
    tps = seq_len // rows
    zcol = 3
    return pl.pallas_call(
        functools.partial(_dn_chunk_kernel, rows=rows),
        grid=(batch, tps),
        in_specs=[pl.BlockSpec((rows, 3 * width), lambda b, n: (b * tps + n, 0)),
                  pl.BlockSpec((rows, width), lambda b, n: (b * tps + n, zcol)),
                  pl.BlockSpec((rows, LANES), lambda b, n: (b * tps + n, 0)),
                  pl.BlockSpec((rows // C, H, C), lambda b, n: (b * tps + n, 0, 0)),
                  pl.BlockSpec((None, 1, dh), lambda b, n: (layer, 0, 0))],
        out_specs=pl.BlockSpec((rows, width), lambda b, n: (b * tps + n, 0)),
        out_shape=jax.ShapeDtypeStruct((T, width), BF16),
        scratch_shapes=[pltpu.VMEM((H, dh, dh), F32),
                        pltpu.VMEM((H, rows, dh), F32),
                        pltpu.VMEM((H, rows, dh), BF16),
                        pltpu.VMEM((H, rows, dh), BF16),
                        pltpu.VMEM((H, rows, dh), BF16),
                        pltpu.VMEM((H, rows, C), BF16)],
        compiler_params=_cparams("parallel", "arbitrary"),
        name="dn_chunked",
    )(qkvn, proj_b, gb, grow, dn_norm)


def _rotate_pairs(x, cos_tab, sin_tab):
    return x * cos_tab + pltpu.roll(x, LANES // 2, axis=1) * sin_tab


def _mla_q_kernel(c_ref, g_ref, w_ref, cos_ref, sin_ref, o_ref):
    H = MLA_HEADS
    scale = (MLA_NOPE + MLA_ROPE) ** -0.5 * float(np.log2(np.e))
    cn = _rms(c_ref[...], g_ref[...]).astype(BF16)
    acc = _dot(cn, w_ref[...]) * scale
    cos_tab, sin_tab = cos_ref[...], sin_ref[...]
    for h in range(H):
        o_ref[:, 2 * h * LANES:(2 * h + 1) * LANES] = acc[:, h * LANES:(h + 1) * LANES].astype(o_ref.dtype)
        xr = acc[:, (H + h) * LANES:(H + h + 1) * LANES]
        o_ref[:, (2 * h + 1) * LANES:(2 * h + 2) * LANES] = _rotate_pairs(xr, cos_tab, sin_tab).astype(o_ref.dtype)


def mla_q(proj_small, gains, wq, cos_tab, sin_tab, layer):
    T = proj_small.shape[0]
    R = wq.shape[1]
    tm = _tile(T, 512)
    N = wq.shape[2]
    return pl.pallas_call(
        _mla_q_kernel,
        grid=(T // tm,),
        in_specs=[pl.BlockSpec((tm, R), lambda i: (i, 0)),
                  pl.BlockSpec((None, 1, R), lambda i: (layer, 0, 0)),
                  pl.BlockSpec((None, R, N), lambda i: (layer, 0, 0)),
                  pl.BlockSpec((tm, LANES), lambda i: (i, 0)),
                  pl.BlockSpec((tm, LANES), lambda i: (i, 0))],
        out_specs=pl.BlockSpec((tm, N), lambda i: (i, 0)),
        out_shape=jax.ShapeDtypeStruct((T, N), BF16),
        compiler_params=_cparams("parallel"),
        name="mla_q",
    )(proj_small, gains, wq, cos_tab, sin_tab)


def _mla_kv_kernel(c_ref, kr_ref, g_ref, w_ref, cos_ref, sin_ref, k_ref, v_ref):
    H = MLA_HEADS
    cn = _rms(c_ref[...], g_ref[...]).astype(BF16)
    acc = _dot(cn, w_ref[...])
    kr = _rotate_pairs(kr_ref[...], cos_ref[...], sin_ref[...]).astype(k_ref.dtype)
    for h in range(H):
        k_ref[:, 2 * h * LANES:(2 * h + 1) * LANES] = acc[:, h * LANES:(h + 1) * LANES].astype(k_ref.dtype)
        k_ref[:, (2 * h + 1) * LANES:(2 * h + 2) * LANES] = kr
    v_ref[...] = acc[:, H * LANES:].astype(v_ref.dtype)


def mla_kv(proj_small, gains, wkv, cos_tab, sin_tab, layer):
    T = proj_small.shape[0]
    R = wkv.shape[1]
    tm = _tile(T, 512)
    N = wkv.shape[2]
    H = MLA_HEADS
    kr_col = (2 * R) // LANES
    return pl.pallas_call(
        _mla_kv_kernel,
        grid=(T // tm,),
        in_specs=[pl.BlockSpec((tm, R), lambda i: (i, 1)),
                  pl.BlockSpec((tm, LANES), lambda i: (i, kr_col)),
                  pl.BlockSpec((None, 1, R), lambda i: (layer, 0, 0)),
                  pl.BlockSpec((None, R, N), lambda i: (layer, 0, 0)),
                  pl.BlockSpec((tm, LANES), lambda i: (i, 0)),
                  pl.BlockSpec((tm, LANES), lambda i: (i, 0))],
        out_specs=[pl.BlockSpec((tm, 2 * H * LANES), lambda i: (i, 0)),
                   pl.BlockSpec((tm, H * MLA_V), lambda i: (i, 0))],
        out_shape=[jax.ShapeDtypeStruct((T, 2 * H * LANES), BF16),
                   jax.ShapeDtypeStruct((T, H * MLA_V), BF16)],
        compiler_params=_cparams("parallel"),
        name="mla_kv",
    )(proj_small, proj_small, gains, wkv, cos_tab, sin_tab)


def _flash_kernel(qi_ref, kj_ref, q_ref, k_ref, v_ref, o_ref, m_ref, acc_ref, *, tq, tk):
    H, dv = MLA_HEADS, MLA_V
    dqk = 2 * LANES
    qi = qi_ref[pl.program_id(1)]
    kj = kj_ref[pl.program_id(1)]

    @pl.when(kj == 0)
    def _():
        m_ref[...] = jnp.full_like(m_ref, NEG_INF)
        acc_ref[...] = jnp.zeros_like(acc_ref)

    def step(masked):
        if masked:
            row = lax.broadcasted_iota(jnp.int32, (tq, tk), 0)
            col = lax.broadcasted_iota(jnp.int32, (tq, tk), 1)
            keep = col <= row
        ones = jnp.ones((tk, dv), BF16)
        scores = [_dot_nt(q_ref[:, h * dqk:(h + 1) * dqk], k_ref[:, h * dqk:(h + 1) * dqk]) for h in range(H)]
        probs, alphas = [], []
        for h in range(H):
            s = scores[h]
            if masked:
                s = jnp.where(keep, s, NEG_INF)
            m_prev = m_ref[h]
            m_new = jnp.maximum(m_prev, jnp.max(s, axis=-1, keepdims=True))
            alpha = jnp.exp2(m_prev - m_new)
            p = jnp.exp2(s - _lane_repeat(m_new, tk // LANES))
            m_ref[h] = m_new
            alphas.append(alpha)
            probs.append(p.astype(BF16))
        for h in range(H):
            v_ext = jnp.concatenate([v_ref[:, h * dv:(h + 1) * dv], ones], axis=1)
            acc_ref[h] = _lane_repeat(alphas[h], 2) * acc_ref[h] + _dot(probs[h], v_ext)

    @pl.when(kj < qi)
    def _():
        step(False)

    @pl.when(kj == qi)
    def _():
        step(True)
        for h in range(H):
            acc = acc_ref[h]
            o_ref[:, h * dv:(h + 1) * dv] = (acc[:, :dv] / acc[:, dv:]).astype(o_ref.dtype)


def mla_flash(qf, kf, v, batch, seq_len):
    T = qf.shape[0]
    H, dv = MLA_HEADS, MLA_V
    t = _tile(seq_len, 512)
    nq = seq_len // t
    pairs = [(i, j) for i in range(nq) for j in range(i + 1)]
    qi_tab = jnp.asarray([i for i, _ in pairs], jnp.int32)
    kj_tab = jnp.asarray([j for _, j in pairs], jnp.int32)
    return pl.pallas_call(
        functools.partial(_flash_kernel, tq=t, tk=t),
        grid_spec=pltpu.PrefetchScalarGridSpec(
            num_scalar_prefetch=2,
            grid=(batch, len(pairs)),
            in_specs=[pl.BlockSpec((t, qf.shape[1]), lambda b, n, qi, kj: (b * nq + qi[n], 0)),
                      pl.BlockSpec((t, kf.shape[1]), lambda b, n, qi, kj: (b * nq + kj[n], 0)),
                      pl.BlockSpec((t, v.shape[1]), lambda b, n, qi, kj: (b * nq + kj[n], 0))],
            out_specs=pl.BlockSpec((t, H * dv), lambda b, n, qi, kj: (b * nq + qi[n], 0)),
            scratch_shapes=[pltpu.VMEM((H, t, LANES), F32),
                            pltpu.VMEM((H, t, 2 * dv), F32)]),
        out_shape=jax.ShapeDtypeStruct((T, H * dv), BF16),
        compiler_params=_cparams("parallel", "arbitrary"),
        name="mla_flash",
    )(qi_tab, kj_tab, qf, kf, v)


def _merge_kernel(ya_ref, yb_ref, yc_ref, ga_ref, gb_ref, gc_ref, wa_ref, wb_ref, wc_ref, o_ref, sa_ref, sb_ref, sc_ref):
    @pl.when(pl.program_id(1) == 0)
    def _():
        sa_ref[...] = wa_ref[...].astype(BF16)
        sb_ref[...] = wb_ref[...].astype(BF16)
        sc_ref[...] = wc_ref[...].astype(BF16)

    acc = ga_ref[...].astype(F32) * _dot(ya_ref[...], sa_ref[...])
    acc = acc + gb_ref[...].astype(F32) * _dot(yb_ref[...], sb_ref[...])
    acc = acc + gc_ref[...].astype(F32) * _dot(yc_ref[...], sc_ref[...])
    o_ref[...] = acc.astype(o_ref.dtype)


def merge_branches(ya, yb, yc, gates, w_branch, layer):
    T, Wb = ya.shape
    D = w_branch.shape[-1]
    tm = _tile(T, 1024)
    tn = _tile(D, 512)
    nj = D // tn
    y_spec = pl.BlockSpec((tm, Wb), lambda j, i: (i, 0))

    def g_spec(n):
        return pl.BlockSpec((tm, tn), lambda j, i: (i, n * nj + j))

    def w_spec(n):
        return pl.BlockSpec((None, None, Wb, tn), lambda j, i: (layer, n, 0, j))

    return pl.pallas_call(
        _merge_kernel,
        grid=(nj, T // tm),
        in_specs=[y_spec, y_spec, y_spec, g_spec(0), g_spec(1), g_spec(2), w_spec(0), w_spec(1), w_spec(2)],
        out_specs=pl.BlockSpec((tm, tn), lambda j, i: (i, j)),
        out_shape=jax.ShapeDtypeStruct((T, D), BF16),
        scratch_shapes=[pltpu.VMEM((Wb, tn), BF16)] * 3,
        compiler_params=_cparams("arbitrary", "arbitrary"),
        name="merge_branches",
    )(ya, yb, yc, gates, gates, gates, w_branch, w_branch, w_branch)


def _out_proj_kernel(m_ref, w_ref, x_ref, g_ref, xo_ref, *maybe_ho_ref):
    xn = x_ref[...] + _dot(m_ref[...], w_ref[...])
    xo_ref[...] = xn
    for ho_ref in maybe_ho_ref:
        ho_ref[...] = _rms(xn, g_ref[...]).astype(ho_ref.dtype)


def out_proj_residual(merged, w_out_bf, x, norm_gain, layer, emit_norm):
    T, D = x.shape
    tm = _tile(T, 512)
    row_spec = pl.BlockSpec((tm, D), lambda i: (i, 0))
    out_specs = [row_spec, row_spec] if emit_norm else [row_spec]
    out_shape = [jax.ShapeDtypeStruct((T, D), F32)] + ([jax.ShapeDtypeStruct((T, D), BF16)] if emit_norm else [])
    return pl.pallas_call(
        _out_proj_kernel,
        grid=(T // tm,),
        in_specs=[row_spec,
                  pl.BlockSpec((None, D, D), lambda i: (layer, 0, 0)),
                  row_spec,
                  pl.BlockSpec((None, 1, D), lambda i: (layer, 0, 0))],
        out_specs=out_specs,
        out_shape=out_shape,
        compiler_params=_cparams("parallel"),
        name="out_proj_residual",
    )(merged, w_out_bf, x, norm_gain)


def _ffn_kernel(h_ref, wg_ref, wu_ref, wd_ref, y_ref, acc_ref):
    f = pl.program_id(1)

    @pl.when(f == 0)
    def _():
        acc_ref[...] = jnp.zeros_like(acc_ref)

    h = h_ref[...]
    gate = _dot(h, wg_ref[...])
    up = _dot(h, wu_ref[...])
    act = (gate * _sigmoid(gate) * up).astype(BF16)
    acc_ref[...] += _dot(act, wd_ref[...])

    @pl.when(f == pl.num_programs(1) - 1)
    def _():
        y_ref[...] = acc_ref[...].astype(y_ref.dtype)


def ffn_swiglu(h, wg, wu, wd, widx):
    T, D = h.shape
    F = wg.shape[-1]
    tm = _tile(T, 512)
    tf = _tile(F, 1024)
    return pl.pallas_call(
        _ffn_kernel,
        grid=(T // tm, F // tf),
        in_specs=[pl.BlockSpec((tm, D), lambda i, f: (i, 0)),
                  pl.BlockSpec((None, D, tf), lambda i, f: (widx, 0, f)),
                  pl.BlockSpec((None, D, tf), lambda i, f: (widx, 0, f)),
                  pl.BlockSpec((None, tf, D), lambda i, f: (widx, f, 0))],
        out_specs=pl.BlockSpec((tm, D), lambda i, f: (i, 0)),
        out_shape=jax.ShapeDtypeStruct((T, D), BF16),
        scratch_shapes=[pltpu.VMEM((tm, D), F32)],
        compiler_params=_cparams("parallel", "arbitrary"),
        name="ffn_swiglu",
    )(h, wg, wu, wd)


def _router_kernel(x_ref, g_ref, w_ref, o_ref):
    h = _rms(x_ref[...], g_ref[...])
    logits = jnp.dot(h, w_ref[...], preferred_element_type=F32, precision=lax.Precision.HIGHEST)
    lane = lax.broadcasted_iota(jnp.int32, logits.shape, 1)
    lanef = lane.astype(F32)
    big = float(LANES)
    logits = jnp.where(lane < N_EXPERTS, logits, NEG_INF)
    l1 = jnp.max(logits, axis=-1, keepdims=True)
    i1 = jnp.min(jnp.where(logits == l1, lanef, big), axis=-1, keepdims=True)
    rest = jnp.where(lanef == i1, NEG_INF, logits)
    l2 = jnp.max(rest, axis=-1, keepdims=True)
    i2 = jnp.min(jnp.where(rest == l2, lanef, big), axis=-1, keepdims=True)
    e2 = jnp.exp(l2 - l1)
    w1 = 1.0 / (1.0 + e2)
    w2 = e2 / (1.0 + e2)
    out = jnp.where(lane == 0, i1, jnp.where(lane == 1, i2, jnp.where(lane == 2, w1, jnp.where(lane == 3, w2, 0.0))))
    o_ref[...] = out


def moe_router(x, norm_gain, w_router_pad, layer, widx):
    T, D = x.shape
    tm = _tile(T, 512)
    return pl.pallas_call(
        _router_kernel,
        grid=(T // tm,),
        in_specs=[pl.BlockSpec((tm, D), lambda i: (i, 0)),
                  pl.BlockSpec((None, 1, D), lambda i: (layer, 0, 0)),
                  pl.BlockSpec((None, D, LANES), lambda i: (widx, 0, 0))],
        out_specs=pl.BlockSpec((tm, LANES), lambda i: (i, 0)),
        out_shape=jax.ShapeDtypeStruct((T, LANES), F32),
        compiler_params=_cparams("parallel"),
        name="moe_router",
    )(x, norm_gain, w_router_pad)


def _start_row_gather(idx_hbm, src_hbm, blk, slot, idx_smem, rows_vmem, idx_sem, row_sem, n):
    idx_copy = pltpu.make_async_copy(idx_hbm.at[blk], idx_smem.at[slot], idx_sem.at[slot])
    idx_copy.start()
    idx_copy.wait()

    def issue(i, c):
        for priority in range(2):
            r = 2 * i + priority
            pltpu.make_async_copy(src_hbm.at[pl.ds(idx_smem[slot, r], 1)],
                                  rows_vmem.at[slot, pl.ds(r, 1)],
                                  row_sem.at[slot]).start(priority=priority)
        return c

    lax.fori_loop(0, n // 2, issue, 0, unroll=4)


def _wait_row_gather(src_hbm, slot, rows_vmem, row_sem, n):
    pltpu.make_async_copy(src_hbm.at[pl.ds(0, n)], rows_vmem.at[slot], row_sem.at[slot]).wait()


def _gather_rows_kernel(meta_ref, idx_hbm, x_hbm, g_ref, o_ref, idx_smem, rows_vmem, idx_sem, row_sem, *, bm):
    m = pl.program_id(0)
    n_used = meta_ref[pl.num_programs(0)]
    slot = m % 2
    start = functools.partial(_start_row_gather, idx_hbm, x_hbm, idx_smem=idx_smem, rows_vmem=rows_vmem,
                              idx_sem=idx_sem, row_sem=row_sem, n=bm)

    @pl.when(m == 0)
    def _():
        start(blk=0, slot=0)

    @pl.when(m + 1 < n_used)
    def _():
        start(blk=m + 1, slot=1 - slot)

    @pl.when(m < n_used)
    def _():
        _wait_row_gather(x_hbm, slot, rows_vmem, row_sem, bm)
        o_ref[...] = _rms(rows_vmem[slot], g_ref[...]).astype(o_ref.dtype)

    @pl.when(m >= n_used)
    def _():
        o_ref[...] = jnp.zeros_like(o_ref)


def moe_gather_norm(x, row_tok, meta, norm_gain, layer, bm):
    T, D = x.shape
    n_blocks = row_tok.shape[0]
    return pl.pallas_call(
        functools.partial(_gather_rows_kernel, bm=bm),
        grid_spec=pltpu.PrefetchScalarGridSpec(
            num_scalar_prefetch=1,
            grid=(n_blocks,),
            in_specs=[pl.BlockSpec(memory_space=pl.ANY),
                      pl.BlockSpec(memory_space=pl.ANY),
                      pl.BlockSpec((None, 1, D), lambda i, meta: (layer, 0, 0))],
            out_specs=pl.BlockSpec((bm, D), lambda i, meta: (i, 0)),
            scratch_shapes=[pltpu.SMEM((2, bm), jnp.int32),
                            pltpu.VMEM((2, bm, D), F32),
                            pltpu.SemaphoreType.DMA((2,)),
                            pltpu.SemaphoreType.DMA((2,))]),
        out_shape=jax.ShapeDtypeStruct((n_blocks * bm, D), BF16),
        compiler_params=_cparams("arbitrary"),
        name="moe_gather_norm",
    )(meta, row_tok, x, norm_gain)


def _expert_up_kernel(meta_ref, x_ref, wg_ref, wu_ref, o_ref, wgb_ref, wub_ref):
    m = pl.program_id(1)
    n_used = meta_ref[pl.num_programs(1)]
    new_expert = jnp.logical_or(m == 0, meta_ref[m] != meta_ref[jnp.maximum(m - 1, 0)])

    @pl.when(jnp.logical_and(new_expert, m < n_used))
    def _():
        wgb_ref[...] = wg_ref[...].astype(BF16)
        wub_ref[...] = wu_ref[...].astype(BF16)

    @pl.when(m < n_used)
    def _():
        x = x_ref[...]
        gate = _dot(x, wgb_ref[...])
        up = _dot(x, wub_ref[...])
        o_ref[...] = (gate * _sigmoid(gate) * up).astype(o_ref.dtype)

    @pl.when(m >= n_used)
    def _():
        o_ref[...] = jnp.zeros_like(o_ref)


def _last_used(m, meta, nb):
    return jnp.minimum(m, meta[nb] - 1)


def moe_expert_up(xs, meta, wg, wu, widx, bm):
    R, D = xs.shape
    F = wg.shape[-1]
    tf = _tile(F, 1024)
    nb = R // bm

    def w_map(f, m, meta):
        return (widx, meta[_last_used(m, meta, nb)], 0, f)

    return pl.pallas_call(
        _expert_up_kernel,
        grid_spec=pltpu.PrefetchScalarGridSpec(
            num_scalar_prefetch=1,
            grid=(F // tf, nb),
            in_specs=[pl.BlockSpec((bm, D), lambda f, m, meta: (_last_used(m, meta, nb), 0)),
                      pl.BlockSpec((None, None, D, tf), w_map),
                      pl.BlockSpec((None, None, D, tf), w_map)],
            out_specs=pl.BlockSpec((bm, tf), lambda f, m, meta: (m, f)),
            scratch_shapes=[pltpu.VMEM((D, tf), BF16), pltpu.VMEM((D, tf), BF16)]),
        out_shape=jax.ShapeDtypeStruct((R, F), BF16),
        compiler_params=_cparams("arbitrary", "arbitrary"),
        name="moe_expert_up",
    )(meta, xs, wg, wu)


def _expert_down_kernel(meta_ref, a_ref, wd_ref, o_ref, wdb_ref):
    m = pl.program_id(1)
    n_used = meta_ref[pl.num_programs(1)]
    new_expert = jnp.logical_or(m == 0, meta_ref[m] != meta_ref[jnp.maximum(m - 1, 0)])

    @pl.when(jnp.logical_and(new_expert, m < n_used))
    def _():
        wdb_ref[...] = wd_ref[...].astype(BF16)

    @pl.when(m < n_used)
    def _():
        o_ref[...] = _dot(a_ref[...], wdb_ref[...]).astype(o_ref.dtype)

    @pl.when(m >= n_used)
    def _():
        o_ref[...] = jnp.zeros_like(o_ref)


def moe_expert_down(act, meta, wd, widx, bm):
    R, F = act.shape
    D = wd.shape[-1]
    tn = _tile(D, 512)
    nb = R // bm
    return pl.pallas_call(
        _expert_down_kernel,
        grid_spec=pltpu.PrefetchScalarGridSpec(
            num_scalar_prefetch=1,
            grid=(D // tn, nb),
            in_specs=[pl.BlockSpec((bm, F), lambda j, m, meta: (_last_used(m, meta, nb), 0)),
                      pl.BlockSpec((None, None, F, tn),
                                   lambda j, m, meta: (widx, meta[_last_used(m, meta, nb)], 0, j))],
            out_specs=pl.BlockSpec((bm, tn), lambda j, m, meta: (m, j)),
            scratch_shapes=[pltpu.VMEM((F, tn), BF16)]),
        out_shape=jax.ShapeDtypeStruct((R, D), F32),
        compiler_params=_cparams("arbitrary", "arbitrary"),
        name="moe_expert_down",
    )(meta, act, wd)


def _combine_kernel(dest_hbm, yb_hbm, rt_ref, y_ref, idx_smem, rows_vmem, idx_sem, row_sem, *, tm):
    m = pl.program_id(0)
    slot = m % 2
    n = TOP_K * tm
    start = functools.partial(_start_row_gather, dest_hbm, yb_hbm, idx_smem=idx_smem, rows_vmem=rows_vmem,
                              idx_sem=idx_sem, row_sem=row_sem, n=n)

    @pl.when(m == 0)
    def _():
        start(blk=0, slot=0)

    @pl.when(m + 1 < pl.num_programs(0))
    def _():
        start(blk=m + 1, slot=1 - slot)

    _wait_row_gather(yb_hbm, slot, rows_vmem, row_sem, n)
    rt = rt_ref[...]
    y = rt[:, 2:3] * rows_vmem[slot, 0:tm, :] + rt[:, 3:4] * rows_vmem[slot, tm:2 * tm, :]
    y_ref[...] = y.astype(y_ref.dtype)


def moe_combine(yb, dest_blocks, route, tm):
    T = route.shape[0]
    D = yb.shape[1]
    return pl.pallas_call(
        functools.partial(_combine_kernel, tm=tm),
        grid=(T // tm,),
        in_specs=[pl.BlockSpec(memory_space=pl.ANY),
                  pl.BlockSpec(memory_space=pl.ANY),
                  pl.BlockSpec((tm, LANES), lambda i: (i, 0))],
        out_specs=pl.BlockSpec((tm, D), lambda i: (i, 0)),
        out_shape=jax.ShapeDtypeStruct((T, D), BF16),
        scratch_shapes=[pltpu.SMEM((2, TOP_K * tm), jnp.int32),
                        pltpu.VMEM((2, TOP_K * tm, D), F32),
                        pltpu.SemaphoreType.DMA((2,)),
                        pltpu.SemaphoreType.DMA((2,))],
        compiler_params=_cparams("arbitrary"),
        name="moe_combine",
    )(dest_blocks, yb, route)


def moe_layer(x, norm_ffn, w_router_pad, wg, wu, wd, layer, widx):
    T, D = x.shape
    E = N_EXPERTS
    bm = _tile(T, 512)
    route = moe_router(x, norm_ffn, w_router_pad, layer, widx)
    flat_e = route[:, :TOP_K].astype(jnp.int32).reshape(-1)
    onehot = (flat_e[:, None] == jnp.arange(E, dtype=jnp.int32)[None, :]).astype(jnp.int32)
    incl = jnp.cumsum(onehot, axis=0)
    counts = incl[-1]
    rank = jnp.sum((incl - onehot) * onehot, axis=1)
    padded = (counts + bm - 1) // bm * bm
    pad_end = jnp.cumsum(padded)
    pad_start = pad_end - padded
    dest = pad_start[flat_e] + rank
    n_rows = -(-(T * TOP_K + E * (bm - 1)) // bm) * bm
    n_blocks = n_rows // bm
    flat_tok = jnp.repeat(jnp.arange(T, dtype=jnp.int32), TOP_K)
    row_tok = jnp.zeros((n_rows,), jnp.int32).at[dest].set(flat_tok)
    block_e = jnp.minimum(
        jnp.sum(jnp.arange(n_blocks, dtype=jnp.int32)[:, None] * bm >= pad_end[None, :], axis=1), E - 1
    ).astype(jnp.int32)
    meta = jnp.concatenate([block_e, (pad_end[-1:] // bm).astype(jnp.int32)])

    xs = moe_gather_norm(x, row_tok.reshape(n_blocks, bm), meta, norm_ffn, layer, bm)
    act = moe_expert_up(xs, meta, wg, wu, widx, bm)
    yb = moe_expert_down(act, meta, wd, widx, bm)
    tmc = _tile(T, 256)
    dest_blocks = dest.reshape(T // tmc, tmc, TOP_K).transpose(0, 2, 1).reshape(T // tmc, TOP_K * tmc)
    return moe_combine(yb, dest_blocks, route, tmc)


def _ple_kernel(x_ref, y_ref, p_ref, wg_ref, wp_ref, gp_ref, gn_ref, xo_ref, ho_ref):
    x = x_ref[...] + y_ref[...].astype(F32)
    hp = _rms(x, gp_ref[...]).astype(BF16)
    gate = _sigmoid(_dot(hp, wg_ref[...]))
    emb = _dot(p_ref[...].astype(BF16), wp_ref[...])
    xn = x + emb * gate
    xo_ref[...] = xn
    ho_ref[...] = _rms(xn, gn_ref[...]).astype(ho_ref.dtype)


def ple_layer(x, y, p, w_gate_bf, w_proj_bf, ple_gain, next_gain, layer, gain_idx, out_dtype):
    T, D = x.shape
    Pd = p.shape[-1]
    tm = _tile(T, 512)
    row_spec = pl.BlockSpec((tm, D), lambda i: (i, 0))
    return pl.pallas_call(
        _ple_kernel,
        grid=(T // tm,),
        in_specs=[row_spec,
                  row_spec,
                  pl.BlockSpec((None, tm, Pd), lambda i: (layer, i, 0)),
                  pl.BlockSpec((None, D, D), lambda i: (layer, 0, 0)),
                  pl.BlockSpec((None, Pd, D), lambda i: (layer, 0, 0)),
                  pl.BlockSpec((None, 1, D), lambda i: (layer, 0, 0)),
                  pl.BlockSpec((None, 1, D), lambda i: (gain_idx, 0, 0))],
        out_specs=[row_spec, row_spec],
        out_shape=[jax.ShapeDtypeStruct((T, D), F32), jax.ShapeDtypeStruct((T, D), out_dtype)],
        compiler_params=_cparams("parallel"),
        name="ple_layer",
    )(x, y, p, w_gate_bf, w_proj_bf, ple_gain, next_gain)


def _split_points(D):
    swa_q = SWA_Q_HEADS * SWA_HEAD_DIM
    swa_kv = SWA_KV_HEADS * SWA_HEAD_DIM
    dn_w = DN_HEADS * DN_HEAD_DIM
    return swa_q + 2 * swa_kv, swa_q + 2 * swa_kv + 4 * dn_w


def _pack_small_in_proj(w_in_t, q_lora, kv_lora):
    _, b_end = _split_points(w_in_t.shape[2])
    H = DN_HEADS
    o = b_end
    beta = w_in_t[:, o:o + H]
    decay = w_in_t[:, o + H:o + 2 * H]
    o += 2 * H
    cq = w_in_t[:, o:o + q_lora]
    o += q_lora
    ckv = w_in_t[:, o:o + kv_lora]
    o += kv_lora
    half = MLA_ROPE // 2
    kr1 = w_in_t[:, o:o + half]
    kr2 = w_in_t[:, o + half:o + 2 * half]
    o += MLA_ROPE
    pad = jnp.zeros((w_in_t.shape[0], LANES - 2 * H, w_in_t.shape[2]), w_in_t.dtype)
    packed = jnp.concatenate([cq, ckv, kr1, kr2, kr2, kr1, beta, decay, pad], axis=1)
    return packed, o


def _pack_w_uq(w_uq):
    Ld, R, _ = w_uq.shape
    H, half = MLA_HEADS, MLA_ROPE // 2
    w = w_uq.reshape(Ld, R, H, MLA_NOPE + MLA_ROPE)
    nope = w[..., :MLA_NOPE].reshape(Ld, R, H * MLA_NOPE)
    r1 = w[..., MLA_NOPE:MLA_NOPE + half]
    r2 = w[..., MLA_NOPE + half:]
    rope = jnp.concatenate([r1, r2, r2, r1], axis=-1).reshape(Ld, R, H * LANES)
    return jnp.concatenate([nope, rope], axis=-1).astype(BF16)


def _pack_w_ukv(w_ukv):
    Ld, R, _ = w_ukv.shape
    H = MLA_HEADS
    w = w_ukv.reshape(Ld, R, H, MLA_NOPE + MLA_V)
    kn = w[..., :MLA_NOPE].reshape(Ld, R, H * MLA_NOPE)
    vv = w[..., MLA_NOPE:].reshape(Ld, R, H * MLA_V)
    return jnp.concatenate([kn, vv], axis=-1).astype(BF16)


def _rope_tables(positions):
    half = MLA_ROPE // 2
    inv_freq = ROPE_THETA ** (-jnp.arange(half, dtype=F32) / half)
    ang = positions.astype(F32).reshape(-1)[:, None] * inv_freq
    cos, sin = jnp.cos(ang), jnp.sin(ang)
    zeros = jnp.zeros_like(cos)
    return (jnp.concatenate([cos, cos, zeros, zeros], axis=-1),
            jnp.concatenate([-sin, sin, zeros, zeros], axis=-1))


def _lane_vec(v, offset):
    Ld, n = v.shape
    out = jnp.zeros((Ld, 1, LANES), F32)
    return out.at[:, 0, offset:offset + n].set(v.astype(F32))


def kernel(x, p, positions, norm_mix, w_in, conv_w, dn_a_log, dn_dt_bias, dn_norm, swa_sinks, mla_q_norm, w_uq,
           mla_kv_norm, w_ukv, w_branch, w_out, norm_ffn, w_ffn_gate, w_ffn_up, w_ffn_down, w_router, w_exp_gate,
           w_exp_up, w_exp_down, norm_ple, w_ple_gate, w_ple_proj, final_norm):
    B, S, D = x.shape
    T = B * S
    depth = w_in.shape[0]
    q_lora, kv_lora = w_uq.shape[1], w_ukv.shape[1]
    a_end, b_end = _split_points(D)

    w_in_t = jnp.swapaxes(w_in, 1, 2)
    w_small_t, gate_row0 = _pack_small_in_proj(w_in_t, q_lora, kv_lora)
    w_gates_t = w_in_t[:, gate_row0:]
    wq_packed = _pack_w_uq(w_uq)
    wkv_packed = _pack_w_ukv(w_ukv)
    w_out_bf = w_out.astype(BF16)
    w_ple_gate_bf = w_ple_gate.astype(BF16)
    w_ple_proj_bf = w_ple_proj.astype(BF16)
    wfg, wfu, wfd = w_ffn_gate.astype(BF16), w_ffn_up.astype(BF16), w_ffn_down.astype(BF16)
    weg, weu, wed = w_exp_gate, w_exp_up, w_exp_down
    w_router_pad = jnp.pad(w_router, ((0, 0), (0, 0), (0, LANES - w_router.shape[-1])))
    cos_tab, sin_tab = _rope_tables(positions)
    alog_vec = _lane_vec(dn_a_log, DN_HEADS)
    dtb_vec = _lane_vec(dn_dt_bias, DN_HEADS)
    row = lambda g: g.reshape(g.shape[0], 1, g.shape[-1])
    norm_mix3, norm_ffn3, norm_ple3 = row(norm_mix), row(norm_ffn), row(norm_ple)
    dn_norm3, mla_q_norm3, mla_kv_norm3 = row(dn_norm), row(mla_q_norm), row(mla_kv_norm)
    final3 = final_norm.reshape(1, 1, D)

    xf = x.reshape(T, D)
    pf = p.reshape(depth, T, p.shape[-1])
    h = rmsnorm_rows(xf, norm_mix3, 0, BF16)
    for i in range(depth):
        proj_a = matmul_ws(h, w_in_t, i, n_cols=a_end, row_block_offset=0, tn=a_end // 2, out_dtype=BF16,
                           tm_pref=2048)
        tn_b = 512
        proj_b = matmul_ws(h, w_in_t, i, n_cols=b_end - a_end, row_block_offset=a_end // tn_b, tn=tn_b,
                           out_dtype=BF16, tm_pref=2048)
        proj_small = matmul_ws(h, w_small_t, i, n_cols=w_small_t.shape[1], row_block_offset=0,
                               tn=w_small_t.shape[1], out_dtype=F32)
        gates = matmul_ws(h, w_gates_t, i, n_cols=w_gates_t.shape[1], row_block_offset=0, tn=1024,
                          out_dtype=BF16, act="sigmoid")

        y_a = swa_attention(proj_a, swa_sinks, i, S)

        qkvn, gb = dn_prep(proj_b, proj_small, conv_w, alog_vec, dtb_vec, i, S)
        nchunks = T // DN_CHUNK
        grow = gb[:, DN_HEADS:2 * DN_HEADS].reshape(nchunks, DN_CHUNK, DN_HEADS).transpose(0, 2, 1)
        y_b = dn_chunked(qkvn, proj_b, gb, grow, dn_norm3, i, B, S)

        qfull = mla_q(proj_small, mla_q_norm3, wq_packed, cos_tab, sin_tab, i)
        kfull, vfull = mla_kv(proj_small, mla_kv_norm3, wkv_packed, cos_tab, sin_tab, i)
        y_c = mla_flash(qfull, kfull, vfull, B, S)

        merged = merge_branches(y_a, y_b, y_c, gates, w_branch, i)
        j = i // 2
        if i % 2 == 0:
            xf, h2 = out_proj_residual(merged, w_out_bf, xf, norm_ffn3, i, True)
            y = ffn_swiglu(h2, wfg, wfu, wfd, j)
        else:
            (xf,) = out_proj_residual(merged, w_out_bf, xf, norm_ffn3, i, False)
            y = moe_layer(xf, norm_ffn3, w_router_pad, weg, weu, wed, i, j)

        if i + 1 < depth:
            xf, h = ple_layer(xf, y, pf, w_ple_gate_bf, w_ple_proj_bf, norm_ple3, norm_mix3, i, i + 1, BF16)
        else:
            xf, out = ple_layer(xf, y, pf, w_ple_gate_bf, w_ple_proj_bf, norm_ple3, final3, i, 0, F32)
    return out.reshape(B, S, D)
```

```python
import functools

import jax
import jax.numpy as jnp
import numpy as np
from jax import lax
from jax.experimental import pallas as pl
from jax.experimental.pallas import tpu as pltpu

BF16 = jnp.bfloat16
F32 = jnp.float32
NEG_INF = float("-inf")

NORM_EPS = 1e-6
SWA_Q_HEADS, SWA_KV_HEADS, SWA_HEAD_DIM, SWA_WINDOW = 16, 4, 64, 128
DN_HEADS, DN_HEAD_DIM, DN_CONV, DN_CHUNK = 8, 128, 4, 64
MLA_HEADS, MLA_NOPE, MLA_ROPE, MLA_V = 8, 128, 64, 128
ROPE_THETA = 10000.0
N_EXPERTS, TOP_K = 8, 2

LANES = 128
VMEM_LIMIT_BYTES = 56 * 1024 * 1024


def _cparams(*semantics):
    return pltpu.CompilerParams(dimension_semantics=semantics, vmem_limit_bytes=VMEM_LIMIT_BYTES)


def _tile(n, pref):
    t = min(n, pref)
    while n % t:
        t //= 2
    return t


def _dot(a, b):
    return jnp.dot(a, b, preferred_element_type=F32)


def _dot_nt(a, b):
    return lax.dot_general(a, b, (((1,), (1,)), ((), ())), preferred_element_type=F32)


def _dot_tn(a, b):
    return lax.dot_general(a, b, (((0,), (0,)), ((), ())), preferred_element_type=F32)


def _rms(x, gain):
    inv = lax.rsqrt(jnp.mean(x * x, axis=-1, keepdims=True) + NORM_EPS)
    return x * inv * gain


def _sigmoid(x):
    return 0.5 * jnp.tanh(0.5 * x) + 0.5


def _lane_repeat(x, n):
    return jnp.concatenate([x] * n, axis=1)


def _rmsnorm_kernel(x_ref, g_ref, o_ref):
    o_ref[...] = _rms(x_ref[...], g_ref[...]).astype(o_ref.dtype)


def rmsnorm_rows(x, gains, layer, out_dtype):
    T, D = x.shape
    tm = _tile(T, 512)
    return pl.pallas_call(
        _rmsnorm_kernel,
        grid=(T // tm,),
        in_specs=[pl.BlockSpec((tm, D), lambda i: (i, 0)),
                  pl.BlockSpec((None, 1, D), lambda i: (layer, 0, 0))],
        out_specs=pl.BlockSpec((tm, D), lambda i: (i, 0)),
        out_shape=jax.ShapeDtypeStruct((T, D), out_dtype),
        compiler_params=_cparams("parallel"),
        name="rmsnorm_rows",
    )(x, gains)


def _matmul_ws_kernel(x_ref, wt_ref, o_ref, wb_ref, *, act):
    @pl.when(pl.program_id(1) == 0)
    def _():
        wb_ref[...] = wt_ref[...].T.astype(BF16)

    acc = _dot(x_ref[...], wb_ref[...])
    if act == "sigmoid":
        acc = _sigmoid(acc)
    o_ref[...] = acc.astype(o_ref.dtype)


def matmul_ws(x, w_t, layer, *, n_cols, row_block_offset, tn, out_dtype, act=None, tm_pref=1024):
    M, K = x.shape
    tm = _tile(M, tm_pref)
    assert n_cols % tn == 0
    return pl.pallas_call(
        functools.partial(_matmul_ws_kernel, act=act),
        grid=(n_cols // tn, M // tm),
        in_specs=[pl.BlockSpec((tm, K), lambda j, i: (i, 0)),
                  pl.BlockSpec((None, tn, K), lambda j, i: (layer, j + row_block_offset, 0))],
        out_specs=pl.BlockSpec((tm, tn), lambda j, i: (i, j)),
        out_shape=jax.ShapeDtypeStruct((M, n_cols), out_dtype),
        scratch_shapes=[pltpu.VMEM((K, tn), BF16)],
        compiler_params=_cparams("arbitrary", "arbitrary"),
        name="matmul_ws",
    )(x, w_t)


def _swa_kernel(sinks_ref, q_ref, kc_ref, vc_ref, kp_ref, vp_ref, o_ref, *, layer, tiles_per_seq, rows):
    L = SWA_WINDOW
    dh = SWA_HEAD_DIM
    G = SWA_Q_HEADS // SWA_KV_HEADS
    first = (pl.program_id(0) % tiles_per_seq) == 0
    kall = jnp.concatenate([kp_ref[...], kc_ref[...]], axis=0)
    vall = jnp.concatenate([vp_ref[...], vc_ref[...]], axis=0)
    qi = lax.broadcasted_iota(jnp.int32, (L, 2 * L), 0)
    kj = lax.broadcasted_iota(jnp.int32, (L, 2 * L), 1)
    rel = qi + L - kj
    band = jnp.logical_and(rel >= 0, rel < SWA_WINDOW)
    kj_min = jnp.where(first, L, 0)
    scale = dh ** -0.5
    problems = [(b, h) for b in range(rows // L) for h in range(SWA_KV_HEADS)]
    scores = []
    for b, h in problems:
        qs = jnp.concatenate(
            [q_ref[b * L:(b + 1) * L, (h * G + g) * dh:(h * G + g + 1) * dh] for g in range(G)], axis=0)
        scores.append(_dot_nt(qs, kall[b * L:(b + 2) * L, h * dh:(h + 1) * dh]))
    probs, inv_denoms = [], []
    for (b, h), s in zip(problems, scores):
        valid = jnp.logical_and(band, kj >= kj_min) if b == 0 else band
        es, rs = [], []
        for g in range(G):
            sg = jnp.where(valid, s[g * L:(g + 1) * L] * scale, NEG_INF)
            sink = sinks_ref[layer, h * G + g]
            m = jnp.maximum(jnp.full((L, LANES), sink, F32), jnp.max(sg, axis=-1, keepdims=True))
            e = jnp.exp(sg - _lane_repeat(m, 2 * L // LANES))
            denom = jnp.sum(e, axis=-1, keepdims=True) + jnp.exp(sink - m[:, :1])
            es.append(e.astype(BF16))
            rs.append(1.0 / denom)
        probs.append(jnp.concatenate(es, axis=0))
        inv_denoms.append(rs)
    for (b, h), e, rs in zip(problems, probs, inv_denoms):
        o = _dot(e, vall[b * L:(b + 2) * L, h * dh:(h + 1) * dh])
        for g in range(G):
            hq = h * G + g
            o_ref[b * L:(b + 1) * L, hq * dh:(hq + 1) * dh] = (o[g * L:(g + 1) * L] * rs[g]).astype(o_ref.dtype)


def swa_attention(qkv, sinks, layer, seq_len):
    T = qkv.shape[0]
    L = SWA_WINDOW
    rows = _tile(seq_len, 512)
    wq = SWA_Q_HEADS * SWA_HEAD_DIM
    wkv = SWA_KV_HEADS * SWA_HEAD_DIM
    kcol, vcol = wq // wkv, wq // wkv + 1
    rpl = rows // L

    def prev_map(col):
        return lambda i: (jnp.maximum(i * rpl - 1, 0), col)

    return pl.pallas_call(
        functools.partial(_swa_kernel, layer=layer, tiles_per_seq=seq_len // rows, rows=rows),
        grid=(T // rows,),
        in_specs=[pl.BlockSpec(memory_space=pltpu.SMEM),
                  pl.BlockSpec((rows, wq), lambda i: (i, 0)),
                  pl.BlockSpec((rows, wkv), lambda i: (i, kcol)),
                  pl.BlockSpec((rows, wkv), lambda i: (i, vcol)),
                  pl.BlockSpec((L, wkv), prev_map(kcol)),
                  pl.BlockSpec((L, wkv), prev_map(vcol))],
        out_specs=pl.BlockSpec((rows, wq), lambda i: (i, 0)),
        out_shape=jax.ShapeDtypeStruct((T, wq), BF16),
        compiler_params=_cparams("parallel"),
        name="swa_attention",
    )(sinks, qkv, qkv, qkv, qkv, qkv)


def _dn_prep_kernel(cur_ref, prev_ref, cw_ref, t2_ref, alog_ref, dtb_ref, qkv_ref, gb_ref, *, tiles_per_seq, tm):
    H, dh, W = DN_HEADS, DN_HEAD_DIM, DN_CONV
    first = (pl.program_id(0) % tiles_per_seq) == 0
    prev = prev_ref[...].astype(F32)
    prev = jnp.where(first, 0.0, prev)
    xcat = jnp.concatenate([prev, cur_ref[...].astype(F32)], axis=0)
    P = prev.shape[0]
    cw = cw_ref[...]
    y = None
    for j in range(W):
        shift = W - 1 - j
        rolled = xcat if shift == 0 else pltpu.roll(xcat, shift, axis=0)
        term = rolled[P:P + tm] * cw[j:j + 1]
        y = term if y is None else y + term
    y = y * _sigmoid(y)
    width = H * dh
    for h in range(H):
        qh = y[:, h * dh:(h + 1) * dh]
        kh = y[:, width + h * dh:width + (h + 1) * dh]
        qn = qh * lax.rsqrt(jnp.sum(qh * qh, axis=-1, keepdims=True) + 1e-6) * (dh ** -0.5)
        kn = kh * lax.rsqrt(jnp.sum(kh * kh, axis=-1, keepdims=True) + 1e-6)
        qkv_ref[:, h * dh:(h + 1) * dh] = qn.astype(qkv_ref.dtype)
        qkv_ref[:, width + h * dh:width + (h + 1) * dh] = kn.astype(qkv_ref.dtype)
    qkv_ref[:, 2 * width:] = y[:, 2 * width:].astype(qkv_ref.dtype)

    t2 = t2_ref[...]
    xs = t2 + dtb_ref[...]
    softplus = jnp.maximum(xs, 0.0) + jnp.log(1.0 + jnp.exp(-jnp.abs(xs)))
    g = -jnp.exp(alog_ref[...]) * softplus
    ri = lax.broadcasted_iota(jnp.int32, (tm, tm), 0)
    ci = lax.broadcasted_iota(jnp.int32, (tm, tm), 1)
    same_chunk = (ri // DN_CHUNK) == (ci // DN_CHUNK)
    tri = jnp.where(jnp.logical_and(same_chunk, ri >= ci), 1.0, 0.0)
    gc = jnp.dot(tri, g, preferred_element_type=F32, precision=lax.Precision.HIGHEST)
    lane = lax.broadcasted_iota(jnp.int32, t2.shape, 1)
    gb_ref[...] = jnp.where(lane < H, _sigmoid(t2), gc)


def dn_prep(proj_b, proj_small, conv_w, alog_vec, dtb_vec, layer, seq_len):
    T = proj_b.shape[0]
    width3 = 3 * DN_HEADS * DN_HEAD_DIM
    tm = _tile(seq_len, 256)
    P = 16
    tail2_col = proj_small.shape[1] // LANES - 1
    return pl.pallas_call(
        functools.partial(_dn_prep_kernel, tiles_per_seq=seq_len // tm, tm=tm),
        grid=(T // tm,),
        in_specs=[pl.BlockSpec((tm, width3), lambda i: (i, 0)),
                  pl.BlockSpec((P, width3), lambda i: (jnp.maximum(i * (tm // P) - 1, 0), 0)),
                  pl.BlockSpec((None, DN_CONV, width3), lambda i: (layer, 0, 0)),
                  pl.BlockSpec((tm, LANES), lambda i: (i, tail2_col)),
                  pl.BlockSpec((None, 1, LANES), lambda i: (layer, 0, 0)),
                  pl.BlockSpec((None, 1, LANES), lambda i: (layer, 0, 0))],
        out_specs=[pl.BlockSpec((tm, width3), lambda i: (i, 0)),
                   pl.BlockSpec((tm, LANES), lambda i: (i, 0))],
        out_shape=[jax.ShapeDtypeStruct((T, width3), BF16),
                   jax.ShapeDtypeStruct((T, LANES), F32)],
        compiler_params=_cparams("parallel"),
        name="dn_prep",
    )(proj_b, proj_b, conv_w, proj_small, alog_vec, dtb_vec)


def _unit_lower_inverse_many(ls):
    C = ls[0].shape[0]
    assert C == 64
    ri = lax.broadcasted_iota(jnp.int32, (C, C), 0)
    ci = lax.broadcasted_iota(jnp.int32, (C, C), 1)
    eye = jnp.where(ri == ci, 1.0, 0.0)

    def same_block(size):
        return (ri // size) == (ci // size)

    def mm(a, b):
        return [_dot(x.astype(BF16), y.astype(BF16)) for x, y in zip(a, b)]

    d = [jnp.where(same_block(8), l, 0.0) for l in ls]
    d2 = mm(d, d)
    d4 = mm(d2, d2)
    d3 = mm(d, d2)
    x1 = [b - a - c for a, b, c in zip(d, d2, d3)]
    x1d4 = mm(x1, d4)
    t = [eye + a + b + c for a, b, c in zip(x1, d4, x1d4)]
    for size in (8, 16, 32):
        joins = jnp.logical_and(same_block(2 * size), jnp.logical_not(same_block(size)))
        b = [jnp.where(joins, l, 0.0) for l in ls]
        tbt = mm(mm(t, b), t)
        t = [a - c for a, c in zip(t, tbt)]
    return t


def _dn_chunk_kernel(qkv_ref, z_ref, gb_ref, grow_ref, norm_ref, o_ref,
                     state_ref, u_ref, w_ref, qe_ref, kd_ref, qk_ref, *, rows):
    H, dh, C = DN_HEADS, DN_HEAD_DIM, DN_CHUNK
    width = H * dh
    heads = range(H)

    @pl.when(pl.program_id(1) == 0)
    def _():
        state_ref[...] = jnp.zeros_like(state_ref)

    ri = lax.broadcasted_iota(jnp.int32, (C, C), 0)
    ci = lax.broadcasted_iota(jnp.int32, (C, C), 1)
    incl = ri >= ci
    strict = ri > ci
    gain = norm_ref[...]

    def prepare(c2, carry):
        r0s = [pl.multiple_of((par * c2 + j) * C, C) for j in range(par)]
        gbs = [gb_ref[pl.ds(r0, C), :] for r0 in r0s]
        grows = [grow_ref[par * c2 + j] for j in range(par)]
        probs = [(j, h) for j in range(par) for h in heads]
        q = [qkv_ref[pl.ds(r0s[j], C), h * dh:(h + 1) * dh] for j, h in probs]
        k = [qkv_ref[pl.ds(r0s[j], C), width + h * dh:width + (h + 1) * dh] for j, h in probs]
        v = [qkv_ref[pl.ds(r0s[j], C), 2 * width + h * dh:2 * width + (h + 1) * dh] for j, h in probs]
        beta = [gbs[j][:, h:h + 1] for j, h in probs]
        gcol = [gbs[j][:, H + h:H + h + 1] for j, h in probs]
        np_ = range(len(probs))
        decay = [jnp.exp(jnp.where(incl, gcol[p] - grows[j][h:h + 1, :], NEG_INF)) for p, (j, h) in enumerate(probs)]
        kf = [k[p].astype(F32) for p in np_]
        kbeta = [kf[p] * beta[p] for p in np_]
        eg = [jnp.exp(gcol[p]) for p in np_]
        kk = [_dot_nt(kbeta[p].astype(BF16), k[p]) for p in np_]
        qk = [_dot_nt(q[p], k[p]) for p in np_]
        tmat = _unit_lower_inverse_many([jnp.where(strict, kk[p] * decay[p], 0.0) for p in np_])
        tb = [t.astype(BF16) for t in tmat]
        u = [_dot(tb[p], (v[p].astype(F32) * beta[p]).astype(BF16)) for p in np_]
        w = [_dot(tb[p], (kbeta[p] * eg[p]).astype(BF16)) for p in np_]
        for p, (j, h) in enumerate(probs):
            r0 = r0s[j]
            g_last = gcol[p][C - 1:C, :]
            u_ref[h, pl.ds(r0, C), :] = u[p]
            w_ref[h, pl.ds(r0, C), :] = w[p].astype(BF16)
            qe_ref[h, pl.ds(r0, C), :] = (q[p].astype(F32) * eg[p]).astype(BF16)
            kd_ref[h, pl.ds(r0, C), :] = (kf[p] * jnp.exp(g_last - gcol[p])).astype(BF16)
            qk_ref[h, pl.ds(r0, C), :] = jnp.where(incl, qk[p] * decay[p], 0.0).astype(BF16)
        return carry

    par = 4 if (rows // C) % 4 == 0 else 1
    lax.fori_loop(0, rows // (C * par), prepare, 0)

    def scan(c, carry):
        r0 = pl.multiple_of(c * C, C)
        gb = gb_ref[pl.ds(r0, C), :]
        state = [state_ref[h] for h in heads]
        sb = [s.astype(BF16) for s in state]
        w_s = [_dot(w_ref[h, pl.ds(r0, C), :], sb[h]) for h in heads]
        q_s = [_dot(qe_ref[h, pl.ds(r0, C), :], sb[h]) for h in heads]
        vnb = [(u_ref[h, pl.ds(r0, C), :] - w_s[h]).astype(BF16) for h in heads]
        o_in = [_dot(qk_ref[h, pl.ds(r0, C), :], vnb[h]) for h in heads]
        kv = [_dot_tn(kd_ref[h, pl.ds(r0, C), :], vnb[h]) for h in heads]
        for h in heads:
            g_last = gb[C - 1:C, H + h:H + h + 1]
            state_ref[h] = state[h] * jnp.exp(g_last) + kv[h]
            z = z_ref[pl.ds(r0, C), h * dh:(h + 1) * dh].astype(F32)
            y = _rms(q_s[h] + o_in[h], gain) * (z * _sigmoid(z))
            o_ref[pl.ds(r0, C), h * dh:(h + 1) * dh] = y.astype(o_ref.dtype)
        return carry

    lax.fori_loop(0, rows // C, scan, 0, unroll=2)


def dn_chunked(qkvn, proj_b, gb, grow, dn_norm, layer, batch, seq_len):
    T = qkvn.shape[0]
    H, dh, C = DN_HEADS, DN_HEAD_DIM, DN_CHUNK
    width = H * dh
    rows = _tile(seq_len, 512)
    tps = seq_len // rows
    zcol = 3
    return pl.pallas_call(
        functools.partial(_dn_chunk_kernel, rows=rows),
        grid=(batch, tps),
        in_specs=[pl.BlockSpec((rows, 3 * width), lambda b, n: (b * tps + n, 0)),
                  pl.BlockSpec((rows, width), lambda b, n: (b * tps + n, zcol)),
                  pl.BlockSpec((rows, LANES), lambda b, n: (b * tps + n, 0)),
                  pl.BlockSpec((rows // C, H, C), lambda b, n: (b * tps + n, 0, 0)),
                  pl.BlockSpec((None, 1, dh), lambda b, n: (layer, 0, 0))],
        out_specs=pl.BlockSpec((rows, width), lambda b, n: (b * tps + n, 0)),
        out_shape=jax.ShapeDtypeStruct((T, width), BF16),
        scratch_shapes=[pltpu.VMEM((H, dh, dh), F32),
                        pltpu.VMEM((H, rows, dh), F32),
                        pltpu.VMEM((H, rows, dh), BF16),
                        pltpu.VMEM((H, rows, dh), BF16),
                        pltpu.VMEM((H, rows, dh), BF16),
                        pltpu.VMEM((H, rows, C), BF16)],
        compiler_params=_cparams("parallel", "arbitrary"),
        name="dn_chunked",
    )(qkvn, proj_b, gb, grow, dn_norm)


def _rotate_pairs(x, cos_tab, sin_tab):
    return x * cos_tab + pltpu.roll(x, LANES // 2, axis=1) * sin_tab


def _mla_q_kernel(c_ref, g_ref, w_ref, cos_ref, sin_ref, o_ref):
    H = MLA_HEADS
    scale = (MLA_NOPE + MLA_ROPE) ** -0.5 * float(np.log2(np.e))
    cn = _rms(c_ref[...], g_ref[...]).astype(BF16)
    acc = _dot(cn, w_ref[...]) * scale
    cos_tab, sin_tab = cos_ref[...], sin_ref[...]
    for h in range(H):
        o_ref[:, 2 * h * LANES:(2 * h + 1) * LANES] = acc[:, h * LANES:(h + 1) * LANES].astype(o_ref.dtype)
        xr = acc[:, (H + h) * LANES:(H + h + 1) * LANES]
        o_ref[:, (2 * h + 1) * LANES:(2 * h + 2) * LANES] = _rotate_pairs(xr, cos_tab, sin_tab).astype(o_ref.dtype)


def mla_q(proj_small, gains, wq, cos_tab, sin_tab, layer):
    T = proj_small.shape[0]
    R = wq.shape[1]
    tm = _tile(T, 512)
    N = wq.shape[2]
    return pl.pallas_call(
        _mla_q_kernel,
        grid=(T // tm,),
        in_specs=[pl.BlockSpec((tm, R), lambda i: (i, 0)),
                  pl.BlockSpec((None, 1, R), lambda i: (layer, 0, 0)),
                  pl.BlockSpec((None, R, N), lambda i: (layer, 0, 0)),
                  pl.BlockSpec((tm, LANES), lambda i: (i, 0)),
                  pl.BlockSpec((tm, LANES), lambda i: (i, 0))],
        out_specs=pl.BlockSpec((tm, N), lambda i: (i, 0)),
        out_shape=jax.ShapeDtypeStruct((T, N), BF16),
        compiler_params=_cparams("parallel"),
        name="mla_q",
    )(proj_small, gains, wq, cos_tab, sin_tab)


def _mla_kv_kernel(c_ref, kr_ref, g_ref, w_ref, cos_ref, sin_ref, k_ref, v_ref):
    H = MLA_HEADS
    cn = _rms(c_ref[...], g_ref[...]).astype(BF16)
    acc = _dot(cn, w_ref[...])
    kr = _rotate_pairs(kr_ref[...], cos_ref[...], sin_ref[...]).astype(k_ref.dtype)
    for h in range(H):
        k_ref[:, 2 * h * LANES:(2 * h + 1) * LANES] = acc[:, h * LANES:(h + 1) * LANES].astype(k_ref.dtype)
        k_ref[:, (2 * h + 1) * LANES:(2 * h + 2) * LANES] = kr
    v_ref[...] = acc[:, H * LANES:].astype(v_ref.dtype)


def mla_kv(proj_small, gains, wkv, cos_tab, sin_tab, layer):
    T = proj_small.shape[0]
    R = wkv.shape[1]
    tm = _tile(T, 512)
    N = wkv.shape[2]
    H = MLA_HEADS
    kr_col = (2 * R) // LANES
    return pl.pallas_call(
        _mla_kv_kernel,
        grid=(T // tm,),
        in_specs=[pl.BlockSpec((tm, R), lambda i: (i, 1)),
                  pl.BlockSpec((tm, LANES), lambda i: (i, kr_col)),
                  pl.BlockSpec((None, 1, R), lambda i: (layer, 0, 0)),
                  pl.BlockSpec((None, R, N), lambda i: (layer, 0, 0)),
                  pl.BlockSpec((tm, LANES), lambda i: (i, 0)),
                  pl.BlockSpec((tm, LANES), lambda i: (i, 0))],
        out_specs=[pl.BlockSpec((tm, 2 * H * LANES), lambda i: (i, 0)),
                   pl.BlockSpec((tm, H * MLA_V), lambda i: (i, 0))],
        out_shape=[jax.ShapeDtypeStruct((T, 2 * H * LANES), BF16),
                   jax.ShapeDtypeStruct((T, H * MLA_V), BF16)],
        compiler_params=_cparams("parallel"),
        name="mla_kv",
    )(proj_small, proj_small, gains, wkv, cos_tab, sin_tab)


def _flash_kernel(qi_ref, kj_ref, q_ref, k_ref, v_ref, o_ref, m_ref, acc_ref, *, tq, tk):
    H, dv = MLA_HEADS, MLA_V
    dqk = 2 * LANES
    qi = qi_ref[pl.program_id(1)]
    kj = kj_ref[pl.program_id(1)]

    @pl.when(kj == 0)
    def _():
        m_ref[...] = jnp.full_like(m_ref, NEG_INF)
        acc_ref[...] = jnp.zeros_like(acc_ref)

    def step(masked):
        if masked:
            row = lax.broadcasted_iota(jnp.int32, (tq, tk), 0)
            col = lax.broadcasted_iota(jnp.int32, (tq, tk), 1)
            keep = col <= row
        ones = jnp.ones((tk, dv), BF16)
        scores = [_dot_nt(q_ref[:, h * dqk:(h + 1) * dqk], k_ref[:, h * dqk:(h + 1) * dqk]) for h in range(H)]
        probs, alphas = [], []
        for h in range(H):
            s = scores[h]
            if masked:
                s = jnp.where(keep, s, NEG_INF)
            m_prev = m_ref[h]
            m_new = jnp.maximum(m_prev, jnp.max(s, axis=-1, keepdims=True))
            alpha = jnp.exp2(m_prev - m_new)
            p = jnp.exp2(s - _lane_repeat(m_new, tk // LANES))
            m_ref[h] = m_new
            alphas.append(alpha)
            probs.append(p.astype(BF16))
        for h in range(H):
            v_ext = jnp.concatenate([v_ref[:, h * dv:(h + 1) * dv], ones], axis=1)
            acc_ref[h] = _lane_repeat(alphas[h], 2) * acc_ref[h] + _dot(probs[h], v_ext)

    @pl.when(kj < qi)
    def _():
        step(False)

    @pl.when(kj == qi)
    def _():
        step(True)
        for h in range(H):
            acc = acc_ref[h]
            o_ref[:, h * dv:(h + 1) * dv] = (acc[:, :dv] / acc[:, dv:]).astype(o_ref.dtype)


def mla_flash(qf, kf, v, batch, seq_len):
    T = qf.shape[0]
    H, dv = MLA_HEADS, MLA_V
    t = _tile(seq_len, 512)
    nq = seq_len // t
    pairs = [(i, j) for i in range(nq) for j in range(i + 1)]
    qi_tab = jnp.asarray([i for i, _ in pairs], jnp.int32)
    kj_tab = jnp.asarray([j for _, j in pairs], jnp.int32)
    return pl.pallas_call(
        functools.partial(_flash_kernel, tq=t, tk=t),
        grid_spec=pltpu.PrefetchScalarGridSpec(
            num_scalar_prefetch=2,
            grid=(batch, len(pairs)),
            in_specs=[pl.BlockSpec((t, qf.shape[1]), lambda b, n, qi, kj: (b * nq + qi[n], 0)),
                      pl.BlockSpec((t, kf.shape[1]), lambda b, n, qi, kj: (b * nq + kj[n], 0)),
                      pl.BlockSpec((t, v.shape[1]), lambda b, n, qi, kj: (b * nq + kj[n], 0))],
            out_specs=pl.BlockSpec((t, H * dv), lambda b, n, qi, kj: (b * nq + qi[n], 0)),
            scratch_shapes=[pltpu.VMEM((H, t, LANES), F32),
                            pltpu.VMEM((H, t, 2 * dv), F32)]),
        out_shape=jax.ShapeDtypeStruct((T, H * dv), BF16),
        compiler_params=_cparams("parallel", "arbitrary"),
        name="mla_flash",
    )(qi_tab, kj_tab, qf, kf, v)


def _merge_kernel(ya_ref, yb_ref, yc_ref, ga_ref, gb_ref, gc_ref, wa_ref, wb_ref, wc_ref, o_ref, sa_ref, sb_ref, sc_ref):
    @pl.when(pl.program_id(1) == 0)
    def _():
        sa_ref[...] = wa_ref[...].astype(BF16)
        sb_ref[...] = wb_ref[...].astype(BF16)
        sc_ref[...] = wc_ref[...].astype(BF16)

    acc = ga_ref[...].astype(F32) * _dot(ya_ref[...], sa_ref[...])
    acc = acc + gb_ref[...].astype(F32) * _dot(yb_ref[...], sb_ref[...])
    acc = acc + gc_ref[...].astype(F32) * _dot(yc_ref[...], sc_ref[...])
    o_ref[...] = acc.astype(o_ref.dtype)


def merge_branches(ya, yb, yc, gates, w_branch, layer):
    T, Wb = ya.shape
    D = w_branch.shape[-1]
    tm = _tile(T, 1024)
    tn = _tile(D, 512)
    nj = D // tn
    y_spec = pl.BlockSpec((tm, Wb), lambda j, i: (i, 0))

    def g_spec(n):
        return pl.BlockSpec((tm, tn), lambda j, i: (i, n * nj + j))

    def w_spec(n):
        return pl.BlockSpec((None, None, Wb, tn), lambda j, i: (layer, n, 0, j))

    return pl.pallas_call(
        _merge_kernel,
        grid=(nj, T // tm),
        in_specs=[y_spec, y_spec, y_spec, g_spec(0), g_spec(1), g_spec(2), w_spec(0), w_spec(1), w_spec(2)],
        out_specs=pl.BlockSpec((tm, tn), lambda j, i: (i, j)),
        out_shape=jax.ShapeDtypeStruct((T, D), BF16),
        scratch_shapes=[pltpu.VMEM((Wb, tn), BF16)] * 3,
        compiler_params=_cparams("arbitrary", "arbitrary"),
        name="merge_branches",
    )(ya, yb, yc, gates, gates, gates, w_branch, w_branch, w_branch)


def _out_proj_kernel(m_ref, w_ref, x_ref, g_ref, xo_ref, *maybe_ho_ref):
    xn = x_ref[...] + _dot(m_ref[...], w_ref[...])
    xo_ref[...] = xn
    for ho_ref in maybe_ho_ref:
        ho_ref[...] = _rms(xn, g_ref[...]).astype(ho_ref.dtype)


def out_proj_residual(merged, w_out_bf, x, norm_gain, layer, emit_norm):
    T, D = x.shape
    tm = _tile(T, 512)
    row_spec = pl.BlockSpec((tm, D), lambda i: (i, 0))
    out_specs = [row_spec, row_spec] if emit_norm else [row_spec]
    out_shape = [jax.ShapeDtypeStruct((T, D), F32)] + ([jax.ShapeDtypeStruct((T, D), BF16)] if emit_norm else [])
    return pl.pallas_call(
        _out_proj_kernel,
        grid=(T // tm,),
        in_specs=[row_spec,
                  pl.BlockSpec((None, D, D), lambda i: (layer, 0, 0)),
                  row_spec,
                  pl.BlockSpec((None, 1, D), lambda i: (layer, 0, 0))],
        out_specs=out_specs,
        out_shape=out_shape,
        compiler_params=_cparams("parallel"),
        name="out_proj_residual",
    )(merged, w_out_bf, x, norm_gain)


def _ffn_kernel(h_ref, wg_ref, wu_ref, wd_ref, y_ref, acc_ref):
    f = pl.program_id(1)

    @pl.when(f == 0)
    def _():
        acc_ref[...] = jnp.zeros_like(acc_ref)

    h = h_ref[...]
    gate = _dot(h, wg_ref[...])
    up = _dot(h, wu_ref[...])
    act = (gate * _sigmoid(gate) * up).astype(BF16)
    acc_ref[...] += _dot(act, wd_ref[...])

    @pl.when(f == pl.num_programs(1) - 1)
    def _():
        y_ref[...] = acc_ref[...].astype(y_ref.dtype)


def ffn_swiglu(h, wg, wu, wd, widx):
    T, D = h.shape
    F = wg.shape[-1]
    tm = _tile(T, 512)
    tf = _tile(F, 1024)
    return pl.pallas_call(
        _ffn_kernel,
        grid=(T // tm, F // tf),
        in_specs=[pl.BlockSpec((tm, D), lambda i, f: (i, 0)),
                  pl.BlockSpec((None, D, tf), lambda i, f: (widx, 0, f)),
                  pl.BlockSpec((None, D, tf), lambda i, f: (widx, 0, f)),
                  pl.BlockSpec((None, tf, D), lambda i, f: (widx, f, 0))],
        out_specs=pl.BlockSpec((tm, D), lambda i, f: (i, 0)),
        out_shape=jax.ShapeDtypeStruct((T, D), BF16),
        scratch_shapes=[pltpu.VMEM((tm, D), F32)],
        compiler_params=_cparams("parallel", "arbitrary"),
        name="ffn_swiglu",
    )(h, wg, wu, wd)


def _router_kernel(x_ref, g_ref, w_ref, o_ref):
    h = _rms(x_ref[...], g_ref[...])
    logits = jnp.dot(h, w_ref[...], preferred_element_type=F32, precision=lax.Precision.HIGHEST)
    lane = lax.broadcasted_iota(jnp.int32, logits.shape, 1)
    lanef = lane.astype(F32)
    big = float(LANES)
    logits = jnp.where(lane < N_EXPERTS, logits, NEG_INF)
    l1 = jnp.max(logits, axis=-1, keepdims=True)
    i1 = jnp.min(jnp.where(logits == l1, lanef, big), axis=-1, keepdims=True)
    rest = jnp.where(lanef == i1, NEG_INF, logits)
    l2 = jnp.max(rest, axis=-1, keepdims=True)
    i2 = jnp.min(jnp.where(rest == l2, lanef, big), axis=-1, keepdims=True)
    e2 = jnp.exp(l2 - l1)
    w1 = 1.0 / (1.0 + e2)
    w2 = e2 / (1.0 + e2)
    out = jnp.where(lane == 0, i1, jnp.where(lane == 1, i2, jnp.where(lane == 2, w1, jnp.where(lane == 3, w2, 0.0))))
    o_ref[...] = out


def moe_router(x, norm_gain, w_router_pad, layer, widx):
    T, D = x.shape
    tm = _tile(T, 512)
    return pl.pallas_call(
        _router_kernel,
        grid=(T // tm,),
        in_specs=[pl.BlockSpec((tm, D), lambda i: (i, 0)),
                  pl.BlockSpec((None, 1, D), lambda i: (layer, 0, 0)),
                  pl.BlockSpec((None, D, LANES), lambda i: (widx, 0, 0))],
        out_specs=pl.BlockSpec((tm, LANES), lambda i: (i, 0)),
        out_shape=jax.ShapeDtypeStruct((T, LANES), F32),
        compiler_params=_cparams("parallel"),
        name="moe_router",
    )(x, norm_gain, w_router_pad)


def _start_row_gather(idx_hbm, src_hbm, blk, slot, idx_smem, rows_vmem, idx_sem, row_sem, n):
    idx_copy = pltpu.make_async_copy(idx_hbm.at[blk], idx_smem.at[slot], idx_sem.at[slot])
    idx_copy.start()
    idx_copy.wait()

    def issue(i, c):
        for priority in range(2):
            r = 2 * i + priority
            pltpu.make_async_copy(src_hbm.at[pl.ds(idx_smem[slot, r], 1)],
                                  rows_vmem.at[slot, pl.ds(r, 1)],
                                  row_sem.at[slot]).start(priority=priority)
        return c

    lax.fori_loop(0, n // 2, issue, 0, unroll=4)


def _wait_row_gather(src_hbm, slot, rows_vmem, row_sem, n):
    pltpu.make_async_copy(src_hbm.at[pl.ds(0, n)], rows_vmem.at[slot], row_sem.at[slot]).wait()


def _gather_rows_kernel(meta_ref, idx_hbm, x_hbm, g_ref, o_ref, idx_smem, rows_vmem, idx_sem, row_sem, *, bm):
    m = pl.program_id(0)
    n_used = meta_ref[pl.num_programs(0)]
    slot = m % 2
    start = functools.partial(_start_row_gather, idx_hbm, x_hbm, idx_smem=idx_smem, rows_vmem=rows_vmem,
                              idx_sem=idx_sem, row_sem=row_sem, n=bm)

    @pl.when(m == 0)
    def _():
        start(blk=0, slot=0)

    @pl.when(m + 1 < n_used)
    def _():
        start(blk=m + 1, slot=1 - slot)

    @pl.when(m < n_used)
    def _():
        _wait_row_gather(x_hbm, slot, rows_vmem, row_sem, bm)
        o_ref[...] = _rms(rows_vmem[slot], g_ref[...]).astype(o_ref.dtype)

    @pl.when(m >= n_used)
    def _():
        o_ref[...] = jnp.zeros_like(o_ref)


def moe_gather_norm(x, row_tok, meta, norm_gain, layer, bm):
    T, D = x.shape
    n_blocks = row_tok.shape[0]
    return pl.pallas_call(
        functools.partial(_gather_rows_kernel, bm=bm),
        grid_spec=pltpu.PrefetchScalarGridSpec(
            num_scalar_prefetch=1,
            grid=(n_blocks,),
            in_specs=[pl.BlockSpec(memory_space=pl.ANY),
                      pl.BlockSpec(memory_space=pl.ANY),
                      pl.BlockSpec((None, 1, D), lambda i, meta: (layer, 0, 0))],
            out_specs=pl.BlockSpec((bm, D), lambda i, meta: (i, 0)),
            scratch_shapes=[pltpu.SMEM((2, bm), jnp.int32),
                            pltpu.VMEM((2, bm, D), F32),
                            pltpu.SemaphoreType.DMA((2,)),
                            pltpu.SemaphoreType.DMA((2,))]),
        out_shape=jax.ShapeDtypeStruct((n_blocks * bm, D), BF16),
        compiler_params=_cparams("arbitrary"),
        name="moe_gather_norm",
    )(meta, row_tok, x, norm_gain)


def _expert_up_kernel(meta_ref, x_ref, wg_ref, wu_ref, o_ref, wgb_ref, wub_ref):
    m = pl.program_id(1)
    n_used = meta_ref[pl.num_programs(1)]
    new_expert = jnp.logical_or(m == 0, meta_ref[m] != meta_ref[jnp.maximum(m - 1, 0)])

    @pl.when(jnp.logical_and(new_expert, m < n_used))
    def _():
        wgb_ref[...] = wg_ref[...].astype(BF16)
        wub_ref[...] = wu_ref[...].astype(BF16)

    @pl.when(m < n_used)
    def _():
        x = x_ref[...]
        gate = _dot(x, wgb_ref[...])
        up = _dot(x, wub_ref[...])
        o_ref[...] = (gate * _sigmoid(gate) * up).astype(o_ref.dtype)

    @pl.when(m >= n_used)
    def _():
        o_ref[...] = jnp.zeros_like(o_ref)


def _last_used(m, meta, nb):
    return jnp.minimum(m, meta[nb] - 1)


def moe_expert_up(xs, meta, wg, wu, widx, bm):
    R, D = xs.shape
    F = wg.shape[-1]
    tf = _tile(F, 1024)
    nb = R // bm

    def w_map(f, m, meta):
        return (widx, meta[_last_used(m, meta, nb)], 0, f)

    return pl.pallas_call(
        _expert_up_kernel,
        grid_spec=pltpu.PrefetchScalarGridSpec(
            num_scalar_prefetch=1,
            grid=(F // tf, nb),
            in_specs=[pl.BlockSpec((bm, D), lambda f, m, meta: (_last_used(m, meta, nb), 0)),
                      pl.BlockSpec((None, None, D, tf), w_map),
                      pl.BlockSpec((None, None, D, tf), w_map)],
            out_specs=pl.BlockSpec((bm, tf), lambda f, m, meta: (m, f)),
            scratch_shapes=[pltpu.VMEM((D, tf), BF16), pltpu.VMEM((D, tf), BF16)]),
        out_shape=jax.ShapeDtypeStruct((R, F), BF16),
        compiler_params=_cparams("arbitrary", "arbitrary"),
        name="moe_expert_up",
    )(meta, xs, wg, wu)


def _expert_down_kernel(meta_ref, a_ref, wd_ref, o_ref, wdb_ref):
    m = pl.program_id(1)
    n_used = meta_ref[pl.num_programs(1)]
    new_expert = jnp.logical_or(m == 0, meta_ref[m] != meta_ref[jnp.maximum(m - 1, 0)])

    @pl.when(jnp.logical_and(new_expert, m < n_used))
    def _():
        wdb_ref[...] = wd_ref[...].astype(BF16)

    @pl.when(m < n_used)
    def _():
        o_ref[...] = _dot(a_ref[...], wdb_ref[...]).astype(o_ref.dtype)

    @pl.when(m >= n_used)
    def _():
        o_ref[...] = jnp.zeros_like(o_ref)


def moe_expert_down(act, meta, wd, widx, bm):
    R, F = act.shape
    D = wd.shape[-1]
    tn = _tile(D, 512)
    nb = R // bm
    return pl.pallas_call(
        _expert_down_kernel,
        grid_spec=pltpu.PrefetchScalarGridSpec(
            num_scalar_prefetch=1,
            grid=(D // tn, nb),
            in_specs=[pl.BlockSpec((bm, F), lambda j, m, meta: (_last_used(m, meta, nb), 0)),
                      pl.BlockSpec((None, None, F, tn),
                                   lambda j, m, meta: (widx, meta[_last_used(m, meta, nb)], 0, j))],
            out_specs=pl.BlockSpec((bm, tn), lambda j, m, meta: (m, j)),
            scratch_shapes=[pltpu.VMEM((F, tn), BF16)]),
        out_shape=jax.ShapeDtypeStruct((R, D), F32),
        compiler_params=_cparams("arbitrary", "arbitrary"),
        name="moe_expert_down",
    )(meta, act, wd)


def _combine_kernel(dest_hbm, yb_hbm, rt_ref, y_ref, idx_smem, rows_vmem, idx_sem, row_sem, *, tm):
    m = pl.program_id(0)
    slot = m % 2
    n = TOP_K * tm
    start = functools.partial(_start_row_gather, dest_hbm, yb_hbm, idx_smem=idx_smem, rows_vmem=rows_vmem,
                              idx_sem=idx_sem, row_sem=row_sem, n=n)

    @pl.when(m == 0)
    def _():
        start(blk=0, slot=0)

    @pl.when(m + 1 < pl.num_programs(0))
    def _():
        start(blk=m + 1, slot=1 - slot)

    _wait_row_gather(yb_hbm, slot, rows_vmem, row_sem, n)
    rt = rt_ref[...]
    y = rt[:, 2:3] * rows_vmem[slot, 0:tm, :] + rt[:, 3:4] * rows_vmem[slot, tm:2 * tm, :]
    y_ref[...] = y.astype(y_ref.dtype)


def moe_combine(yb, dest_blocks, route, tm):
    T = route.shape[0]
    D = yb.shape[1]
    return pl.pallas_call(
        functools.partial(_combine_kernel, tm=tm),
        grid=(T // tm,),
        in_specs=[pl.BlockSpec(memory_space=pl.ANY),
                  pl.BlockSpec(memory_space=pl.ANY),
                  pl.BlockSpec((tm, LANES), lambda i: (i, 0))],
        out_specs=pl.BlockSpec((tm, D), lambda i: (i, 0)),
        out_shape=jax.ShapeDtypeStruct((T, D), BF16),
        scratch_shapes=[pltpu.SMEM((2, TOP_K * tm), jnp.int32),
                        pltpu.VMEM((2, TOP_K * tm, D), F32),
                        pltpu.SemaphoreType.DMA((2,)),
                        pltpu.SemaphoreType.DMA((2,))],
        compiler_params=_cparams("arbitrary"),
        name="moe_combine",
    )(dest_blocks, yb, route)


def moe_layer(x, norm_ffn, w_router_pad, wg, wu, wd, layer, widx):
    T, D = x.shape
    E = N_EXPERTS
    bm = _tile(T, 512)
    route = moe_router(x, norm_ffn, w_router_pad, layer, widx)
    flat_e = route[:, :TOP_K].astype(jnp.int32).reshape(-1)
    onehot = (flat_e[:, None] == jnp.arange(E, dtype=jnp.int32)[None, :]).astype(jnp.int32)
    incl = jnp.cumsum(onehot, axis=0)
    counts = incl[-1]
    rank = jnp.sum((incl - onehot) * onehot, axis=1)
    padded = (counts + bm - 1) // bm * bm
    pad_end = jnp.cumsum(padded)
    pad_start = pad_end - padded
    dest = pad_start[flat_e] + rank
    n_rows = -(-(T * TOP_K + E * (bm - 1)) // bm) * bm
    n_blocks = n_rows // bm
    flat_tok = jnp.repeat(jnp.arange(T, dtype=jnp.int32), TOP_K)
    row_tok = jnp.zeros((n_rows,), jnp.int32).at[dest].set(flat_tok)
    block_e = jnp.minimum(
        jnp.sum(jnp.arange(n_blocks, dtype=jnp.int32)[:, None] * bm >= pad_end[None, :], axis=1), E - 1
    ).astype(jnp.int32)
    meta = jnp.concatenate([block_e, (pad_end[-1:] // bm).astype(jnp.int32)])

    xs = moe_gather_norm(x, row_tok.reshape(n_blocks, bm), meta, norm_ffn, layer, bm)
    act = moe_expert_up(xs, meta, wg, wu, widx, bm)
    yb = moe_expert_down(act, meta, wd, widx, bm)
    tmc = _tile(T, 256)
    dest_blocks = dest.reshape(T // tmc, tmc, TOP_K).transpose(0, 2, 1).reshape(T // tmc, TOP_K * tmc)
    return moe_combine(yb, dest_blocks, route, tmc)


def _ple_kernel(x_ref, y_ref, p_ref, wg_ref, wp_ref, gp_ref, gn_ref, xo_ref, ho_ref):
    x = x_ref[...] + y_ref[...].astype(F32)
    hp = _rms(x, gp_ref[...]).astype(BF16)
    gate = _sigmoid(_dot(hp, wg_ref[...]))
    emb = _dot(p_ref[...].astype(BF16), wp_ref[...])
    xn = x + emb * gate
    xo_ref[...] = xn
    ho_ref[...] = _rms(xn, gn_ref[...]).astype(ho_ref.dtype)


def ple_layer(x, y, p, w_gate_bf, w_proj_bf, ple_gain, next_gain, layer, gain_idx, out_dtype):
    T, D = x.shape
    Pd = p.shape[-1]
    tm = _tile(T, 512)
    row_spec = pl.BlockSpec((tm, D), lambda i: (i, 0))
    return pl.pallas_call(
        _ple_kernel,
        grid=(T // tm,),
        in_specs=[row_spec,
                  row_spec,
                  pl.BlockSpec((None, tm, Pd), lambda i: (layer, i, 0)),
                  pl.BlockSpec((None, D, D), lambda i: (layer, 0, 0)),
                  pl.BlockSpec((None, Pd, D), lambda i: (layer, 0, 0)),
                  pl.BlockSpec((None, 1, D), lambda i: (layer, 0, 0)),
                  pl.BlockSpec((None, 1, D), lambda i: (gain_idx, 0, 0))],
        out_specs=[row_spec, row_spec],
        out_shape=[jax.ShapeDtypeStruct((T, D), F32), jax.ShapeDtypeStruct((T, D), out_dtype)],
        compiler_params=_cparams("parallel"),
        name="ple_layer",
    )(x, y, p, w_gate_bf, w_proj_bf, ple_gain, next_gain)


def _split_points(D):
    swa_q = SWA_Q_HEADS * SWA_HEAD_DIM
    swa_kv = SWA_KV_HEADS * SWA_HEAD_DIM
    dn_w = DN_HEADS * DN_HEAD_DIM
    return swa_q + 2 * swa_kv, swa_q + 2 * swa_kv + 4 * dn_w


def _pack_small_in_proj(w_in_t, q_lora, kv_lora):
    _, b_end = _split_points(w_in_t.shape[2])
    H = DN_HEADS
    o = b_end
    beta = w_in_t[:, o:o + H]
    decay = w_in_t[:, o + H:o + 2 * H]
    o += 2 * H
    cq = w_in_t[:, o:o + q_lora]
    o += q_lora
    ckv = w_in_t[:, o:o + kv_lora]
    o += kv_lora
    half = MLA_ROPE // 2
    kr1 = w_in_t[:, o:o + half]
    kr2 = w_in_t[:, o + half:o + 2 * half]
    o += MLA_ROPE
    pad = jnp.zeros((w_in_t.shape[0], LANES - 2 * H, w_in_t.shape[2]), w_in_t.dtype)
    packed = jnp.concatenate([cq, ckv, kr1, kr2, kr2, kr1, beta, decay, pad], axis=1)
    return packed, o


def _pack_w_uq(w_uq):
    Ld, R, _ = w_uq.shape
    H, half = MLA_HEADS, MLA_ROPE // 2
    w = w_uq.reshape(Ld, R, H, MLA_NOPE + MLA_ROPE)
    nope = w[..., :MLA_NOPE].reshape(Ld, R, H * MLA_NOPE)
    r1 = w[..., MLA_NOPE:MLA_NOPE + half]
    r2 = w[..., MLA_NOPE + half:]
    rope = jnp.concatenate([r1, r2, r2, r1], axis=-1).reshape(Ld, R, H * LANES)
    return jnp.concatenate([nope, rope], axis=-1).astype(BF16)


def _pack_w_ukv(w_ukv):
    Ld, R, _ = w_ukv.shape
    H = MLA_HEADS
    w = w_ukv.reshape(Ld, R, H, MLA_NOPE + MLA_V)
    kn = w[..., :MLA_NOPE].reshape(Ld, R, H * MLA_NOPE)
    vv = w[..., MLA_NOPE:].reshape(Ld, R, H * MLA_V)
    return jnp.concatenate([kn, vv], axis=-1).astype(BF16)


def _rope_tables(positions):
    half = MLA_ROPE // 2
    inv_freq = ROPE_THETA ** (-jnp.arange(half, dtype=F32) / half)
    ang = positions.astype(F32).reshape(-1)[:, None] * inv_freq
    cos, sin = jnp.cos(ang), jnp.sin(ang)
    zeros = jnp.zeros_like(cos)
    return (jnp.concatenate([cos, cos, zeros, zeros], axis=-1),
            jnp.concatenate([-sin, sin, zeros, zeros], axis=-1))


def _lane_vec(v, offset):
    Ld, n = v.shape
    out = jnp.zeros((Ld, 1, LANES), F32)
    return out.at[:, 0, offset:offset + n].set(v.astype(F32))


def kernel(x, p, positions, norm_mix, w_in, conv_w, dn_a_log, dn_dt_bias, dn_norm, swa_sinks, mla_q_norm, w_uq,
           mla_kv_norm, w_ukv, w_branch, w_out, norm_ffn, w_ffn_gate, w_ffn_up, w_ffn_down, w_router, w_exp_gate,
           w_exp_up, w_exp_down, norm_ple, w_ple_gate, w_ple_proj, final_norm):
    B, S, D = x.shape
    T = B * S
    depth = w_in.shape[0]
    q_lora, kv_lora = w_uq.shape[1], w_ukv.shape[1]
    a_end, b_end = _split_points(D)

    w_in_t = jnp.swapaxes(w_in, 1, 2)
    w_small_t, gate_row0 = _pack_small_in_proj(w_in_t, q_lora, kv_lora)
    w_gates_t = w_in_t[:, gate_row0:]
    wq_packed = _pack_w_uq(w_uq)
    wkv_packed = _pack_w_ukv(w_ukv)
    w_out_bf = w_out.astype(BF16)
    w_ple_gate_bf = w_ple_gate.astype(BF16)
    w_ple_proj_bf = w_ple_proj.astype(BF16)
    wfg, wfu, wfd = w_ffn_gate.astype(BF16), w_ffn_up.astype(BF16), w_ffn_down.astype(BF16)
    weg, weu, wed = w_exp_gate, w_exp_up, w_exp_down
    w_router_pad = jnp.pad(w_router, ((0, 0), (0, 0), (0, LANES - w_router.shape[-1])))
    cos_tab, sin_tab = _rope_tables(positions)
    alog_vec = _lane_vec(dn_a_log, DN_HEADS)
    dtb_vec = _lane_vec(dn_dt_bias, DN_HEADS)
    row = lambda g: g.reshape(g.shape[0], 1, g.shape[-1])
    norm_mix3, norm_ffn3, norm_ple3 = row(norm_mix), row(norm_ffn), row(norm_ple)
    dn_norm3, mla_q_norm3, mla_kv_norm3 = row(dn_norm), row(mla_q_norm), row(mla_kv_norm)
    final3 = final_norm.reshape(1, 1, D)

    xf = x.reshape(T, D)
    pf = p.reshape(depth, T, p.shape[-1])
    h = rmsnorm_rows(xf, norm_mix3, 0, BF16)
    for i in range(depth):
        proj_a = matmul_ws(h, w_in_t, i, n_cols=a_end, row_block_offset=0, tn=a_end // 2, out_dtype=BF16,
                           tm_pref=2048)
        tn_b = 512
        proj_b = matmul_ws(h, w_in_t, i, n_cols=b_end - a_end, row_block_offset=a_end // tn_b, tn=tn_b,
                           out_dtype=BF16, tm_pref=2048)
        proj_small = matmul_ws(h, w_small_t, i, n_cols=w_small_t.shape[1], row_block_offset=0,
                               tn=w_small_t.shape[1], out_dtype=F32)
        gates = matmul_ws(h, w_gates_t, i, n_cols=w_gates_t.shape[1], row_block_offset=0, tn=1024,
                          out_dtype=BF16, act="sigmoid")

        y_a = swa_attention(proj_a, swa_sinks, i, S)

        qkvn, gb = dn_prep(proj_b, proj_small, conv_w, alog_vec, dtb_vec, i, S)
        nchunks = T // DN_CHUNK
        grow = gb[:, DN_HEADS:2 * DN_HEADS].reshape(nchunks, DN_CHUNK, DN_HEADS).transpose(0, 2, 1)
        y_b = dn_chunked(qkvn, proj_b, gb, grow, dn_norm3, i, B, S)

        qfull = mla_q(proj_small, mla_q_norm3, wq_packed, cos_tab, sin_tab, i)
        kfull, vfull = mla_kv(proj_small, mla_kv_norm3, wkv_packed, cos_tab, sin_tab, i)
        y_c = mla_flash(qfull, kfull, vfull, B, S)

        merged = merge_branches(y_a, y_b, y_c, gates, w_branch, i)
        j = i // 2
        if i % 2 == 0:
            xf, h2 = out_proj_residual(merged, w_out_bf, xf, norm_ffn3, i, True)
            y = ffn_swiglu(h2, wfg, wfu, wfd, j)
        else:
            (xf,) = out_proj_residual(merged, w_out_bf, xf, norm_ffn3, i, False)
            y = moe_layer(xf, norm_ffn3, w_router_pad, weg, weu, wed, i, j)

        if i + 1 < depth:
            xf, h = ple_layer(xf, y, pf, w_ple_gate_bf, w_ple_proj_bf, norm_ple3, norm_mix3, i, i + 1, BF16)
        else:
            xf, out = ple_layer(xf, y, pf, w_ple_gate_bf, w_ple_proj_bf, norm_ple3, final3, i, 0, F32)
    return out.reshape(B, S, D)
```

```python
import functools

import jax
import jax.numpy as jnp
import numpy as np
from jax import lax
from jax.experimental import pallas as pl
from jax.experimental.pallas import tpu as pltpu

BF16 = jnp.bfloat16
F32 = jnp.float32
NEG_INF = float("-inf")

NORM_EPS = 1e-6
SWA_Q_HEADS, SWA_KV_HEADS, SWA_HEAD_DIM, SWA_WINDOW = 16, 4, 64, 128
DN_HEADS, DN_HEAD_DIM, DN_CONV, DN_CHUNK = 8, 128, 4, 64
MLA_HEADS, MLA_NOPE, MLA_ROPE, MLA_V = 8, 128, 64, 128
ROPE_THETA = 10000.0
N_EXPERTS, TOP_K = 8, 2

LANES = 128
VMEM_LIMIT_BYTES = 56 * 1024 * 1024


def _cparams(*semantics):
    return pltpu.CompilerParams(dimension_semantics=semantics, vmem_limit_bytes=VMEM_LIMIT_BYTES)


def _tile(n, pref):
    t = min(n, pref)
    while n % t:
        t //= 2
    return t


def _dot(a, b):
    return jnp.dot(a, b, preferred_element_type=F32)


def _dot_nt(a, b):
    return lax.dot_general(a, b, (((1,), (1,)), ((), ())), preferred_element_type=F32)


def _dot_tn(a, b):
    return lax.dot_general(a, b, (((0,), (0,)), ((), ())), preferred_element_type=F32)


def _rms(x, gain):
    inv = lax.rsqrt(jnp.mean(x * x, axis=-1, keepdims=True) + NORM_EPS)
    return x * inv * gain


def _sigmoid(x):
    return 0.5 * jnp.tanh(0.5 * x) + 0.5


def _lane_repeat(x, n):
    return jnp.concatenate([x] * n, axis=1)


def _rmsnorm_kernel(x_ref, g_ref, o_ref):
    o_ref[...] = _rms(x_ref[...], g_ref[...]).astype(o_ref.dtype)


def rmsnorm_rows(x, gains, layer, out_dtype):
    T, D = x.shape
    tm = _tile(T, 512)
    return pl.pallas_call(
        _rmsnorm_kernel,
        grid=(T // tm,),
        in_specs=[pl.BlockSpec((tm, D), lambda i: (i, 0)),
                  pl.BlockSpec((None, 1, D), lambda i: (layer, 0, 0))],
        out_specs=pl.BlockSpec((tm, D), lambda i: (i, 0)),
        out_shape=jax.ShapeDtypeStruct((T, D), out_dtype),
        compiler_params=_cparams("parallel"),
        name="rmsnorm_rows",
    )(x, gains)


def _matmul_ws_kernel(x_ref, wt_ref, o_ref, wb_ref, *, act):
    @pl.when(pl.program_id(1) == 0)
    def _():
        wt = wt_ref[...]
        wt = wt.reshape(wt.shape[-2:])
        wb_ref[...] = wt.T.astype(BF16)

    acc = _dot(x_ref[...], wb_ref[...])
    if act == "sigmoid":
        acc = _sigmoid(acc)
    o_ref[...] = acc.astype(o_ref.dtype)


def matmul_ws(x, w_t, layer, *, n_cols, row_block_offset, tn, out_dtype, act=None, tm_pref=1024, row_offset=None):
    M, K = x.shape
    tm = _tile(M, tm_pref)
    assert n_cols % tn == 0
    return pl.pallas_call(
        functools.partial(_matmul_ws_kernel, act=act),
        grid=(n_cols // tn, M // tm),
        in_specs=[pl.BlockSpec((tm, K), lambda j, i: (i, 0)),
                  (pl.BlockSpec((None, tn, K), lambda j, i: (layer, j + row_block_offset, 0))
                   if row_offset is None else
                   pl.BlockSpec((pl.Element(1), pl.Element(tn), pl.Element(K)),
                                lambda j, i: (layer, pl.multiple_of(row_offset + j * tn, 16), 0)))],
        out_specs=pl.BlockSpec((tm, tn), lambda j, i: (i, j)),
        out_shape=jax.ShapeDtypeStruct((M, n_cols), out_dtype),
        scratch_shapes=[pltpu.VMEM((K, tn), BF16)],
        compiler_params=_cparams("arbitrary", "arbitrary"),
        name="matmul_ws",
    )(x, w_t)


def _swa_kernel(sinks_ref, q_ref, kc_ref, vc_ref, kp_ref, vp_ref, o_ref, *, layer, tiles_per_seq, rows):
    L = SWA_WINDOW
    dh = SWA_HEAD_DIM
    G = SWA_Q_HEADS // SWA_KV_HEADS
    first = (pl.program_id(0) % tiles_per_seq) == 0
    kall = jnp.concatenate([kp_ref[...], kc_ref[...]], axis=0)
    vall = jnp.concatenate([vp_ref[...], vc_ref[...]], axis=0)
    qi = lax.broadcasted_iota(jnp.int32, (L, 2 * L), 0)
    kj = lax.broadcasted_iota(jnp.int32, (L, 2 * L), 1)
    rel = qi + L - kj
    band = jnp.logical_and(rel >= 0, rel < SWA_WINDOW)
    kj_min = jnp.where(first, L, 0)
    scale = dh ** -0.5
    problems = [(b, h) for b in range(rows // L) for h in range(SWA_KV_HEADS)]
    scores = []
    for b, h in problems:
        qs = jnp.concatenate(
            [q_ref[b * L:(b + 1) * L, (h * G + g) * dh:(h * G + g + 1) * dh] for g in range(G)], axis=0)
        scores.append(_dot_nt(qs, kall[b * L:(b + 2) * L, h * dh:(h + 1) * dh]))
    probs, inv_denoms = [], []
    for (b, h), s in zip(problems, scores):
        valid = jnp.logical_and(band, kj >= kj_min) if b == 0 else band
        es, rs = [], []
        for g in range(G):
            sg = jnp.where(valid, s[g * L:(g + 1) * L] * scale, NEG_INF)
            sink = sinks_ref[layer, h * G + g]
            m = jnp.maximum(jnp.full((L, LANES), sink, F32), jnp.max(sg, axis=-1, keepdims=True))
            e = jnp.exp(sg - _lane_repeat(m, 2 * L // LANES))
            denom = jnp.sum(e, axis=-1, keepdims=True) + jnp.exp(sink - m[:, :1])
            es.append(e.astype(BF16))
            rs.append(1.0 / denom)
        probs.append(jnp.concatenate(es, axis=0))
        inv_denoms.append(rs)
    for (b, h), e, rs in zip(problems, probs, inv_denoms):
        o = _dot(e, vall[b * L:(b + 2) * L, h * dh:(h + 1) * dh])
        for g in range(G):
            hq = h * G + g
            o_ref[b * L:(b + 1) * L, hq * dh:(hq + 1) * dh] = (o[g * L:(g + 1) * L] * rs[g]).astype(o_ref.dtype)


def swa_attention(qkv, sinks, layer, seq_len):
    T = qkv.shape[0]
    L = SWA_WINDOW
    rows = _tile(seq_len, 512)
    wq = SWA_Q_HEADS * SWA_HEAD_DIM
    wkv = SWA_KV_HEADS * SWA_HEAD_DIM
    kcol, vcol = wq // wkv, wq // wkv + 1
    rpl = rows // L

    def prev_map(col):
        return lambda i: (jnp.maximum(i * rpl - 1, 0), col)

    return pl.pallas_call(
        functools.partial(_swa_kernel, layer=layer, tiles_per_seq=seq_len // rows, rows=rows),
        grid=(T // rows,),
        in_specs=[pl.BlockSpec(memory_space=pltpu.SMEM),
                  pl.BlockSpec((rows, wq), lambda i: (i, 0)),
                  pl.BlockSpec((rows, wkv), lambda i: (i, kcol)),
                  pl.BlockSpec((rows, wkv), lambda i: (i, vcol)),
                  pl.BlockSpec((L, wkv), prev_map(kcol)),
                  pl.BlockSpec((L, wkv), prev_map(vcol))],
        out_specs=pl.BlockSpec((rows, wq), lambda i: (i, 0)),
        out_shape=jax.ShapeDtypeStruct((T, wq), BF16),
        compiler_params=_cparams("parallel"),
        name="swa_attention",
    )(sinks, qkv, qkv, qkv, qkv, qkv)


def _dn_prep_kernel(cur_ref, prev_ref, cw_ref, t2_ref, alog_ref, dtb_ref, qkv_ref, gb_ref, *, tiles_per_seq, tm):
    H, dh, W = DN_HEADS, DN_HEAD_DIM, DN_CONV
    first = (pl.program_id(0) % tiles_per_seq) == 0
    prev = prev_ref[...].astype(F32)
    prev = jnp.where(first, 0.0, prev)
    xcat = jnp.concatenate([prev, cur_ref[...].astype(F32)], axis=0)
    P = prev.shape[0]
    cw = cw_ref[...]
    y = None
    for j in range(W):
        shift = W - 1 - j
        rolled = xcat if shift == 0 else pltpu.roll(xcat, shift, axis=0)
        term = rolled[P:P + tm] * cw[j:j + 1]
        y = term if y is None else y + term
    y = y * _sigmoid(y)
    width = H * dh
    for h in range(H):
        qh = y[:, h * dh:(h + 1) * dh]
        kh = y[:, width + h * dh:width + (h + 1) * dh]
        qn = qh * lax.rsqrt(jnp.sum(qh * qh, axis=-1, keepdims=True) + 1e-6) * (dh ** -0.5)
        kn = kh * lax.rsqrt(jnp.sum(kh * kh, axis=-1, keepdims=True) + 1e-6)
        qkv_ref[:, h * dh:(h + 1) * dh] = qn.astype(qkv_ref.dtype)
        qkv_ref[:, width + h * dh:width + (h + 1) * dh] = kn.astype(qkv_ref.dtype)
    qkv_ref[:, 2 * width:] = y[:, 2 * width:].astype(qkv_ref.dtype)

    t2 = t2_ref[...]
    xs = t2 + dtb_ref[...]
    softplus = jnp.maximum(xs, 0.0) + jnp.log(1.0 + jnp.exp(-jnp.abs(xs)))
    g = -jnp.exp(alog_ref[...]) * softplus
    ri = lax.broadcasted_iota(jnp.int32, (tm, tm), 0)
    ci = lax.broadcasted_iota(jnp.int32, (tm, tm), 1)
    same_chunk = (ri // DN_CHUNK) == (ci // DN_CHUNK)
    tri = jnp.where(jnp.logical_and(same_chunk, ri >= ci), 1.0, 0.0)
    gc = jnp.dot(tri, g, preferred_element_type=F32, precision=lax.Precision.HIGHEST)
    lane = lax.broadcasted_iota(jnp.int32, t2.shape, 1)
    gb_ref[...] = jnp.where(lane < H, _sigmoid(t2), gc)


def dn_prep(proj_b, proj_small, conv_w, alog_vec, dtb_vec, layer, seq_len):
    T = proj_b.shape[0]
    width3 = 3 * DN_HEADS * DN_HEAD_DIM
    tm = _tile(seq_len, 256)
    P = 16
    tail2_col = proj_small.shape[1] // LANES - 1
    return pl.pallas_call(
        functools.partial(_dn_prep_kernel, tiles_per_seq=seq_len // tm, tm=tm),
        grid=(T // tm,),
        in_specs=[pl.BlockSpec((tm, width3), lambda i: (i, 0)),
                  pl.BlockSpec((P, width3), lambda i: (jnp.maximum(i * (tm // P) - 1, 0), 0)),
                  pl.BlockSpec((None, DN_CONV, width3), lambda i: (layer, 0, 0)),
                  pl.BlockSpec((tm, LANES), lambda i: (i, tail2_col)),
                  pl.BlockSpec((None, 1, LANES), lambda i: (layer, 0, 0)),
                  pl.BlockSpec((None, 1, LANES), lambda i: (layer, 0, 0))],
        out_specs=[pl.BlockSpec((tm, width3), lambda i: (i, 0)),
                   pl.BlockSpec((tm, LANES), lambda i: (i, 0))],
        out_shape=[jax.ShapeDtypeStruct((T, width3), BF16),
                   jax.ShapeDtypeStruct((T, LANES), F32)],
        compiler_params=_cparams("parallel"),
        name="dn_prep",
    )(proj_b, proj_b, conv_w, proj_small, alog_vec, dtb_vec)


def _unit_lower_inverse_many(ls):
    C = ls[0].shape[0]
    assert C == 64
    ri = lax.broadcasted_iota(jnp.int32, (C, C), 0)
    ci = lax.broadcasted_iota(jnp.int32, (C, C), 1)
    eye = jnp.where(ri == ci, 1.0, 0.0)

    def same_block(size):
        return (ri // size) == (ci // size)

    def mm(a, b):
        return [_dot(x.astype(BF16), y.astype(BF16)) for x, y in zip(a, b)]

    d = [jnp.where(same_block(8), l, 0.0) for l in ls]
    d2 = mm(d, d)
    d4 = mm(d2, d2)
    d3 = mm(d, d2)
    x1 = [b - a - c for a, b, c in zip(d, d2, d3)]
    x1d4 = mm(x1, d4)
    t = [eye + a + b + c for a, b, c in zip(x1, d4, x1d4)]
    for size in (8, 16, 32):
        joins = jnp.logical_and(same_block(2 * size), jnp.logical_not(same_block(size)))
        b = [jnp.where(joins, l, 0.0) for l in ls]
        tbt = mm(mm(t, b), t)
        t = [a - c for a, c in zip(t, tbt)]
    return t


def _dn_chunk_kernel(qkv_ref, z_ref, gb_ref, grow_ref, norm_ref, o_ref,
                     state_ref, u_ref, w_ref, qe_ref, kd_ref, qk_ref, *, rows):
    H, dh, C = DN_HEADS, DN_HEAD_DIM, DN_CHUNK
    width = H * dh
    heads = range(H)

    @pl.when(pl.program_id(1) == 0)
    def _():
        state_ref[...] = jnp.zeros_like(state_ref)

    ri = lax.broadcasted_iota(jnp.int32, (C, C), 0)
    ci = lax.broadcasted_iota(jnp.int32, (C, C), 1)
    incl = ri >= ci
    strict = ri > ci
    gain = norm_ref[...]

    def prepare(c2, carry):
        r0s = [pl.multiple_of((par * c2 + j) * C, C) for j in range(par)]
        gbs = [gb_ref[pl.ds(r0, C), :] for r0 in r0s]
        grows = [grow_ref[par * c2 + j] for j in range(par)]
        probs = [(j, h) for j in range(par) for h in heads]
        q = [qkv_ref[pl.ds(r0s[j], C), h * dh:(h + 1) * dh] for j, h in probs]
        k = [qkv_ref[pl.ds(r0s[j], C), width + h * dh:width + (h + 1) * dh] for j, h in probs]
        v = [qkv_ref[pl.ds(r0s[j], C), 2 * width + h * dh:2 * width + (h + 1) * dh] for j, h in probs]
        beta = [gbs[j][:, h:h + 1] for j, h in probs]
        gcol = [gbs[j][:, H + h:H + h + 1] for j, h in probs]
        np_ = range(len(probs))
        decay = [jnp.exp(jnp.where(incl, gcol[p] - grows[j][h:h + 1, :], NEG_INF)) for p, (j, h) in enumerate(probs)]
        kf = [k[p].astype(F32) for p in np_]
        kbeta = [kf[p] * beta[p] for p in np_]
        eg = [jnp.exp(gcol[p]) for p in np_]
        kk = [_dot_nt(kbeta[p].astype(BF16), k[p]) for p in np_]
        qk = [_dot_nt(q[p], k[p]) for p in np_]
        tmat = _unit_lower_inverse_many([jnp.where(strict, kk[p] * decay[p], 0.0) for p in np_])
        tb = [t.astype(BF16) for t in tmat]
        u = [_dot(tb[p], (v[p].astype(F32) * beta[p]).astype(BF16)) for p in np_]
        w = [_dot(tb[p], (kbeta[p] * eg[p]).astype(BF16)) for p in np_]
        for p, (j, h) in enumerate(probs):
            r0 = r0s[j]
            g_last = gcol[p][C - 1:C, :]
            u_ref[h, pl.ds(r0, C), :] = u[p]
            w_ref[h, pl.ds(r0, C), :] = w[p].astype(BF16)
            qe_ref[h, pl.ds(r0, C), :] = (q[p].astype(F32) * eg[p]).astype(BF16)
            kd_ref[h, pl.ds(r0, C), :] = (kf[p] * jnp.exp(g_last - gcol[p])).astype(BF16)
            qk_ref[h, pl.ds(r0, C), :] = jnp.where(incl, qk[p] * decay[p], 0.0).astype(BF16)
        return carry

    par = 4 if (rows // C) % 4 == 0 else 1
    lax.fori_loop(0, rows // (C * par), prepare, 0)

    def scan(c, carry):
        r0 = pl.multiple_of(c * C, C)
        gb = gb_ref[pl.ds(r0, C), :]
        state = [state_ref[h] for h in heads]
        sb = [s.astype(BF16) for s in state]
        w_s = [_dot(w_ref[h, pl.ds(r0, C), :], sb[h]) for h in heads]
        q_s = [_dot(qe_ref[h, pl.ds(r0, C), :], sb[h]) for h in heads]
        vnb = [(u_ref[h, pl.ds(r0, C), :] - w_s[h]).astype(BF16) for h in heads]
        o_in = [_dot(qk_ref[h, pl.ds(r0, C), :], vnb[h]) for h in heads]
        kv = [_dot_tn(kd_ref[h, pl.ds(r0, C), :], vnb[h]) for h in heads]
        for h in heads:
            g_last = gb[C - 1:C, H + h:H + h + 1]
            state_ref[h] = state[h] * jnp.exp(g_last) + kv[h]
            z = z_ref[pl.ds(r0, C), h * dh:(h + 1) * dh].astype(F32)
            y = _rms(q_s[h] + o_in[h], gain) * (z * _sigmoid(z))
            o_ref[pl.ds(r0, C), h * dh:(h + 1) * dh] = y.astype(o_ref.dtype)
        return carry

    lax.fori_loop(0, rows // C, scan, 0, unroll=2)


def dn_chunked(qkvn, proj_b, gb, grow, dn_norm, layer, batch, seq_len):
    T = qkvn.shape[0]
    H, dh, C = DN_HEADS, DN_HEAD_DIM, DN_CHUNK
    width = H * dh
    rows = _tile(seq_len, 512)
    tps = seq_len // rows
    zcol = 3
    return pl.pallas_call(
        functools.partial(_dn_chunk_kernel, rows=rows),
        grid=(batch, tps),
        in_specs=[pl.BlockSpec((rows, 3 * width), lambda b, n: (b * tps + n, 0)),
                  pl.BlockSpec((rows, width), lambda b, n: (b * tps + n, zcol)),
                  pl.BlockSpec((rows, LANES), lambda b, n: (b * tps + n, 0)),
                  pl.BlockSpec((rows // C, H, C), lambda b, n: (b * tps + n, 0, 0)),
                  pl.BlockSpec((None, 1, dh), lambda b, n: (layer, 0, 0))],
        out_specs=pl.BlockSpec((rows, width), lambda b, n: (b * tps + n, 0)),
        out_shape=jax.ShapeDtypeStruct((T, width), BF16),
        scratch_shapes=[pltpu.VMEM((H, dh, dh), F32),
                        pltpu.VMEM((H, rows, dh), F32),
                        pltpu.VMEM((H, rows, dh), BF16),
                        pltpu.VMEM((H, rows, dh), BF16),
                        pltpu.VMEM((H, rows, dh), BF16),
                        pltpu.VMEM((H, rows, C), BF16)],
        compiler_params=_cparams("parallel", "arbitrary"),
        name="dn_chunked",
    )(qkvn, proj_b, gb, grow, dn_norm)


def _rotate_pairs(x, cos_tab, sin_tab):
    return x * cos_tab + pltpu.roll(x, LANES // 2, axis=1) * sin_tab


def _mla_q_kernel(c_ref, g_ref, w_ref, cos_ref, sin_ref, o_ref):
    H = MLA_HEADS
    scale = (MLA_NOPE + MLA_ROPE) ** -0.5 * float(np.log2(np.e))
    cn = _rms(c_ref[...], g_ref[...]).astype(BF16)
    acc = _dot(cn, w_ref[...]) * scale
    cos_tab, sin_tab = cos_ref[...], sin_ref[...]
    for h in range(H):
        o_ref[:, 2 * h * LANES:(2 * h + 1) * LANES] = acc[:, h * LANES:(h + 1) * LANES].astype(o_ref.dtype)
        xr = acc[:, (H + h) * LANES:(H + h + 1) * LANES]
        o_ref[:, (2 * h + 1) * LANES:(2 * h + 2) * LANES] = _rotate_pairs(xr, cos_tab, sin_tab).astype(o_ref.dtype)


def mla_q(proj_small, gains, wq, cos_tab, sin_tab, layer):
    T = proj_small.shape[0]
    R = wq.shape[1]
    tm = _tile(T, 512)
    N = wq.shape[2]
    return pl.pallas_call(
        _mla_q_kernel,
        grid=(T // tm,),
        in_specs=[pl.BlockSpec((tm, R), lambda i: (i, 0)),
                  pl.BlockSpec((None, 1, R), lambda i: (layer, 0, 0)),
                  pl.BlockSpec((None, R, N), lambda i: (layer, 0, 0)),
                  pl.BlockSpec((tm, LANES), lambda i: (i, 0)),
                  pl.BlockSpec((tm, LANES), lambda i: (i, 0))],
        out_specs=pl.BlockSpec((tm, N), lambda i: (i, 0)),
        out_shape=jax.ShapeDtypeStruct((T, N), BF16),
        compiler_params=_cparams("parallel"),
        name="mla_q",
    )(proj_small, gains, wq, cos_tab, sin_tab)


def _mla_kv_kernel(c_ref, kr_ref, g_ref, w_ref, cos_ref, sin_ref, k_ref, v_ref):
    H = MLA_HEADS
    cn = _rms(c_ref[...], g_ref[...]).astype(BF16)
    acc = _dot(cn, w_ref[...])
    kr = _rotate_pairs(kr_ref[...], cos_ref[...], sin_ref[...]).astype(k_ref.dtype)
    for h in range(H):
        k_ref[:, 2 * h * LANES:(2 * h + 1) * LANES] = acc[:, h * LANES:(h + 1) * LANES].astype(k_ref.dtype)
        k_ref[:, (2 * h + 1) * LANES:(2 * h + 2) * LANES] = kr
    v_ref[...] = acc[:, H * LANES:].astype(v_ref.dtype)


def mla_kv(proj_small, gains, wkv, cos_tab, sin_tab, layer):
    T = proj_small.shape[0]
    R = wkv.shape[1]
    tm = _tile(T, 512)
    N = wkv.shape[2]
    H = MLA_HEADS
    kr_col = (2 * R) // LANES
    return pl.pallas_call(
        _mla_kv_kernel,
        grid=(T // tm,),
        in_specs=[pl.BlockSpec((tm, R), lambda i: (i, 1)),
                  pl.BlockSpec((tm, LANES), lambda i: (i, kr_col)),
                  pl.BlockSpec((None, 1, R), lambda i: (layer, 0, 0)),
                  pl.BlockSpec((None, R, N), lambda i: (layer, 0, 0)),
                  pl.BlockSpec((tm, LANES), lambda i: (i, 0)),
                  pl.BlockSpec((tm, LANES), lambda i: (i, 0))],
        out_specs=[pl.BlockSpec((tm, 2 * H * LANES), lambda i: (i, 0)),
                   pl.BlockSpec((tm, H * MLA_V), lambda i: (i, 0))],
        out_shape=[jax.ShapeDtypeStruct((T, 2 * H * LANES), BF16),
                   jax.ShapeDtypeStruct((T, H * MLA_V), BF16)],
        compiler_params=_cparams("parallel"),
        name="mla_kv",
    )(proj_small, proj_small, gains, wkv, cos_tab, sin_tab)


def _flash_kernel(qi_ref, kj_ref, q_ref, k_ref, v_ref, o_ref, m_ref, acc_ref, *, tq, tk):
    H, dv = MLA_HEADS, MLA_V
    dqk = 2 * LANES
    qi = qi_ref[pl.program_id(1)]
    kj = kj_ref[pl.program_id(1)]

    @pl.when(kj == 0)
    def _():
        m_ref[...] = jnp.full_like(m_ref, NEG_INF)
        acc_ref[...] = jnp.zeros_like(acc_ref)

    def step(masked):
        if masked:
            row = lax.broadcasted_iota(jnp.int32, (tq, tk), 0)
            col = lax.broadcasted_iota(jnp.int32, (tq, tk), 1)
            keep = col <= row
        ones = jnp.ones((tk, dv), BF16)
        scores = [_dot_nt(q_ref[:, h * dqk:(h + 1) * dqk], k_ref[:, h * dqk:(h + 1) * dqk]) for h in range(H)]
        probs, alphas = [], []
        for h in range(H):
            s = scores[h]
            if masked:
                s = jnp.where(keep, s, NEG_INF)
            m_prev = m_ref[h]
            m_new = jnp.maximum(m_prev, jnp.max(s, axis=-1, keepdims=True))
            alpha = jnp.exp2(m_prev - m_new)
            p = jnp.exp2(s - _lane_repeat(m_new, tk // LANES))
            m_ref[h] = m_new
            alphas.append(alpha)
            probs.append(p.astype(BF16))
        for h in range(H):
            v_ext = jnp.concatenate([v_ref[:, h * dv:(h + 1) * dv], ones], axis=1)
            acc_ref[h] = _lane_repeat(alphas[h], 2) * acc_ref[h] + _dot(probs[h], v_ext)

    @pl.when(kj < qi)
    def _():
        step(False)

    @pl.when(kj == qi)
    def _():
        step(True)
        for h in range(H):
            acc = acc_ref[h]
            o_ref[:, h * dv:(h + 1) * dv] = (acc[:, :dv] / acc[:, dv:]).astype(o_ref.dtype)


def mla_flash(qf, kf, v, batch, seq_len):
    T = qf.shape[0]
    H, dv = MLA_HEADS, MLA_V
    t = _tile(seq_len, 512)
    nq = seq_len // t
    pairs = [(i, j) for i in range(nq) for j in range(i + 1)]
    qi_tab = jnp.asarray([i for i, _ in pairs], jnp.int32)
    kj_tab = jnp.asarray([j for _, j in pairs], jnp.int32)
    return pl.pallas_call(
        functools.partial(_flash_kernel, tq=t, tk=t),
        grid_spec=pltpu.PrefetchScalarGridSpec(
            num_scalar_prefetch=2,
            grid=(batch, len(pairs)),
            in_specs=[pl.BlockSpec((t, qf.shape[1]), lambda b, n, qi, kj: (b * nq + qi[n], 0)),
                      pl.BlockSpec((t, kf.shape[1]), lambda b, n, qi, kj: (b * nq + kj[n], 0)),
                      pl.BlockSpec((t, v.shape[1]), lambda b, n, qi, kj: (b * nq + kj[n], 0))],
            out_specs=pl.BlockSpec((t, H * dv), lambda b, n, qi, kj: (b * nq + qi[n], 0)),
            scratch_shapes=[pltpu.VMEM((H, t, LANES), F32),
                            pltpu.VMEM((H, t, 2 * dv), F32)]),
        out_shape=jax.ShapeDtypeStruct((T, H * dv), BF16),
        compiler_params=_cparams("parallel", "arbitrary"),
        name="mla_flash",
    )(qi_tab, kj_tab, qf, kf, v)


def _merge_kernel(ya_ref, yb_ref, yc_ref, ga_ref, gb_ref, gc_ref, wa_ref, wb_ref, wc_ref, o_ref, sa_ref, sb_ref, sc_ref):
    @pl.when(pl.program_id(1) == 0)
    def _():
        sa_ref[...] = wa_ref[...].astype(BF16)
        sb_ref[...] = wb_ref[...].astype(BF16)
        sc_ref[...] = wc_ref[...].astype(BF16)

    acc = ga_ref[...].astype(F32) * _dot(ya_ref[...], sa_ref[...])
    acc = acc + gb_ref[...].astype(F32) * _dot(yb_ref[...], sb_ref[...])
    acc = acc + gc_ref[...].astype(F32) * _dot(yc_ref[...], sc_ref[...])
    o_ref[...] = acc.astype(o_ref.dtype)


def merge_branches(ya, yb, yc, gates, w_branch, layer):
    T, Wb = ya.shape
    D = w_branch.shape[-1]
    tm = _tile(T, 1024)
    tn = _tile(D, 512)
    nj = D // tn
    y_spec = pl.BlockSpec((tm, Wb), lambda j, i: (i, 0))

    def g_spec(n):
        return pl.BlockSpec((tm, tn), lambda j, i: (i, n * nj + j))

    def w_spec(n):
        return pl.BlockSpec((None, None, Wb, tn), lambda j, i: (layer, n, 0, j))

    return pl.pallas_call(
        _merge_kernel,
        grid=(nj, T // tm),
        in_specs=[y_spec, y_spec, y_spec, g_spec(0), g_spec(1), g_spec(2), w_spec(0), w_spec(1), w_spec(2)],
        out_specs=pl.BlockSpec((tm, tn), lambda j, i: (i, j)),
        out_shape=jax.ShapeDtypeStruct((T, D), BF16),
        scratch_shapes=[pltpu.VMEM((Wb, tn), BF16)] * 3,
        compiler_params=_cparams("arbitrary", "arbitrary"),
        name="merge_branches",
    )(ya, yb, yc, gates, gates, gates, w_branch, w_branch, w_branch)


def _out_proj_kernel(m_ref, w_ref, x_ref, g_ref, xo_ref, *maybe_ho_ref):
    xn = x_ref[...] + _dot(m_ref[...], w_ref[...])
    xo_ref[...] = xn
    for ho_ref in maybe_ho_ref:
        ho_ref[...] = _rms(xn, g_ref[...]).astype(ho_ref.dtype)


def out_proj_residual(merged, w_out_bf, x, norm_gain, layer, emit_norm):
    T, D = x.shape
    tm = _tile(T, 512)
    row_spec = pl.BlockSpec((tm, D), lambda i: (i, 0))
    out_specs = [row_spec, row_spec] if emit_norm else [row_spec]
    out_shape = [jax.ShapeDtypeStruct((T, D), F32)] + ([jax.ShapeDtypeStruct((T, D), BF16)] if emit_norm else [])
    return pl.pallas_call(
        _out_proj_kernel,
        grid=(T // tm,),
        in_specs=[row_spec,
                  pl.BlockSpec((None, D, D), lambda i: (layer, 0, 0)),
                  row_spec,
                  pl.BlockSpec((None, 1, D), lambda i: (layer, 0, 0))],
        out_specs=out_specs,
        out_shape=out_shape,
        compiler_params=_cparams("parallel"),
        name="out_proj_residual",
    )(merged, w_out_bf, x, norm_gain)


def _ffn_kernel(h_ref, wg_ref, wu_ref, wd_ref, y_ref, acc_ref):
    f = pl.program_id(1)

    @pl.when(f == 0)
    def _():
        acc_ref[...] = jnp.zeros_like(acc_ref)

    h = h_ref[...]
    gate = _dot(h, wg_ref[...])
    up = _dot(h, wu_ref[...])
    act = (gate * _sigmoid(gate) * up).astype(BF16)
    acc_ref[...] += _dot(act, wd_ref[...])

    @pl.when(f == pl.num_programs(1) - 1)
    def _():
        y_ref[...] = acc_ref[...].astype(y_ref.dtype)


def ffn_swiglu(h, wg, wu, wd, widx):
    T, D = h.shape
    F = wg.shape[-1]
    tm = _tile(T, 512)
    tf = _tile(F, 1024)
    return pl.pallas_call(
        _ffn_kernel,
        grid=(T // tm, F // tf),
        in_specs=[pl.BlockSpec((tm, D), lambda i, f: (i, 0)),
                  pl.BlockSpec((None, D, tf), lambda i, f: (widx, 0, f)),
                  pl.BlockSpec((None, D, tf), lambda i, f: (widx, 0, f)),
                  pl.BlockSpec((None, tf, D), lambda i, f: (widx, f, 0))],
        out_specs=pl.BlockSpec((tm, D), lambda i, f: (i, 0)),
        out_shape=jax.ShapeDtypeStruct((T, D), BF16),
        scratch_shapes=[pltpu.VMEM((tm, D), F32)],
        compiler_params=_cparams("parallel", "arbitrary"),
        name="ffn_swiglu",
    )(h, wg, wu, wd)


def _router_kernel(x_ref, g_ref, w_ref, o_ref):
    h = _rms(x_ref[...], g_ref[...])
    logits = jnp.dot(h, w_ref[...], preferred_element_type=F32, precision=lax.Precision.HIGHEST)
    lane = lax.broadcasted_iota(jnp.int32, logits.shape, 1)
    lanef = lane.astype(F32)
    big = float(LANES)
    logits = jnp.where(lane < N_EXPERTS, logits, NEG_INF)
    l1 = jnp.max(logits, axis=-1, keepdims=True)
    i1 = jnp.min(jnp.where(logits == l1, lanef, big), axis=-1, keepdims=True)
    rest = jnp.where(lanef == i1, NEG_INF, logits)
    l2 = jnp.max(rest, axis=-1, keepdims=True)
    i2 = jnp.min(jnp.where(rest == l2, lanef, big), axis=-1, keepdims=True)
    e2 = jnp.exp(l2 - l1)
    w1 = 1.0 / (1.0 + e2)
    w2 = e2 / (1.0 + e2)
    out = jnp.where(lane == 0, i1, jnp.where(lane == 1, i2, jnp.where(lane == 2, w1, jnp.where(lane == 3, w2, 0.0))))
    o_ref[...] = out


def moe_router(x, norm_gain, w_router_pad, layer, widx):
    T, D = x.shape
    tm = _tile(T, 512)
    return pl.pallas_call(
        _router_kernel,
        grid=(T // tm,),
        in_specs=[pl.BlockSpec((tm, D), lambda i: (i, 0)),
                  pl.BlockSpec((None, 1, D), lambda i: (layer, 0, 0)),
                  pl.BlockSpec((None, D, LANES), lambda i: (widx, 0, 0))],
        out_specs=pl.BlockSpec((tm, LANES), lambda i: (i, 0)),
        out_shape=jax.ShapeDtypeStruct((T, LANES), F32),
        compiler_params=_cparams("parallel"),
        name="moe_router",
    )(x, norm_gain, w_router_pad)


def _start_row_gather(idx_hbm, src_hbm, blk, slot, idx_smem, rows_vmem, idx_sem, row_sem, n):
    idx_copy = pltpu.make_async_copy(idx_hbm.at[blk], idx_smem.at[slot], idx_sem.at[slot])
    idx_copy.start()
    idx_copy.wait()

    def issue(i, c):
        for priority in range(2):
            r = 2 * i + priority
            pltpu.make_async_copy(src_hbm.at[pl.ds(idx_smem[slot, r], 1)],
                                  rows_vmem.at[slot, pl.ds(r, 1)],
                                  row_sem.at[slot]).start(priority=priority)
        return c

    lax.fori_loop(0, n // 2, issue, 0, unroll=4)


def _wait_row_gather(src_hbm, slot, rows_vmem, row_sem, n):
    pltpu.make_async_copy(src_hbm.at[pl.ds(0, n)], rows_vmem.at[slot], row_sem.at[slot]).wait()


def _gather_rows_kernel(meta_ref, idx_hbm, x_hbm, g_ref, o_ref, idx_smem, rows_vmem, idx_sem, row_sem, *, bm):
    m = pl.program_id(0)
    n_used = meta_ref[pl.num_programs(0)]
    slot = m % 2
    start = functools.partial(_start_row_gather, idx_hbm, x_hbm, idx_smem=idx_smem, rows_vmem=rows_vmem,
                              idx_sem=idx_sem, row_sem=row_sem, n=bm)

    @pl.when(m == 0)
    def _():
        start(blk=0, slot=0)

    @pl.when(m + 1 < n_used)
    def _():
        start(blk=m + 1, slot=1 - slot)

    @pl.when(m < n_used)
    def _():
        _wait_row_gather(x_hbm, slot, rows_vmem, row_sem, bm)
        o_ref[...] = _rms(rows_vmem[slot], g_ref[...]).astype(o_ref.dtype)

    @pl.when(m >= n_used)
    def _():
        o_ref[...] = jnp.zeros_like(o_ref)


def moe_gather_norm(x, row_tok, meta, norm_gain, layer, bm):
    T, D = x.shape
    n_blocks = row_tok.shape[0]
    return pl.pallas_call(
        functools.partial(_gather_rows_kernel, bm=bm),
        grid_spec=pltpu.PrefetchScalarGridSpec(
            num_scalar_prefetch=1,
            grid=(n_blocks,),
            in_specs=[pl.BlockSpec(memory_space=pl.ANY),
                      pl.BlockSpec(memory_space=pl.ANY),
                      pl.BlockSpec((None, 1, D), lambda i, meta: (layer, 0, 0))],
            out_specs=pl.BlockSpec((bm, D), lambda i, meta: (i, 0)),
            scratch_shapes=[pltpu.SMEM((2, bm), jnp.int32),
                            pltpu.VMEM((2, bm, D), F32),
                            pltpu.SemaphoreType.DMA((2,)),
                            pltpu.SemaphoreType.DMA((2,))]),
        out_shape=jax.ShapeDtypeStruct((n_blocks * bm, D), BF16),
        compiler_params=_cparams("arbitrary"),
        name="moe_gather_norm",
    )(meta, row_tok, x, norm_gain)


def _expert_up_kernel(meta_ref, x_ref, wg_ref, wu_ref, o_ref, wgb_ref, wub_ref):
    m = pl.program_id(1)
    n_used = meta_ref[pl.num_programs(1)]
    new_expert = jnp.logical_or(m == 0, meta_ref[m] != meta_ref[jnp.maximum(m - 1, 0)])

    @pl.when(jnp.logical_and(new_expert, m < n_used))
    def _():
        wgb_ref[...] = wg_ref[...].astype(BF16)
        wub_ref[...] = wu_ref[...].astype(BF16)

    @pl.when(m < n_used)
    def _():
        x = x_ref[...]
        gate = _dot(x, wgb_ref[...])
        up = _dot(x, wub_ref[...])
        o_ref[...] = (gate * _sigmoid(gate) * up).astype(o_ref.dtype)

    @pl.when(m >= n_used)
    def _():
        o_ref[...] = jnp.zeros_like(o_ref)


def _last_used(m, meta, nb):
    return jnp.minimum(m, meta[nb] - 1)


def moe_expert_up(xs, meta, wg, wu, widx, bm):
    R, D = xs.shape
    F = wg.shape[-1]
    tf = _tile(F, 1024)
    nb = R // bm

    def w_map(f, m, meta):
        return (widx, meta[_last_used(m, meta, nb)], 0, f)

    return pl.pallas_call(
        _expert_up_kernel,
        grid_spec=pltpu.PrefetchScalarGridSpec(
            num_scalar_prefetch=1,
            grid=(F // tf, nb),
            in_specs=[pl.BlockSpec((bm, D), lambda f, m, meta: (_last_used(m, meta, nb), 0)),
                      pl.BlockSpec((None, None, D, tf), w_map),
                      pl.BlockSpec((None, None, D, tf), w_map)],
            out_specs=pl.BlockSpec((bm, tf), lambda f, m, meta: (m, f)),
            scratch_shapes=[pltpu.VMEM((D, tf), BF16), pltpu.VMEM((D, tf), BF16)]),
        out_shape=jax.ShapeDtypeStruct((R, F), BF16),
        compiler_params=_cparams("arbitrary", "arbitrary"),
        name="moe_expert_up",
    )(meta, xs, wg, wu)


def _expert_down_kernel(meta_ref, a_ref, wd_ref, o_ref, wdb_ref):
    m = pl.program_id(1)
    n_used = meta_ref[pl.num_programs(1)]
    new_expert = jnp.logical_or(m == 0, meta_ref[m] != meta_ref[jnp.maximum(m - 1, 0)])

    @pl.when(jnp.logical_and(new_expert, m < n_used))
    def _():
        wdb_ref[...] = wd_ref[...].astype(BF16)

    @pl.when(m < n_used)
    def _():
        o_ref[...] = _dot(a_ref[...], wdb_ref[...]).astype(o_ref.dtype)

    @pl.when(m >= n_used)
    def _():
        o_ref[...] = jnp.zeros_like(o_ref)


def moe_expert_down(act, meta, wd, widx, bm):
    R, F = act.shape
    D = wd.shape[-1]
    tn = _tile(D, 512)
    nb = R // bm
    return pl.pallas_call(
        _expert_down_kernel,
        grid_spec=pltpu.PrefetchScalarGridSpec(
            num_scalar_prefetch=1,
            grid=(D // tn, nb),
            in_specs=[pl.BlockSpec((bm, F), lambda j, m, meta: (_last_used(m, meta, nb), 0)),
                      pl.BlockSpec((None, None, F, tn),
                                   lambda j, m, meta: (widx, meta[_last_used(m, meta, nb)], 0, j))],
            out_specs=pl.BlockSpec((bm, tn), lambda j, m, meta: (m, j)),
            scratch_shapes=[pltpu.VMEM((F, tn), BF16)]),
        out_shape=jax.ShapeDtypeStruct((R, D), F32),
        compiler_params=_cparams("arbitrary", "arbitrary"),
        name="moe_expert_down",
    )(meta, act, wd)


def _combine_kernel(dest_hbm, yb_hbm, rt_ref, y_ref, idx_smem, rows_vmem, idx_sem, row_sem, *, tm):
    m = pl.program_id(0)
    slot = m % 2
    n = TOP_K * tm
    start = functools.partial(_start_row_gather, dest_hbm, yb_hbm, idx_smem=idx_smem, rows_vmem=rows_vmem,
                              idx_sem=idx_sem, row_sem=row_sem, n=n)

    @pl.when(m == 0)
    def _():
        start(blk=0, slot=0)

    @pl.when(m + 1 < pl.num_programs(0))
    def _():
        start(blk=m + 1, slot=1 - slot)

    _wait_row_gather(yb_hbm, slot, rows_vmem, row_sem, n)
    rt = rt_ref[...]
    y = rt[:, 2:3] * rows_vmem[slot, 0:tm, :] + rt[:, 3:4] * rows_vmem[slot, tm:2 * tm, :]
    y_ref[...] = y.astype(y_ref.dtype)


def moe_combine(yb, dest_blocks, route, tm):
    T = route.shape[0]
    D = yb.shape[1]
    return pl.pallas_call(
        functools.partial(_combine_kernel, tm=tm),
        grid=(T // tm,),
        in_specs=[pl.BlockSpec(memory_space=pl.ANY),
                  pl.BlockSpec(memory_space=pl.ANY),
                  pl.BlockSpec((tm, LANES), lambda i: (i, 0))],
        out_specs=pl.BlockSpec((tm, D), lambda i: (i, 0)),
        out_shape=jax.ShapeDtypeStruct((T, D), BF16),
        scratch_shapes=[pltpu.SMEM((2, TOP_K * tm), jnp.int32),
                        pltpu.VMEM((2, TOP_K * tm, D), F32),
                        pltpu.SemaphoreType.DMA((2,)),
                        pltpu.SemaphoreType.DMA((2,))],
        compiler_params=_cparams("arbitrary"),
        name="moe_combine",
    )(dest_blocks, yb, route)


def moe_layer(x, norm_ffn, w_router_pad, wg, wu, wd, layer, widx):
    T, D = x.shape
    E = N_EXPERTS
    bm = _tile(T, 512)
    route = moe_router(x, norm_ffn, w_router_pad, layer, widx)
    flat_e = route[:, :TOP_K].astype(jnp.int32).reshape(-1)
    onehot = (flat_e[:, None] == jnp.arange(E, dtype=jnp.int32)[None, :]).astype(jnp.int32)
    incl = jnp.cumsum(onehot, axis=0)
    counts = incl[-1]
    rank = jnp.sum((incl - onehot) * onehot, axis=1)
    padded = (counts + bm - 1) // bm * bm
    pad_end = jnp.cumsum(padded)
    pad_start = pad_end - padded
    dest = pad_start[flat_e] + rank
    n_rows = -(-(T * TOP_K + E * (bm - 1)) // bm) * bm
    n_blocks = n_rows // bm
    flat_tok = jnp.repeat(jnp.arange(T, dtype=jnp.int32), TOP_K)
    row_tok = jnp.zeros((n_rows,), jnp.int32).at[dest].set(flat_tok)
    block_e = jnp.minimum(
        jnp.sum(jnp.arange(n_blocks, dtype=jnp.int32)[:, None] * bm >= pad_end[None, :], axis=1), E - 1
    ).astype(jnp.int32)
    meta = jnp.concatenate([block_e, (pad_end[-1:] // bm).astype(jnp.int32)])

    xs = moe_gather_norm(x, row_tok.reshape(n_blocks, bm), meta, norm_ffn, layer, bm)
    act = moe_expert_up(xs, meta, wg, wu, widx, bm)
    yb = moe_expert_down(act, meta, wd, widx, bm)
    tmc = _tile(T, 256)
    dest_blocks = dest.reshape(T // tmc, tmc, TOP_K).transpose(0, 2, 1).reshape(T // tmc, TOP_K * tmc)
    return moe_combine(yb, dest_blocks, route, tmc)


def _ple_kernel(x_ref, y_ref, p_ref, wg_ref, wp_ref, gp_ref, gn_ref, xo_ref, ho_ref):
    x = x_ref[...] + y_ref[...].astype(F32)
    hp = _rms(x, gp_ref[...]).astype(BF16)
    gate = _sigmoid(_dot(hp, wg_ref[...]))
    emb = _dot(p_ref[...].astype(BF16), wp_ref[...])
    xn = x + emb * gate
    xo_ref[...] = xn
    ho_ref[...] = _rms(xn, gn_ref[...]).astype(ho_ref.dtype)


def ple_layer(x, y, p, w_gate_bf, w_proj_bf, ple_gain, next_gain, layer, gain_idx, out_dtype):
    T, D = x.shape
    Pd = p.shape[-1]
    tm = _tile(T, 512)
    row_spec = pl.BlockSpec((tm, D), lambda i: (i, 0))
    return pl.pallas_call(
        _ple_kernel,
        grid=(T // tm,),
        in_specs=[row_spec,
                  row_spec,
                  pl.BlockSpec((None, tm, Pd), lambda i: (layer, i, 0)),
                  pl.BlockSpec((None, D, D), lambda i: (layer, 0, 0)),
                  pl.BlockSpec((None, Pd, D), lambda i: (layer, 0, 0)),
                  pl.BlockSpec((None, 1, D), lambda i: (layer, 0, 0)),
                  pl.BlockSpec((None, 1, D), lambda i: (gain_idx, 0, 0))],
        out_specs=[row_spec, row_spec],
        out_shape=[jax.ShapeDtypeStruct((T, D), F32), jax.ShapeDtypeStruct((T, D), out_dtype)],
        compiler_params=_cparams("parallel"),
        name="ple_layer",
    )(x, y, p, w_gate_bf, w_proj_bf, ple_gain, next_gain)


def _split_points(D):
    swa_q = SWA_Q_HEADS * SWA_HEAD_DIM
    swa_kv = SWA_KV_HEADS * SWA_HEAD_DIM
    dn_w = DN_HEADS * DN_HEAD_DIM
    return swa_q + 2 * swa_kv, swa_q + 2 * swa_kv + 4 * dn_w


def _pack_small_in_proj(w_in_t, q_lora, kv_lora):
    _, b_end = _split_points(w_in_t.shape[2])
    H = DN_HEADS
    o = b_end
    beta = w_in_t[:, o:o + H]
    decay = w_in_t[:, o + H:o + 2 * H]
    o += 2 * H
    cq = w_in_t[:, o:o + q_lora]
    o += q_lora
    ckv = w_in_t[:, o:o + kv_lora]
    o += kv_lora
    half = MLA_ROPE // 2
    kr1 = w_in_t[:, o:o + half]
    kr2 = w_in_t[:, o + half:o + 2 * half]
    o += MLA_ROPE
    pad = jnp.zeros((w_in_t.shape[0], LANES - 2 * H, w_in_t.shape[2]), w_in_t.dtype)
    packed = jnp.concatenate([cq, ckv, kr1, kr2, kr2, kr1, beta, decay, pad], axis=1)
    return packed, o


def _pack_w_uq(w_uq):
    Ld, R, _ = w_uq.shape
    H, half = MLA_HEADS, MLA_ROPE // 2
    w = w_uq.reshape(Ld, R, H, MLA_NOPE + MLA_ROPE)
    nope = w[..., :MLA_NOPE].reshape(Ld, R, H * MLA_NOPE)
    r1 = w[..., MLA_NOPE:MLA_NOPE + half]
    r2 = w[..., MLA_NOPE + half:]
    rope = jnp.concatenate([r1, r2, r2, r1], axis=-1).reshape(Ld, R, H * LANES)
    return jnp.concatenate([nope, rope], axis=-1).astype(BF16)


def _pack_w_ukv(w_ukv):
    Ld, R, _ = w_ukv.shape
    H = MLA_HEADS
    w = w_ukv.reshape(Ld, R, H, MLA_NOPE + MLA_V)
    kn = w[..., :MLA_NOPE].reshape(Ld, R, H * MLA_NOPE)
    vv = w[..., MLA_NOPE:].reshape(Ld, R, H * MLA_V)
    return jnp.concatenate([kn, vv], axis=-1).astype(BF16)


def _rope_tables(positions):
    half = MLA_ROPE // 2
    inv_freq = ROPE_THETA ** (-jnp.arange(half, dtype=F32) / half)
    ang = positions.astype(F32).reshape(-1)[:, None] * inv_freq
    cos, sin = jnp.cos(ang), jnp.sin(ang)
    zeros = jnp.zeros_like(cos)
    return (jnp.concatenate([cos, cos, zeros, zeros], axis=-1),
            jnp.concatenate([-sin, sin, zeros, zeros], axis=-1))


def _lane_vec(v, offset):
    Ld, n = v.shape
    out = jnp.zeros((Ld, 1, LANES), F32)
    return out.at[:, 0, offset:offset + n].set(v.astype(F32))


def kernel(x, p, positions, norm_mix, w_in, conv_w, dn_a_log, dn_dt_bias, dn_norm, swa_sinks, mla_q_norm, w_uq,
           mla_kv_norm, w_ukv, w_branch, w_out, norm_ffn, w_ffn_gate, w_ffn_up, w_ffn_down, w_router, w_exp_gate,
           w_exp_up, w_exp_down, norm_ple, w_ple_gate, w_ple_proj, final_norm):
    B, S, D = x.shape
    T = B * S
    depth = w_in.shape[0]
    q_lora, kv_lora = w_uq.shape[1], w_ukv.shape[1]
    a_end, b_end = _split_points(D)

    w_in_t = jnp.swapaxes(w_in, 1, 2)
    w_small_t, gate_row0 = _pack_small_in_proj(w_in_t, q_lora, kv_lora)
    wq_packed = _pack_w_uq(w_uq)
    wkv_packed = _pack_w_ukv(w_ukv)
    w_out_bf = w_out.astype(BF16)
    w_ple_gate_bf = w_ple_gate.astype(BF16)
    w_ple_proj_bf = w_ple_proj.astype(BF16)
    wfg, wfu, wfd = w_ffn_gate.astype(BF16), w_ffn_up.astype(BF16), w_ffn_down.astype(BF16)
    weg, weu, wed = w_exp_gate, w_exp_up, w_exp_down
    w_router_pad = jnp.pad(w_router, ((0, 0), (0, 0), (0, LANES - w_router.shape[-1])))
    cos_tab, sin_tab = _rope_tables(positions)
    alog_vec = _lane_vec(dn_a_log, DN_HEADS)
    dtb_vec = _lane_vec(dn_dt_bias, DN_HEADS)
    row = lambda g: g.reshape(g.shape[0], 1, g.shape[-1])
    norm_mix3, norm_ffn3, norm_ple3 = row(norm_mix), row(norm_ffn), row(norm_ple)
    dn_norm3, mla_q_norm3, mla_kv_norm3 = row(dn_norm), row(mla_q_norm), row(mla_kv_norm)
    final3 = final_norm.reshape(1, 1, D)

    xf = x.reshape(T, D)
    pf = p.reshape(depth, T, p.shape[-1])
    h = rmsnorm_rows(xf, norm_mix3, 0, BF16)
    for i in range(depth):
        proj_a = matmul_ws(h, w_in_t, i, n_cols=a_end, row_block_offset=0, tn=a_end // 2, out_dtype=BF16,
                           tm_pref=2048)
        tn_b = 512
        proj_b = matmul_ws(h, w_in_t, i, n_cols=b_end - a_end, row_block_offset=a_end // tn_b, tn=tn_b,
                           out_dtype=BF16, tm_pref=2048)
        proj_small = matmul_ws(h, w_small_t, i, n_cols=w_small_t.shape[1], row_block_offset=0,
                               tn=w_small_t.shape[1], out_dtype=F32)
        gates = matmul_ws(h, w_in_t, i, n_cols=w_in_t.shape[1] - gate_row0, row_block_offset=0, tn=1024,
                          out_dtype=BF16, act="sigmoid", row_offset=gate_row0)

        y_a = swa_attention(proj_a, swa_sinks, i, S)

        qkvn, gb = dn_prep(proj_b, proj_small, conv_w, alog_vec, dtb_vec, i, S)
        nchunks = T // DN_CHUNK
        grow = gb[:, DN_HEADS:2 * DN_HEADS].reshape(nchunks, DN_CHUNK, DN_HEADS).transpose(0, 2, 1)
        y_b = dn_chunked(qkvn, proj_b, gb, grow, dn_norm3, i, B, S)

        qfull = mla_q(proj_small, mla_q_norm3, wq_packed, cos_tab, sin_tab, i)
        kfull, vfull = mla_kv(proj_small, mla_kv_norm3, wkv_packed, cos_tab, sin_tab, i)
        y_c = mla_flash(qfull, kfull, vfull, B, S)

        merged = merge_branches(y_a, y_b, y_c, gates, w_branch, i)
        j = i // 2
        if i % 2 == 0:
            xf, h2 = out_proj_residual(merged, w_out_bf, xf, norm_ffn3, i, True)
            y = ffn_swiglu(h2, wfg, wfu, wfd, j)
        else:
            (xf,) = out_proj_residual(merged, w_out_bf, xf, norm_ffn3, i, False)
            y = moe_layer(xf, norm_ffn3, w_router_pad, weg, weu, wed, i, j)

        if i + 1 < depth:
            xf, h = ple_layer(xf, y, pf, w_ple_gate_bf, w_ple_proj_bf, norm_ple3, norm_mix3, i, i + 1, BF16)
        else:
            xf, out = ple_layer(xf, y, pf, w_ple_gate_bf, w_ple_proj_bf, norm_ple3, final3, i, 0, F32)
    return out.reshape(B, S, D)
```
